```python
import math
import jax, jax.numpy as jnp
from jax import lax
import numpy as np

D_MODEL = 1024
BATCH = 16
SEQ = 4096
DEPTH = 4
DEC_BATCH = 32
DEC_SEQ = 16
PAST_LEN = 2048

CHUNK = 64
Q_BLOCK = 128
LAYER_MIXERS = tuple("ABC"[i % 3] for i in range(DEPTH))
MIXER_SLOT = tuple(LAYER_MIXERS[:i].count(LAYER_MIXERS[i]) for i in range(DEPTH))
N_A = LAYER_MIXERS.count("A")
N_B = LAYER_MIXERS.count("B")
N_C = LAYER_MIXERS.count("C")

H_A = 16
DH_A = D_MODEL // H_A
D_A = H_A * DH_A
FORGET_BIAS_INIT = 3.0

H_B = 16
DH_B = D_MODEL // H_B
HKV_B = 4
G_B = H_B // HKV_B
H_IDX = 8
D_IDX = 64
TOPK_MAX = 256
B_SPLITS = (H_B * DH_B, H_B * DH_B + HKV_B * DH_B, H_B * DH_B + 2 * HKV_B * DH_B,
            H_B * DH_B + 2 * HKV_B * DH_B + H_IDX * D_IDX,
            H_B * DH_B + 2 * HKV_B * DH_B + H_IDX * D_IDX + D_IDX)
B_IN_COLS = B_SPLITS[-1] + H_IDX

CONV_W = 3

D_FF = 2816
FF_CONV_W = 3

NUM_BUCKETS = 32
MAX_DISTANCE = 128

N_MEM = 256
XA_HEADS = 4
XA_DH = D_MODEL // XA_HEADS
XA_D = XA_HEADS * XA_DH

EPS = 1e-6

kernel_name = "hybrid_streaming_encoder_step"


def rmsnorm(x, g):
    xf = x.astype(jnp.float32)
    y = xf * lax.rsqrt(jnp.mean(xf * xf, axis=-1, keepdims=True) + EPS)
    return (y * g.astype(jnp.float32)).astype(x.dtype)


def causal_dwconv(x, w, prev):
    W = w.shape[0]
    T = x.shape[1]
    xp = jnp.concatenate([prev.astype(x.dtype), x], axis=1)
    y = w[0] * xp[:, 0:T]
    for k in range(1, W):
        y = y + w[k] * xp[:, k:k + T]
    return y, xp[:, T:]


def t5_bucket(rel):
    half = NUM_BUCKETS // 2
    ret = jnp.where(rel > 0, half, 0)
    n = jnp.abs(rel)
    max_exact = half // 2
    nf = jnp.maximum(n, 1).astype(jnp.float32)
    large = max_exact + (jnp.log(nf / max_exact) / math.log(MAX_DISTANCE / max_exact)
                         * (half - max_exact)).astype(jnp.int32)
    large = jnp.minimum(large, half - 1)
    return ret + jnp.where(n < max_exact, n, large)


def fox_project(h, w_in, b_f):
    B, T, _ = h.shape
    z = h @ w_in
    q, k, v, fl = jnp.split(z, [D_A, 2 * D_A, 3 * D_A], axis=-1)
    q = q.reshape(B, T, H_A, DH_A)
    k = k.reshape(B, T, H_A, DH_A)
    v = v.reshape(B, T, H_A, DH_A)
    logf = jax.nn.log_sigmoid((fl + b_f).astype(jnp.float32))
    return q, k, v, logf


def fox_attend(q, k, v, Fq, Fk, qpos, kpos):
    s = jnp.einsum("bqhd,bkhd->bhqk", q, k).astype(jnp.float32) * (DH_A ** -0.5)
    s = s + (jnp.swapaxes(Fq, 1, 2)[..., :, None] - jnp.swapaxes(Fk, 1, 2)[..., None, :])
    mask = kpos[None, :] <= qpos[:, None]
    s = jnp.where(mask[None, None], s, -jnp.inf)
    p = jax.nn.softmax(s, axis=-1).astype(v.dtype)
    return jnp.einsum("bhqk,bkhd->bqhd", p, v)


def fox_prompt(h, w_in, b_f, w_out):
    B, T, _ = h.shape
    q, k, v, logf = fox_project(h, w_in, b_f)
    F = jnp.cumsum(logf, axis=1)
    pos = jnp.arange(T, dtype=jnp.int32)

    def block(i):
        sl = lambda a: lax.dynamic_slice_in_dim(a, i * Q_BLOCK, Q_BLOCK, axis=1)
        qp = lax.dynamic_slice_in_dim(pos, i * Q_BLOCK, Q_BLOCK, axis=0)
        return fox_attend(sl(q), k, v, sl(F), F, qp, pos)

    o = lax.map(block, jnp.arange(T // Q_BLOCK))
    o = jnp.swapaxes(o, 0, 1).reshape(B, T, D_A)
    return o @ w_out, k, v, logf


def fox_sample(h, ck, cv, clogf, w_in, b_f, w_out):
    B, T, _ = h.shape
    P = ck.shape[1]
    q, k, v, logf = fox_project(h, w_in, b_f)
    F = jnp.cumsum(jnp.concatenate([clogf.astype(jnp.float32), logf], axis=1), axis=1)
    k_all = jnp.concatenate([ck.astype(k.dtype), k], axis=1)
    v_all = jnp.concatenate([cv.astype(v.dtype), v], axis=1)
    kpos = jnp.arange(P + T, dtype=jnp.int32)
    qpos = P + jnp.arange(T, dtype=jnp.int32)
    o = fox_attend(q, k_all, v_all, F[:, P:], F, qpos, kpos).reshape(B, T, D_A)
    return o @ w_out, k, v, logf


def dsa_project(h, w_in):
    B, T, _ = h.shape
    z = h @ w_in
    q, k, v, qi, ki, wi = jnp.split(z, list(B_SPLITS), axis=-1)
    q = q.reshape(B, T, H_B, DH_B)
    k = k.reshape(B, T, HKV_B, DH_B)
    v = v.reshape(B, T, HKV_B, DH_B)
    qi = qi.reshape(B, T, H_IDX, D_IDX)
    wi = wi * (H_IDX ** -0.5)
    return q, k, v, qi, ki, wi


def dsa_attend(q, qi, wi, qpos, k, v, ki, kpos, rel_bias, k_top):
    B, Tq = q.shape[0], q.shape[1]
    I = jnp.einsum("bqhd,bsd->bqhs", qi, ki).astype(jnp.float32) * (D_IDX ** -0.5)
    I = jnp.einsum("bqhs,bqh->bqs", jax.nn.relu(I), wi.astype(jnp.float32))
    adm = (kpos[None, :] // CHUNK) <= (qpos[:, None] // CHUNK)
    I = jnp.where(adm[None], I, -jnp.inf)
    _, idx = lax.top_k(I, k_top)
    gather = jax.vmap(lambda a, i: a[i])
    ks = gather(k, idx)
    vs = gather(v, idx)
    spos = kpos[idx]
    valid = (spos // CHUNK) <= (qpos[None, :, None] // CHUNK)
    qg = q.reshape(B, Tq, HKV_B, G_B, DH_B)
    s = jnp.einsum("bqgrd,bqkgd->bgrqk", qg, ks).astype(jnp.float32) * (DH_B ** -0.5)
    bias = rel_bias[t5_bucket(spos - qpos[None, :, None])].astype(jnp.float32)
    bias = bias.reshape(B, Tq, k_top, HKV_B, G_B).transpose(0, 3, 4, 1, 2)
    s = jnp.where(valid[:, None, None], s + bias, -jnp.inf)
    p = jax.nn.softmax(s, axis=-1).astype(vs.dtype)
    o = jnp.einsum("bgrqk,bqkgd->bqgrd", p, vs)
    return o.reshape(B, Tq, H_B * DH_B)


def dsa_prompt(h, w_in, w_out, rel_bias, k_top):
    B, T, _ = h.shape
    q, k, v, qi, ki, wi = dsa_project(h, w_in)
    pos = jnp.arange(T, dtype=jnp.int32)

    def block(i):
        sl = lambda a: lax.dynamic_slice_in_dim(a, i * Q_BLOCK, Q_BLOCK, axis=1)
        qp = lax.dynamic_slice_in_dim(pos, i * Q_BLOCK, Q_BLOCK, axis=0)
        return dsa_attend(sl(q), sl(qi), sl(wi), qp, k, v, ki, pos, rel_bias, k_top)

    o = lax.map(block, jnp.arange(T // Q_BLOCK))
    o = jnp.swapaxes(o, 0, 1).reshape(B, T, H_B * DH_B)
    return o @ w_out, k, v, ki


def dsa_sample(h, ck, cv, cki, w_in, w_out, rel_bias, k_top):
    B, T, _ = h.shape
    P = ck.shape[1]
    q, k, v, qi, ki, wi = dsa_project(h, w_in)
    k_all = jnp.concatenate([ck.astype(k.dtype), k], axis=1)
    v_all = jnp.concatenate([cv.astype(v.dtype), v], axis=1)
    ki_all = jnp.concatenate([cki.astype(ki.dtype), ki], axis=1)
    kpos = jnp.arange(P + T, dtype=jnp.int32)
    qpos = P + jnp.arange(T, dtype=jnp.int32)
    o = dsa_attend(q, qi, wi, qpos, k_all, v_all, ki_all, kpos, rel_bias, k_top)
    return o @ w_out, k, v, ki


def short_conv(h, prev, w_in, w_conv, w_out):
    z = h @ w_in
    gb, gc, u = jnp.split(z, 3, axis=-1)
    y, buf = causal_dwconv(gc * u, w_conv, prev)
    return (gb * y) @ w_out, buf


def mem_project(mem, g_mem, w_k, w_v):
    B, M, _ = mem.shape
    m = rmsnorm(mem, g_mem)
    return (m @ w_k).reshape(B, M, XA_HEADS, XA_DH), (m @ w_v).reshape(B, M, XA_HEADS, XA_DH)


def mem_attend(h, mk, mv, w_q, w_o):
    B, T, _ = h.shape
    q = (h @ w_q).reshape(B, T, XA_HEADS, XA_DH)
    s = jnp.einsum("bqhd,bmhd->bhqm", q, mk.astype(q.dtype)).astype(jnp.float32) * (XA_DH ** -0.5)
    p = jax.nn.softmax(s, axis=-1).astype(q.dtype)
    o = jnp.einsum("bhqm,bmhd->bqhd", p, mv.astype(q.dtype)).reshape(B, T, XA_D)
    return o @ w_o


def conv_ffn(h, prev, w_up, w_conv, b_conv, w_down):
    up = h @ w_up
    y, buf = causal_dwconv(up, w_conv, prev)
    g, u = jnp.split(y + b_conv, 2, axis=-1)
    return (jax.nn.silu(g) * u) @ w_down, buf


def setup_inputs(seed: int = 0) -> dict:
    key = jax.random.key(seed)
    ks = iter(jax.random.split(key, 48))
    nrm = lambda shape, scale=1.0: scale * jax.random.normal(next(ks), shape, jnp.float32)
    D = D_MODEL
    inp = {}
    inp["x_prompt"] = nrm((BATCH, SEQ, D))
    inp["x_sample"] = nrm((DEC_BATCH, DEC_SEQ, D))
    inp["mem_prompt"] = nrm((BATCH, N_MEM, D))
    inp["cache_k_A"] = nrm((N_A, DEC_BATCH, PAST_LEN, H_A, DH_A))
    inp["cache_v_A"] = nrm((N_A, DEC_BATCH, PAST_LEN, H_A, DH_A))
    inp["cache_logf_A"] = jax.nn.log_sigmoid(FORGET_BIAS_INIT + nrm((N_A, DEC_BATCH, PAST_LEN, H_A), 0.5))
    inp["cache_k_B"] = nrm((N_B, DEC_BATCH, PAST_LEN, HKV_B, DH_B))
    inp["cache_v_B"] = nrm((N_B, DEC_BATCH, PAST_LEN, HKV_B, DH_B))
    inp["cache_kidx_B"] = nrm((N_B, DEC_BATCH, PAST_LEN, D_IDX))
    inp["state_conv_C"] = nrm((N_C, DEC_BATCH, CONV_W - 1, D))
    inp["state_ffconv"] = nrm((DEPTH, DEC_BATCH, FF_CONV_W - 1, 2 * D_FF))
    inp["cache_mem_k"] = nrm((DEPTH, DEC_BATCH, N_MEM, XA_HEADS, XA_DH))
    inp["cache_mem_v"] = nrm((DEPTH, DEC_BATCH, N_MEM, XA_HEADS, XA_DH))
    inp["g_mix"] = 1.0 + nrm((DEPTH, D), 0.02)
    inp["g_xa"] = 1.0 + nrm((DEPTH, D), 0.02)
    inp["g_ffn"] = 1.0 + nrm((DEPTH, D), 0.02)
    inp["g_mem"] = 1.0 + nrm((DEPTH, D), 0.02)
    inp["g_final"] = 1.0 + nrm((D,), 0.02)
    inp["w_in_A"] = nrm((N_A, D, 3 * D_A + H_A), D ** -0.5)
    inp["b_f_A"] = FORGET_BIAS_INIT + nrm((N_A, H_A), 0.5)
    inp["w_out_A"] = nrm((N_A, D_A, D), D_A ** -0.5)
    inp["w_in_B"] = nrm((N_B, D, B_IN_COLS), D ** -0.5)
    inp["w_out_B"] = nrm((N_B, H_B * DH_B, D), (H_B * DH_B) ** -0.5)
    inp["rel_bias"] = nrm((NUM_BUCKETS, H_B), 0.5)
    inp["w_in_C"] = nrm((N_C, D, 3 * D), D ** -0.5)
    inp["w_conv_C"] = nrm((N_C, CONV_W, D), CONV_W ** -0.5)
    inp["w_out_C"] = nrm((N_C, D, D), D ** -0.5)
    inp["w_q_xa"] = nrm((DEPTH, D, XA_D), D ** -0.5)
    inp["w_k_xa"] = nrm((DEPTH, D, XA_D), D ** -0.5)
    inp["w_v_xa"] = nrm((DEPTH, D, XA_D), D ** -0.5)
    inp["w_o_xa"] = nrm((DEPTH, XA_D, D), XA_D ** -0.5)
    inp["w_up"] = nrm((DEPTH, D, 2 * D_FF), D ** -0.5)
    inp["w_conv_ff"] = nrm((DEPTH, FF_CONV_W, 2 * D_FF), FF_CONV_W ** -0.5)
    inp["b_conv_ff"] = nrm((DEPTH, 2 * D_FF), 0.02)
    inp["w_down"] = nrm((DEPTH, D_FF, D), D_FF ** -0.5)
    return inp


def reference(x_prompt, x_sample, mem_prompt, cache_k_A, cache_v_A, cache_logf_A,
              cache_k_B, cache_v_B, cache_kidx_B, state_conv_C, state_ffconv,
              cache_mem_k, cache_mem_v, g_mix, g_xa, g_ffn, g_mem, g_final,
              w_in_A, b_f_A, w_out_A, w_in_B, w_out_B, rel_bias,
              w_in_C, w_conv_C, w_out_C, w_q_xa, w_k_xa, w_v_xa, w_o_xa,
              w_up, w_conv_ff, b_conv_ff, w_down):
    xp, xs = x_prompt, x_sample
    Bp, Bs = xp.shape[0], xs.shape[0]
    k_top_p = min(TOPK_MAX, xp.shape[1] // 4)
    k_top_s = min(TOPK_MAX, (cache_k_B.shape[2] + xs.shape[1]) // 4)
    kA_p, vA_p, fA_p, kA_s, vA_s, fA_s = [], [], [], [], [], []
    kB_p, vB_p, iB_p, kB_s, vB_s, iB_s = [], [], [], [], [], []
    cC_p, cC_s, ff_p, ff_s, mk_p, mv_p = [], [], [], [], [], []
    for l in range(DEPTH):
        mix, j = LAYER_MIXERS[l], MIXER_SLOT[l]
        hp = rmsnorm(xp, g_mix[l])
        hs = rmsnorm(xs, g_mix[l])
        if mix == "A":
            yp, k, v, f = fox_prompt(hp, w_in_A[j], b_f_A[j], w_out_A[j])
            kA_p.append(k); vA_p.append(v); fA_p.append(f)
            ys, k, v, f = fox_sample(hs, cache_k_A[j], cache_v_A[j], cache_logf_A[j],
                                     w_in_A[j], b_f_A[j], w_out_A[j])
            kA_s.append(k); vA_s.append(v); fA_s.append(f)
        elif mix == "B":
            yp, k, v, ki = dsa_prompt(hp, w_in_B[j], w_out_B[j], rel_bias, k_top_p)
            kB_p.append(k); vB_p.append(v); iB_p.append(ki)
            ys, k, v, ki = dsa_sample(hs, cache_k_B[j], cache_v_B[j], cache_kidx_B[j],
                                      w_in_B[j], w_out_B[j], rel_bias, k_top_s)
            kB_s.append(k); vB_s.append(v); iB_s.append(ki)
        else:
            zero_c = jnp.zeros((Bp, CONV_W - 1, D_MODEL), xp.dtype)
            yp, buf = short_conv(hp, zero_c, w_in_C[j], w_conv_C[j], w_out_C[j])
            cC_p.append(buf)
            ys, buf = short_conv(hs, state_conv_C[j], w_in_C[j], w_conv_C[j], w_out_C[j])
            cC_s.append(buf)
        xp = xp + yp
        xs = xs + ys
        mk, mv = mem_project(mem_prompt, g_mem[l], w_k_xa[l], w_v_xa[l])
        mk_p.append(mk); mv_p.append(mv)
        xp = xp + mem_attend(rmsnorm(xp, g_xa[l]), mk, mv, w_q_xa[l], w_o_xa[l])
        xs = xs + mem_attend(rmsnorm(xs, g_xa[l]), cache_mem_k[l], cache_mem_v[l], w_q_xa[l], w_o_xa[l])
        zero_f = jnp.zeros((Bp, FF_CONV_W - 1, 2 * D_FF), xp.dtype)
        yp, buf = conv_ffn(rmsnorm(xp, g_ffn[l]), zero_f, w_up[l], w_conv_ff[l], b_conv_ff[l], w_down[l])
        ff_p.append(buf)
        xp = xp + yp
        ys, buf = conv_ffn(rmsnorm(xs, g_ffn[l]), state_ffconv[l], w_up[l], w_conv_ff[l], b_conv_ff[l], w_down[l])
        ff_s.append(buf)
        xs = xs + ys
    y_prompt = rmsnorm(xp, g_final)
    y_sample = rmsnorm(xs, g_final)
    return (y_prompt, y_sample,
            jnp.stack(kA_p), jnp.stack(vA_p), jnp.stack(fA_p),
            jnp.stack(kB_p), jnp.stack(vB_p), jnp.stack(iB_p),
            jnp.stack(cC_p), jnp.stack(ff_p), jnp.stack(mk_p), jnp.stack(mv_p),
            jnp.stack(kA_s), jnp.stack(vA_s), jnp.stack(fA_s),
            jnp.stack(kB_s), jnp.stack(vB_s), jnp.stack(iB_s),
            jnp.stack(cC_s), jnp.stack(ff_s))
```

```python
import functools
import math

import jax
import jax.numpy as jnp
from jax import lax
from jax.experimental import pallas as pl
from jax.experimental.pallas import tpu as pltpu

F32, BF16, I32 = jnp.float32, jnp.bfloat16, jnp.int32

CHUNK = 64
H_A = 16
H_B = 16
HKV_B = 4
G_B = H_B // HKV_B
H_IDX = 8
D_IDX = 64
TOPK_MAX = 256
NUM_BUCKETS = 32
MAX_DISTANCE = 128
XA_HEADS = 4
EPS = 1e-6

LANES = 128
SUBLANES = 8
HEAD_DIM = 64
KEY_TILE = 128
VMEM_LIMIT_BYTES = 56 * 1024 * 1024

INT_MIN = -2 ** 31
NEG_INF = float("-inf")

_NT = (((1,), (1,)), ((), ()))
_WHOLE = pl.BlockSpec(memory_space=pltpu.VMEM)


def _cp(*sem):
    return pltpu.CompilerParams(dimension_semantics=sem, vmem_limit_bytes=VMEM_LIMIT_BYTES)


def _rows(tm, n):
    return pl.BlockSpec((tm, n), lambda i: (i, 0))


def _rms(x, g):
    return x * lax.rsqrt(jnp.mean(x * x, axis=-1, keepdims=True) + EPS) * g


def _mm(a, b):
    return jnp.dot(a, b, preferred_element_type=F32)


def _mm_nt(a, b):
    return lax.dot_general(a, b, _NT, preferred_element_type=F32)


def _proj_kernel(x_ref, g_ref, *refs, n_w):
    w_refs, o_refs = refs[:n_w], refs[n_w:]
    h = _rms(x_ref[...], g_ref[...]).astype(BF16)
    for w_ref, o_ref in zip(w_refs, o_refs):
        o_ref[...] = _mm(h, w_ref[...])


def _proj(x, g, ws, tm):
    m, d = x.shape
    return pl.pallas_call(
        functools.partial(_proj_kernel, n_w=len(ws)),
        grid=(m // tm,),
        in_specs=[_rows(tm, d), _WHOLE] + [_WHOLE] * len(ws),
        out_specs=[_rows(tm, w.shape[1]) for w in ws],
        out_shape=[jax.ShapeDtypeStruct((m, w.shape[1]), F32) for w in ws],
        compiler_params=_cp("parallel"),
    )(x, g, *ws)


def _a_in_kernel(x_ref, g_ref, w_ref, wf_ref, bf_ref, q_ref, k_ref, v_ref, kb_ref, vb_ref, lf_ref):
    d = q_ref.shape[-1]
    h = _rms(x_ref[...], g_ref[...]).astype(BF16)
    q_ref[...] = (_mm(h, w_ref[:, 0:d]) * (HEAD_DIM ** -0.5)).astype(BF16)
    k = _mm(h, w_ref[:, d:2 * d])
    k_ref[...] = k
    kb_ref[...] = k.astype(BF16)
    v = _mm(h, w_ref[:, 2 * d:3 * d])
    v_ref[...] = v
    vb_ref[...] = v.astype(BF16)
    fl = _mm(h, wf_ref[...]) + bf_ref[...]
    lf_ref[...] = jnp.minimum(fl, 0.0) - jnp.log1p(jnp.exp(-jnp.abs(fl)))


def _a_in(x, g, w_qkv, w_f, b_f, tm):
    m, d = x.shape
    sds = jax.ShapeDtypeStruct
    return pl.pallas_call(
        _a_in_kernel,
        grid=(m // tm,),
        in_specs=[_rows(tm, d), _WHOLE, _WHOLE, _WHOLE, _WHOLE],
        out_specs=[_rows(tm, d)] * 5 + [_rows(tm, H_A)],
        out_shape=[sds((m, d), BF16), sds((m, d), F32), sds((m, d), F32),
                   sds((m, d), BF16), sds((m, d), BF16), sds((m, H_A), F32)],
        compiler_params=_cp("parallel"),
    )(x, g, w_qkv, w_f, b_f)


def _b_in_kernel(x_ref, g_ref, wq_ref, wkv_ref, wqi_ref, wki_ref, wwi_ref,
                 q_ref, k_ref, v_ref, kb_ref, vb_ref, qi_ref, ki_ref, kib_ref, wi_ref):
    dkv = k_ref.shape[-1]
    h = _rms(x_ref[...], g_ref[...]).astype(BF16)
    q_ref[...] = (_mm(h, wq_ref[...]) * (HEAD_DIM ** -0.5)).astype(BF16)
    kv = _mm(h, wkv_ref[...])
    k_ref[...] = kv[:, 0:dkv]
    kb_ref[...] = kv[:, 0:dkv].astype(BF16)
    v_ref[...] = kv[:, dkv:2 * dkv]
    vb_ref[...] = kv[:, dkv:2 * dkv].astype(BF16)
    qi_ref[...] = (_mm(h, wqi_ref[...]) * (D_IDX ** -0.5)).astype(BF16)
    ki2 = _mm(h, wki_ref[...])
    ki_ref[...] = ki2[:, 0:D_IDX]
    kib_ref[...] = ki2.astype(BF16)
    wi_ref[...] = _mm(h, wwi_ref[...]) * (H_IDX ** -0.5)


def _b_in(x, g, w_q, w_kv, w_qi, w_ki2, w_wi, tm):
    m, d = x.shape
    dkv = w_kv.shape[1] // 2
    dqi = w_qi.shape[1]
    sds = jax.ShapeDtypeStruct
    return pl.pallas_call(
        _b_in_kernel,
        grid=(m // tm,),
        in_specs=[_rows(tm, d)] + [_WHOLE] * 6,
        out_specs=[_rows(tm, d), _rows(tm, dkv), _rows(tm, dkv), _rows(tm, dkv), _rows(tm, dkv),
                   _rows(tm, dqi), _rows(tm, D_IDX), _rows(tm, 2 * D_IDX), _rows(tm, H_IDX)],
        out_shape=[sds((m, d), BF16), sds((m, dkv), F32), sds((m, dkv), F32),
                   sds((m, dkv), BF16), sds((m, dkv), BF16), sds((m, dqi), BF16),
                   sds((m, D_IDX), F32), sds((m, 2 * D_IDX), BF16), sds((m, H_IDX), F32)],
        compiler_params=_cp("parallel"),
    )(x, g, w_q, w_kv, w_qi, w_ki2, w_wi)


def _cumsum_kernel(x_ref, o_ref):
    n = x_ref.shape[-1]
    a = lax.broadcasted_iota(I32, (LANES, LANES), 0)
    b = lax.broadcasted_iota(I32, (LANES, LANES), 1)
    tri = (a <= b).astype(F32)
    carry = jnp.zeros((x_ref.shape[0], 1), F32)
    for c in range(n // LANES):
        sl = slice(c * LANES, (c + 1) * LANES)
        y = jnp.dot(x_ref[:, sl], tri, precision=lax.Precision.HIGHEST,
                    preferred_element_type=F32) + carry
        o_ref[:, sl] = y
        carry = y[:, LANES - 1:LANES]


def _cumsum_lanes(x, rb):
    r, n = x.shape
    return pl.pallas_call(
        _cumsum_kernel,
        grid=(r // rb,),
        in_specs=[_rows(rb, n)],
        out_specs=_rows(rb, n),
        out_shape=jax.ShapeDtypeStruct((r, n), F32),
        compiler_params=_cp("parallel"),
    )(x)


def _fox_kernel(q_ref, k_ref, v_ref, fq_ref, fk_ref, o_ref, m_scr, l_scr, acc_scr,
                *, tq, tk, q_off, nkt):
    i = pl.program_id(2)
    lane_q = lax.broadcasted_iota(I32, (tq, LANES), 1)
    lane_k = lax.broadcasted_iota(I32, (tk, LANES), 1)
    q = q_ref[0]
    zq = jnp.zeros_like(q)
    qh = (jnp.where(lane_q < HEAD_DIM, q, zq), jnp.where(lane_q >= HEAD_DIM, q, zq))
    fq = (fq_ref[0, 0, :, 0:1], fq_ref[0, 0, :, 1:2])
    m_scr[...] = jnp.full(m_scr.shape, NEG_INF, F32)
    l_scr[...] = jnp.zeros(l_scr.shape, F32)
    acc_scr[...] = jnp.zeros(acc_scr.shape, F32)
    q0 = q_off + i * tq
    qpos = q0 + lax.broadcasted_iota(I32, (tq, 1), 0)
    n_tiles = jnp.minimum((q0 + tq + tk - 1) // tk, nkt)

    def body(kk, carry):
        ks = k_ref[0, pl.ds(kk * tk, tk), :]
        vs = v_ref[0, pl.ds(kk * tk, tk), :]
        zv = jnp.zeros_like(vs)
        kpos = kk * tk + lax.broadcasted_iota(I32, (1, tk), 1)
        mask = kpos <= qpos
        alphas, pvs = [], []
        for e in range(2):
            s = _mm_nt(qh[e], ks)
            s = s + (fq[e] - fk_ref[0, e, pl.ds(kk, 1), :])
            s = jnp.where(mask, s, NEG_INF)
            m_old = m_scr[e]
            m_new = jnp.maximum(m_old, jnp.max(s, axis=1, keepdims=True))
            alpha = jnp.exp(m_old - m_new)
            p = jnp.exp(s - m_new)
            l_scr[e] = alpha * l_scr[e] + jnp.sum(p, axis=1, keepdims=True)
            m_scr[e] = m_new
            ve = jnp.where((lane_k >= HEAD_DIM) == (e == 1), vs, zv)
            pvs.append(_mm(p.astype(BF16), ve))
            alphas.append(alpha)
        acc_scr[...] = (acc_scr[...] * jnp.where(lane_q < HEAD_DIM, alphas[0], alphas[1])
                        + pvs[0] + pvs[1])
        return carry

    lax.fori_loop(0, n_tiles, body, 0)
    o_ref[0] = (acc_scr[...] / jnp.where(lane_q < HEAD_DIM, l_scr[0], l_scr[1])).astype(BF16)


def _fox_attention(q, k, v, fq, fk, *, tq, tk, q_off):
    b, t_q, d = q.shape
    t_k = k.shape[1]
    nkt = t_k // tk
    kern = functools.partial(_fox_kernel, tq=tq, tk=tk, q_off=q_off, nkt=nkt)
    return pl.pallas_call(
        kern,
        grid=(b, d // LANES, t_q // tq),
        in_specs=[pl.BlockSpec((1, tq, LANES), lambda bb, j, i: (bb, i, j)),
                  pl.BlockSpec((1, t_k, LANES), lambda bb, j, i: (bb, 0, j)),
                  pl.BlockSpec((1, t_k, LANES), lambda bb, j, i: (bb, 0, j)),
                  pl.BlockSpec((1, 1, tq, 2), lambda bb, j, i: (bb, j, i, 0)),
                  pl.BlockSpec((1, 2, nkt, tk), lambda bb, j, i: (bb, j, 0, 0))],
        out_specs=pl.BlockSpec((1, tq, LANES), lambda bb, j, i: (bb, i, j)),
        out_shape=jax.ShapeDtypeStruct((b, t_q, d), BF16),
        scratch_shapes=[pltpu.VMEM((2, tq, 1), F32), pltpu.VMEM((2, tq, 1), F32),
                        pltpu.VMEM((tq, LANES), F32)],
        compiler_params=_cp("parallel", "parallel", "arbitrary"),
    )(q, k, v, fq, fk)


def _dsa_kernel(q_ref, qi_ref, wi_ref, k_ref, v_ref, ki_ref, bias_ref, o_ref,
                keys_scr, mb_scr, qim_scr, wib_scr, m_scr, l_scr, acc_scr,
                *, tq, q_off, k_top, tk_valid):
    i = pl.program_id(1)
    q0 = q_off + i * tq
    home = q0 // KEY_TILE
    qpos = q0 + lax.broadcasted_iota(I32, (tq, 1), 0)
    qchunk = lax.shift_right_logical(qpos, int(math.log2(CHUNK)))
    lane_q = lax.broadcasted_iota(I32, (tq, LANES), 1)
    lane_k = lax.broadcasted_iota(I32, (KEY_TILE, LANES), 1)
    kf = float(k_top)

    wi = wi_ref[0]
    for h in range(H_IDX):
        slab = qi_ref[0, :, (h // 2) * LANES:(h // 2 + 1) * LANES]
        qim_scr[h] = jnp.where((lane_q >= HEAD_DIM) == (h % 2 == 1), slab, jnp.zeros_like(slab))
        wib_scr[h] = jnp.broadcast_to(wi[:, h:h + 1], (tq, LANES))

    def score_body(kt, carry):
        kit = ki_ref[0, pl.ds(kt * KEY_TILE, KEY_TILE), :]
        acc = jnp.zeros((tq, KEY_TILE), F32)
        for h in range(H_IDX):
            acc = acc + jnp.maximum(_mm_nt(qim_scr[h], kit), 0.0) * wib_scr[h]
        acc = jnp.where(acc == 0.0, 0.0, acc)
        bits = lax.bitcast_convert_type(acc, I32)
        key = bits ^ (lax.shift_right_arithmetic(bits, 31) & 0x7FFFFFFF)
        kpos = kt * KEY_TILE + lax.broadcasted_iota(I32, (1, KEY_TILE), 1)
        adm = (lax.shift_right_logical(kpos, int(math.log2(CHUNK))) <= qchunk) & (kpos < tk_valid)
        keys_scr[kt] = jnp.where(adm, key, INT_MIN)
        return carry

    lax.fori_loop(0, home + 1, score_body, 0)

    def count(cand, strict):
        cb = jnp.broadcast_to(cand, (tq, LANES))

        def body(kt, a):
            key = keys_scr[kt]
            hit = (key > cb) if strict else (key >= cb)
            return a + jnp.where(hit, 1.0, 0.0)

        a = lax.fori_loop(0, home + 1, body, jnp.zeros((tq, LANES), F32))
        return jnp.sum(a, axis=1, keepdims=True)

    zero = jnp.zeros((tq, 1), I32)
    thr = jnp.where(count(zero, False) >= kf, zero, jnp.full((tq, 1), INT_MIN, I32))

    def bit_body(b, t):
        cand = t + lax.shift_left(jnp.int32(1), 30 - b)
        return jnp.where(count(cand, False) >= kf, cand, t)

    thr = lax.fori_loop(0, 31, bit_body, thr)

    need = kf - count(thr, True)
    tb = jnp.broadcast_to(thr, (tq, LANES))
    ra = lax.broadcasted_iota(I32, (KEY_TILE, KEY_TILE), 0)
    rb = lax.broadcasted_iota(I32, (KEY_TILE, KEY_TILE), 1)
    before = (ra < rb).astype(BF16)
    ones = jnp.ones((KEY_TILE, KEY_TILE), BF16)

    def tie_body(kt, seen):
        key = keys_scr[kt]
        eq = key == tb
        eqb = jnp.where(eq, 1.0, 0.0).astype(BF16)
        rank = _mm(eqb, before) + seen
        sel = ((key > tb) | (eq & (rank < need))) & (key != INT_MIN)
        mb_scr[kt] = jnp.where(sel, 0.0, NEG_INF)
        return seen + _mm(eqb, ones)

    lax.fori_loop(0, home + 1, tie_body, jnp.zeros((tq, LANES), F32))

    m_scr[...] = jnp.full(m_scr.shape, NEG_INF, F32)
    l_scr[...] = jnp.zeros(l_scr.shape, F32)
    acc_scr[...] = jnp.zeros(acc_scr.shape, F32)

    def attend(kt, near):
        mb = mb_scr[kt]
        for g in range(HKV_B):
            cols = slice((g // 2) * LANES, (g // 2 + 1) * LANES)
            ks = k_ref[0, pl.ds(kt * KEY_TILE, KEY_TILE), cols]
            vs = v_ref[0, pl.ds(kt * KEY_TILE, KEY_TILE), cols]
            half = (lane_k >= HEAD_DIM) == (g % 2 == 1)
            ks = jnp.where(half, ks, jnp.zeros_like(ks))
            vs = jnp.where(half, vs, jnp.zeros_like(vs))
            for r in range(G_B):
                hq = g * G_B + r
                slab = (g // 2) * G_B + r
                s = _mm_nt(q_ref[0, :, slab * LANES:(slab + 1) * LANES], ks) + mb
                if near is not None:
                    s = s + bias_ref[hq, near]
                m_old = m_scr[hq]
                m_new = jnp.maximum(m_old, jnp.max(s, axis=1, keepdims=True))
                m_use = jnp.where(m_new == NEG_INF, 0.0, m_new)
                alpha = jnp.exp(m_old - m_use)
                p = jnp.exp(s - m_use)
                l_scr[hq] = alpha * l_scr[hq] + jnp.sum(p, axis=1, keepdims=True)
                m_scr[hq] = m_new
                acc_scr[hq] = acc_scr[hq] * alpha + _mm(p.astype(BF16), vs)

    def far_body(kt, carry):
        attend(kt, None)
        return carry

    lax.fori_loop(0, jnp.maximum(home - 1, 0), far_body, 0)

    @pl.when(home >= 1)
    def _():
        attend(home - 1, 0)

    attend(home, 1)

    for slab in range(H_B // 2):
        r = slab % G_B
        h0 = (2 * (slab // G_B)) * G_B + r
        h1 = (2 * (slab // G_B) + 1) * G_B + r
        o_ref[0, :, slab * LANES:(slab + 1) * LANES] = (
            acc_scr[h0] / l_scr[h0] + acc_scr[h1] / l_scr[h1]).astype(BF16)


def _dsa_attention(q, qi, wi, k, v, ki, bias, *, tq, q_off, k_top, tk_valid):
    b, t_q, d = q.shape
    t_k = k.shape[1]
    nkt = t_k // KEY_TILE
    kern = functools.partial(_dsa_kernel, tq=tq, q_off=q_off, k_top=k_top, tk_valid=tk_valid)
    qblk = lambda n: pl.BlockSpec((1, tq, n), lambda bb, i: (bb, i, 0))
    kblk = lambda n: pl.BlockSpec((1, t_k, n), lambda bb, i: (bb, 0, 0))
    return pl.pallas_call(
        kern,
        grid=(b, t_q // tq),
        in_specs=[qblk(d), qblk(qi.shape[2]), qblk(H_IDX),
                  kblk(k.shape[2]), kblk(v.shape[2]), kblk(ki.shape[2]), _WHOLE],
        out_specs=qblk(d),
        out_shape=jax.ShapeDtypeStruct((b, t_q, d), BF16),
        scratch_shapes=[pltpu.VMEM((nkt, tq, KEY_TILE), I32), pltpu.VMEM((nkt, tq, KEY_TILE), F32),
                        pltpu.VMEM((H_IDX, tq, LANES), BF16), pltpu.VMEM((H_IDX, tq, LANES), F32),
                        pltpu.VMEM((H_B, tq, 1), F32), pltpu.VMEM((H_B, tq, 1), F32),
                        pltpu.VMEM((H_B, tq, LANES), F32)],
        compiler_params=_cp("parallel", "arbitrary"),
    )(q, qi, wi, k, v, ki, bias)


def _conv3(u, w_ref, cols, prev):
    tm = u.shape[0]
    row = lax.broadcasted_iota(I32, (tm, 1), 0)
    r1 = pltpu.roll(u, 1, 0)
    r2 = pltpu.roll(u, 2, 0)
    if prev[0] == "stream":
        carry = prev[1]
        c6, c7 = carry[6:7, :], carry[7:8, :]
        um1 = jnp.where(row == 0, c7, r1)
        um2 = jnp.where(row == 0, c6, jnp.where(row == 1, c7, r2))
    else:
        _, seq, pm1, pm2 = prev
        t = row % seq
        um1 = jnp.where(t == 0, pm1, r1)
        um2 = jnp.where(t < 2, pm2, r2)
    return w_ref[0:1, cols] * um2 + w_ref[1:2, cols] * um1 + w_ref[2:3, cols] * u


def _col_chunks(n, width):
    out, c = [], 0
    while c < n:
        out.append((c, min(width, n - c)))
        c += width
    return out


def _ffn_kernel(*refs, mode, seq, tpb, dff, cw):
    if mode == "stream":
        x_ref, g_ref, wup_ref, wc_ref, bc_ref, wdn_ref, o_ref, tail_ref, carry_scr = refs
    else:
        x_ref, g_ref, wup_ref, wc_ref, bc_ref, wdn_ref, pm1_ref, pm2_ref, o_ref, up_ref = refs
    x = x_ref[...]
    tm = x.shape[0]
    h = _rms(x, g_ref[...]).astype(BF16)
    if mode == "stream":
        @pl.when(pl.program_id(0) % tpb == 0)
        def _():
            carry_scr[...] = jnp.zeros(carry_scr.shape, F32)
    acc = x
    for c0, w in _col_chunks(dff, cw):
        ys = []
        for base in (0, dff):
            cols = slice(base + c0, base + c0 + w)
            up = _mm(h, wup_ref[:, cols])
            if mode == "stream":
                y = _conv3(up, wc_ref, cols, ("stream", carry_scr[:, cols]))
                carry_scr[:, cols] = up[tm - SUBLANES:tm, :]
                tail_ref[0, :, cols] = up[tm - SUBLANES:tm, :]
            else:
                y = _conv3(up, wc_ref, cols, ("seq", seq, pm1_ref[:, cols], pm2_ref[:, cols]))
                up_ref[:, cols] = up
            ys.append(y + bc_ref[:, cols])
        gate, val = ys
        act = (gate / (1.0 + jnp.exp(-gate))) * val
        acc = acc + _mm(act.astype(BF16), wdn_ref[c0:c0 + w, :])
    o_ref[...] = acc


def _ffn_stream(x, g, w_up, w_conv, b_conv, w_down, *, tm, t_len):
    m, d = x.shape
    c2 = w_up.shape[1]
    tpb = t_len // tm
    kern = functools.partial(_ffn_kernel, mode="stream", seq=None, tpb=tpb, dff=c2 // 2, cw=512)
    return pl.pallas_call(
        kern,
        grid=(m // tm,),
        in_specs=[_rows(tm, d)] + [_WHOLE] * 5,
        out_specs=[_rows(tm, d), pl.BlockSpec((1, SUBLANES, c2), lambda i: (i // tpb, 0, 0))],
        out_shape=[jax.ShapeDtypeStruct((m, d), F32),
                   jax.ShapeDtypeStruct((m // t_len, SUBLANES, c2), F32)],
        scratch_shapes=[pltpu.VMEM((SUBLANES, c2), F32)],
        compiler_params=_cp("arbitrary"),
    )(x, g, w_up, w_conv, b_conv, w_down)


def _ffn_seq(x, g, w_up, w_conv, b_conv, w_down, pm1, pm2, *, tm, seq):
    m, d = x.shape
    c2 = w_up.shape[1]
    kern = functools.partial(_ffn_kernel, mode="seq", seq=seq, tpb=None, dff=c2 // 2, cw=512)
    return pl.pallas_call(
        kern,
        grid=(m // tm,),
        in_specs=[_rows(tm, d)] + [_WHOLE] * 5 + [_rows(tm, c2), _rows(tm, c2)],
        out_specs=[_rows(tm, d), _rows(tm, c2)],
        out_shape=[jax.ShapeDtypeStruct((m, d), F32), jax.ShapeDtypeStruct((m, c2), F32)],
        compiler_params=_cp("parallel"),
    )(x, g, w_up, w_conv, b_conv, w_down, pm1, pm2)


def _sconv_kernel(*refs, mode, seq, tpb, cw):
    if mode == "stream":
        x_ref, g_ref, win_ref, wc_ref, wout_ref, o_ref, tail_ref, carry_scr = refs
    else:
        x_ref, g_ref, win_ref, wc_ref, wout_ref, pm1_ref, pm2_ref, o_ref, p_ref = refs
    x = x_ref[...]
    tm, d = x.shape
    h = _rms(x, g_ref[...]).astype(BF16)
    if mode == "stream":
        @pl.when(pl.program_id(0) % tpb == 0)
        def _():
            carry_scr[...] = jnp.zeros(carry_scr.shape, F32)
    acc = x
    for c0, w in _col_chunks(d, cw):
        cols = slice(c0, c0 + w)
        gb = _mm(h, win_ref[:, c0:c0 + w])
        gc = _mm(h, win_ref[:, d + c0:d + c0 + w])
        u = _mm(h, win_ref[:, 2 * d + c0:2 * d + c0 + w])
        p = gc * u
        if mode == "stream":
            y = _conv3(p, wc_ref, cols, ("stream", carry_scr[:, cols]))
            carry_scr[:, cols] = p[tm - SUBLANES:tm, :]
            tail_ref[0, :, cols] = p[tm - SUBLANES:tm, :]
        else:
            y = _conv3(p, wc_ref, cols, ("seq", seq, pm1_ref[:, cols], pm2_ref[:, cols]))
            p_ref[:, cols] = p
        acc = acc + _mm((gb * y).astype(BF16), wout_ref[c0:c0 + w, :])
    o_ref[...] = acc


def _sconv_stream(x, g, w_in, w_conv, w_out, *, tm, t_len):
    m, d = x.shape
    tpb = t_len // tm
    kern = functools.partial(_sconv_kernel, mode="stream", seq=None, tpb=tpb, cw=512)
    return pl.pallas_call(
        kern,
        grid=(m // tm,),
        in_specs=[_rows(tm, d)] + [_WHOLE] * 4,
        out_specs=[_rows(tm, d), pl.BlockSpec((1, SUBLANES, d), lambda i: (i // tpb, 0, 0))],
        out_shape=[jax.ShapeDtypeStruct((m, d), F32),
                   jax.ShapeDtypeStruct((m // t_len, SUBLANES, d), F32)],
        scratch_shapes=[pltpu.VMEM((SUBLANES, d), F32)],
        compiler_params=_cp("arbitrary"),
    )(x, g, w_in, w_conv, w_out)


def _sconv_seq(x, g, w_in, w_conv, w_out, pm1, pm2, *, tm, seq):
    m, d = x.shape
    kern = functools.partial(_sconv_kernel, mode="seq", seq=seq, tpb=None, cw=512)
    return pl.pallas_call(
        kern,
        grid=(m // tm,),
        in_specs=[_rows(tm, d)] + [_WHOLE] * 4 + [_rows(tm, d), _rows(tm, d)],
        out_specs=[_rows(tm, d), _rows(tm, d)],
        out_shape=[jax.ShapeDtypeStruct((m, d), F32), jax.ShapeDtypeStruct((m, d), F32)],
        compiler_params=_cp("parallel"),
    )(x, g, w_in, w_conv, w_out, pm1, pm2)


def _post_kernel(*refs, nb, has_mix):
    if has_mix:
        x_ref, a_ref, wmix_ref, g_ref, wq_ref, mk_ref, mv_ref, wo_ref, o_ref, oc_scr = refs
        x = x_ref[...] + _mm(a_ref[...], wmix_ref[...])
    else:
        x_ref, g_ref, wq_ref, mk_ref, mv_ref, wo_ref, o_ref, oc_scr = refs
        x = x_ref[...]
    tm, d = x.shape
    dh = d // XA_HEADS
    rpb = tm // nb
    h = _rms(x, g_ref[...]).astype(BF16)
    q = (_mm(h, wq_ref[...]) * (dh ** -0.5)).astype(BF16)
    for b in range(nb):
        rows = slice(b * rpb, (b + 1) * rpb)
        for hh in range(XA_HEADS):
            cols = slice(hh * dh, (hh + 1) * dh)
            s = _mm_nt(q[rows, cols], mk_ref[b, :, cols].astype(BF16))
            e = jnp.exp(s - jnp.max(s, axis=1, keepdims=True))
            p = e / jnp.sum(e, axis=1, keepdims=True)
            oc_scr[rows, cols] = _mm(p.astype(BF16), mv_ref[b, :, cols].astype(BF16)).astype(BF16)
    o_ref[...] = x + _mm(oc_scr[...], wo_ref[...])


def _post(x, a, w_mix, g, w_q, mk, mv, w_o, *, tm, nb, tiles_per_mem):
    m, d = x.shape
    n_mem = mk.shape[1]
    has_mix = a is not None
    mem_spec = pl.BlockSpec((nb, n_mem, d), lambda i: (i // tiles_per_mem, 0, 0))
    ins = [x] + ([a, w_mix] if has_mix else []) + [g, w_q, mk, mv, w_o]
    specs = ([_rows(tm, d)] + ([_rows(tm, d), _WHOLE] if has_mix else [])
             + [_WHOLE, _WHOLE, mem_spec, mem_spec, _WHOLE])
    return pl.pallas_call(
        functools.partial(_post_kernel, nb=nb, has_mix=has_mix),
        grid=(m // tm,),
        in_specs=specs,
        out_specs=_rows(tm, d),
        out_shape=jax.ShapeDtypeStruct((m, d), F32),
        scratch_shapes=[pltpu.VMEM((tm, d), BF16)],
        compiler_params=_cp("parallel"),
    )(*ins)


def _norm_kernel(x_ref, g_ref, o_ref):
    o_ref[...] = _rms(x_ref[...], g_ref[...])


def _final_norm(x, g, tm):
    m, d = x.shape
    return pl.pallas_call(
        _norm_kernel, grid=(m // tm,), in_specs=[_rows(tm, d), _WHOLE], out_specs=_rows(tm, d),
        out_shape=jax.ShapeDtypeStruct((m, d), F32), compiler_params=_cp("parallel"),
    )(x, g)


def _t5_bucket(rel):
    half = NUM_BUCKETS // 2
    ret = jnp.where(rel > 0, half, 0)
    n = jnp.abs(rel)
    max_exact = half // 2
    nf = jnp.maximum(n, 1).astype(jnp.float32)
    large = max_exact + (jnp.log(nf / max_exact) / math.log(MAX_DISTANCE / max_exact)
                         * (half - max_exact)).astype(jnp.int32)
    large = jnp.minimum(large, half - 1)
    return ret + jnp.where(n < max_exact, n, large)


def _near_bias(rel_bias, tq):
    ql = jnp.arange(tq, dtype=jnp.int32)[None, :, None]
    sl = jnp.arange(KEY_TILE, dtype=jnp.int32)[None, None, :]
    off = jnp.array([-KEY_TILE, 0], jnp.int32)[:, None, None]
    rel = sl + off - ql
    far = _t5_bucket(jnp.array(-2 * KEY_TILE, jnp.int32))
    tab = rel_bias[_t5_bucket(rel)] - rel_bias[far]
    return jnp.transpose(tab, (3, 0, 1, 2)).astype(F32)


def _dsa_head_order():
    order = []
    for slab in range(H_B // 2):
        for e in range(2):
            order.append((2 * (slab // G_B) + e) * G_B + slab % G_B)
    return order


def _permute_heads(w, order, axis):
    idx = jnp.concatenate([jnp.arange(h * HEAD_DIM, (h + 1) * HEAD_DIM) for h in order])
    return jnp.take(w, idx, axis=axis)


def _seq_context(prev, seq):
    b, _, c = prev.shape
    z = jnp.zeros((b, seq, c), prev.dtype)
    pm1 = z.at[:, 0].set(prev[:, 1])
    pm2 = z.at[:, 0].set(prev[:, 0]).at[:, 1].set(prev[:, 1])
    return pm1.reshape(b * seq, c), pm2.reshape(b * seq, c)


def _pad_keys(cache, new, t_pad):
    b, p, c = cache.shape
    t = new.shape[1]
    pad = jnp.zeros((b, t_pad - p - t, c), BF16)
    return jnp.concatenate([cache.astype(BF16), new.astype(BF16), pad], axis=1)


def kernel(x_prompt, x_sample, mem_prompt, cache_k_A, cache_v_A, cache_logf_A, cache_k_B, cache_v_B,
           cache_kidx_B, state_conv_C, state_ffconv, cache_mem_k, cache_mem_v, g_mix, g_xa, g_ffn,
           g_mem, g_final, w_in_A, b_f_A, w_out_A, w_in_B, w_out_B, rel_bias, w_in_C, w_conv_C,
           w_out_C, w_q_xa, w_k_xa, w_v_xa, w_o_xa, w_up, w_conv_ff, b_conv_ff, w_down):
    bp, tp, d = x_prompt.shape
    bs, ts, _ = x_sample.shape
    depth = g_mix.shape[0]
    past = cache_k_A.shape[2]
    n_mem = mem_prompt.shape[1]
    assert d == H_A * HEAD_DIM == H_B * HEAD_DIM and ts % SUBLANES == 0 and past % KEY_TILE == 0
    mixers = tuple("ABC"[l % 3] for l in range(depth))
    slot = tuple(mixers[:l].count(mixers[l]) for l in range(depth))
    mp, ms = bp * tp, bs * ts
    tm_p = min(256, tp)
    tm_s = min(ms, 8 * ts)
    tq_a = min(512, tp)
    tk_a = min(256, tp)
    t_all = past + ts
    t_pad = -(-t_all // KEY_TILE) * KEY_TILE
    k_top_p = min(TOPK_MAX, tp // 4)
    k_top_s = min(TOPK_MAX, t_all // 4)
    nb_s = min(4, bs)

    bf = lambda a: a.astype(BF16)
    row = lambda a: a.reshape(1, -1)
    xp = x_prompt.reshape(mp, d)
    xs = x_sample.reshape(ms, d)
    mem = mem_prompt.reshape(bp * n_mem, d)

    order = _dsa_head_order()
    bias_p = _near_bias(rel_bias, min(KEY_TILE, tp))
    bias_s = _near_bias(rel_bias, ts)

    outs = {n: [] for n in ("kA_p", "vA_p", "fA_p", "kB_p", "vB_p", "iB_p", "cC_p", "ff_p", "mk_p",
                            "mv_p", "kA_s", "vA_s", "fA_s", "kB_s", "vB_s", "iB_s", "cC_s", "ff_s")}

    def fox(x, g, j, b, t, cache):
        m = b * t
        q, k, v, kb, vb, lf = _a_in(x, g, bf(w_in_A[j][:, :3 * d]), bf(w_in_A[j][:, 3 * d:]),
                                    row(b_f_A[j]), min(256, m))
        lf_t = jnp.swapaxes(lf.reshape(b, t, H_A), 1, 2)
        if cache is None:
            q_off, t_k, tq, tk = 0, t, tq_a, tk_a
            lf_all = lf_t
            k_all, v_all = kb.reshape(b, t, d), vb.reshape(b, t, d)
        else:
            ck, cv, clf = cache
            q_off, t_k, tq, tk = past, t_pad, t, KEY_TILE
            lf_all = jnp.concatenate([jnp.swapaxes(clf, 1, 2).astype(F32), lf_t,
                                      jnp.zeros((b, H_A, t_pad - t_all), F32)], axis=2)
            k_all = _pad_keys(ck.reshape(b, past, d), kb.reshape(b, t, d), t_pad)
            v_all = _pad_keys(cv.reshape(b, past, d), vb.reshape(b, t, d), t_pad)
        f_all = _cumsum_lanes(lf_all.reshape(b * H_A, t_k), min(64, b * H_A)).reshape(b, H_A, t_k)
        f_q = f_all[:, :, q_off:q_off + t]
        f_q = jnp.swapaxes(f_q.reshape(b, H_A // 2, 2, t), 2, 3)
        f_k = f_all.reshape(b, H_A, t_k // tk, tk)
        o = _fox_attention(q.reshape(b, t, d), k_all, v_all, f_q, f_k, tq=tq, tk=tk, q_off=q_off)
        return o.reshape(m, d), k, v, lf

    def dsa(x, g, j, b, t, cache):
        m = b * t
        w = w_in_B[j]
        dkv = HKV_B * HEAD_DIM
        c0, c1, c2, c3 = d, d + dkv, d + 2 * dkv, d + 2 * dkv + H_IDX * D_IDX
        q, k, v, kb, vb, qi, ki, kib, wi = _b_in(
            x, g, bf(_permute_heads(w[:, :c0], order, 1)), bf(w[:, c0:c2]), bf(w[:, c2:c3]),
            bf(jnp.concatenate([w[:, c3:c3 + D_IDX]] * 2, axis=1)), bf(w[:, c3 + D_IDX:]), min(256, m))
        if cache is None:
            q_off, tq, k_top, tk_valid = 0, min(KEY_TILE, t), k_top_p, t
            k_all, v_all, ki_all = kb.reshape(b, t, dkv), vb.reshape(b, t, dkv), kib.reshape(b, t, -1)
            bias = bias_p
        else:
            ck, cv, cki = cache
            q_off, tq, k_top, tk_valid = past, t, k_top_s, t_all
            k_all = _pad_keys(ck.reshape(b, past, dkv), kb.reshape(b, t, dkv), t_pad)
            v_all = _pad_keys(cv.reshape(b, past, dkv), vb.reshape(b, t, dkv), t_pad)
            cki2 = jnp.concatenate([cki, cki], axis=-1)
            ki_all = _pad_keys(cki2, kib.reshape(b, t, -1), t_pad)
            bias = bias_s
        o = _dsa_attention(q.reshape(b, t, d), qi.reshape(b, t, -1), wi.reshape(b, t, H_IDX),
                           k_all, v_all, ki_all, bias, tq=tq, q_off=q_off, k_top=k_top,
                           tk_valid=tk_valid)
        return o.reshape(m, d), k, v, ki

    for l in range(depth):
        mix, j = mixers[l], slot[l]
        gm, gx, gf = row(g_mix[l]), row(g_xa[l]), row(g_ffn[l])
        mk, mv = _proj(mem, row(g_mem[l]), [bf(w_k_xa[l]), bf(w_v_xa[l])], min(256, bp * n_mem))
        outs["mk_p"].append(mk.reshape(bp, n_mem, XA_HEADS, d // XA_HEADS))
        outs["mv_p"].append(mv.reshape(bp, n_mem, XA_HEADS, d // XA_HEADS))
        mks, mvs = cache_mem_k[l].reshape(bs, n_mem, d), cache_mem_v[l].reshape(bs, n_mem, d)
        post = dict(g=gx, w_q=bf(w_q_xa[l]), w_o=bf(w_o_xa[l]))

        if mix == "A":
            ap, k, v, f = fox(xp, gm, j, bp, tp, None)
            outs["kA_p"].append(k.reshape(bp, tp, H_A, HEAD_DIM))
            outs["vA_p"].append(v.reshape(bp, tp, H_A, HEAD_DIM))
            outs["fA_p"].append(f.reshape(bp, tp, H_A))
            a_s, k, v, f = fox(xs, gm, j, bs, ts, (cache_k_A[j], cache_v_A[j], cache_logf_A[j]))
            outs["kA_s"].append(k.reshape(bs, ts, H_A, HEAD_DIM))
            outs["vA_s"].append(v.reshape(bs, ts, H_A, HEAD_DIM))
            outs["fA_s"].append(f.reshape(bs, ts, H_A))
            w_mix = bf(w_out_A[j])
        elif mix == "B":
            ap, k, v, ki = dsa(xp, gm, j, bp, tp, None)
            outs["kB_p"].append(k.reshape(bp, tp, HKV_B, HEAD_DIM))
            outs["vB_p"].append(v.reshape(bp, tp, HKV_B, HEAD_DIM))
            outs["iB_p"].append(ki.reshape(bp, tp, D_IDX))
            a_s, k, v, ki = dsa(xs, gm, j, bs, ts, (cache_k_B[j], cache_v_B[j], cache_kidx_B[j]))
            outs["kB_s"].append(k.reshape(bs, ts, HKV_B, HEAD_DIM))
            outs["vB_s"].append(v.reshape(bs, ts, HKV_B, HEAD_DIM))
            outs["iB_s"].append(ki.reshape(bs, ts, D_IDX))
            w_mix = bf(_permute_heads(w_out_B[j], order, 0))
        else:
            xp, tail = _sconv_stream(xp, gm, bf(w_in_C[j]), w_conv_C[j], bf(w_out_C[j]),
                                     tm=tm_p, t_len=tp)
            outs["cC_p"].append(tail[:, SUBLANES - 2:, :])
            pm1, pm2 = _seq_context(state_conv_C[j], ts)
            xs, p_all = _sconv_seq(xs, gm, bf(w_in_C[j]), w_conv_C[j], bf(w_out_C[j]), pm1, pm2,
                                   tm=tm_s, seq=ts)
            outs["cC_s"].append(p_all.reshape(bs, ts, d)[:, ts - 2:, :])
            ap = a_s = w_mix = None

        xp = _post(xp, ap, w_mix, mk=mk.reshape(bp, n_mem, d), mv=mv.reshape(bp, n_mem, d),
                   tm=tm_p, nb=1, tiles_per_mem=tp // tm_p, **post)
        xs = _post(xs, a_s, w_mix, mk=mks, mv=mvs, tm=nb_s * ts, nb=nb_s, tiles_per_mem=1, **post)

        wu, wc, bc, wd = bf(w_up[l]), w_conv_ff[l], row(b_conv_ff[l]), bf(w_down[l])
        xp, tail = _ffn_stream(xp, gf, wu, wc, bc, wd, tm=tm_p, t_len=tp)
        outs["ff_p"].append(tail[:, SUBLANES - 2:, :])
        pm1, pm2 = _seq_context(state_ffconv[l], ts)
        xs, up_all = _ffn_seq(xs, gf, wu, wc, bc, wd, pm1, pm2, tm=tm_s, seq=ts)
        outs["ff_s"].append(up_all.reshape(bs, ts, -1)[:, ts - 2:, :])

    y_p = _final_norm(xp, row(g_final), tm_p).reshape(bp, tp, d)
    y_s = _final_norm(xs, row(g_final), min(256, ms)).reshape(bs, ts, d)
    st = lambda n: jnp.stack(outs[n])
    return (y_p, y_s, st("kA_p"), st("vA_p"), st("fA_p"), st("kB_p"), st("vB_p"), st("iB_p"),
            st("cC_p"), st("ff_p"), st("mk_p"), st("mv_p"), st("kA_s"), st("vA_s"), st("fA_s"),
            st("kB_s"), st("vB_s"), st("iB_s"), st("cC_s"), st("ff_s"))
```

```python
import functools
import math

import jax
import jax.numpy as jnp
from jax import lax
from jax.experimental import pallas as pl
from jax.experimental.pallas import tpu as pltpu

F32, BF16, I32 = jnp.float32, jnp.bfloat16, jnp.int32

CHUNK = 64
H_A = 16
H_B = 16
HKV_B = 4
G_B = H_B // HKV_B
H_IDX = 8
D_IDX = 64
TOPK_MAX = 256
NUM_BUCKETS = 32
MAX_DISTANCE = 128
XA_HEADS = 4
EPS = 1e-6

LANES = 128
SUBLANES = 8
HEAD_DIM = 64
KEY_TILE = 128
VMEM_LIMIT_BYTES = 56 * 1024 * 1024

INT_MIN = -2 ** 31
NEG_INF = float("-inf")

_WHOLE = pl.BlockSpec(memory_space=pltpu.VMEM)


def _cp(*sem):
    return pltpu.CompilerParams(dimension_semantics=sem, vmem_limit_bytes=VMEM_LIMIT_BYTES)


def _rows(tm, n):
    return pl.BlockSpec((tm, n), lambda i: (i, 0))


def _rms(x, g):
    return x * lax.rsqrt(jnp.mean(x * x, axis=-1, keepdims=True) + EPS) * g


def _mm(a, b):
    return jnp.dot(a, b, preferred_element_type=F32)


def _mm_nt(a, b):
    return lax.dot_general(a, b, (((1,), (1,)), ((), ())), preferred_element_type=F32)


def _proj_kernel(x_ref, g_ref, *refs, n_w):
    w_refs, o_refs = refs[:n_w], refs[n_w:]
    h = _rms(x_ref[...], g_ref[...]).astype(BF16)
    for w_ref, o_ref in zip(w_refs, o_refs):
        o_ref[...] = _mm(h, w_ref[...])


def _proj(x, g, ws, tm):
    m, d = x.shape
    return pl.pallas_call(
        functools.partial(_proj_kernel, n_w=len(ws)),
        grid=(m // tm,),
        in_specs=[_rows(tm, d), _WHOLE] + [_WHOLE] * len(ws),
        out_specs=[_rows(tm, w.shape[1]) for w in ws],
        out_shape=[jax.ShapeDtypeStruct((m, w.shape[1]), F32) for w in ws],
        compiler_params=_cp("parallel"),
    )(x, g, *ws)


def _a_in_kernel(x_ref, g_ref, w_ref, wf_ref, bf_ref, q_ref, k_ref, v_ref, kb_ref, vb_ref, lf_ref):
    d = q_ref.shape[-1]
    h = _rms(x_ref[...], g_ref[...]).astype(BF16)
    q_ref[...] = (_mm(h, w_ref[:, 0:d]) * (HEAD_DIM ** -0.5)).astype(BF16)
    k = _mm(h, w_ref[:, d:2 * d])
    k_ref[...] = k
    kb_ref[...] = k.astype(BF16)
    v = _mm(h, w_ref[:, 2 * d:3 * d])
    v_ref[...] = v
    vb_ref[...] = v.astype(BF16)
    fl = _mm(h, wf_ref[...]) + bf_ref[...]
    lf_ref[...] = jnp.minimum(fl, 0.0) - jnp.log1p(jnp.exp(-jnp.abs(fl)))


def _a_in(x, g, w_qkv, w_f, b_f, tm):
    m, d = x.shape
    sds = jax.ShapeDtypeStruct
    return pl.pallas_call(
        _a_in_kernel,
        grid=(m // tm,),
        in_specs=[_rows(tm, d), _WHOLE, _WHOLE, _WHOLE, _WHOLE],
        out_specs=[_rows(tm, d)] * 5 + [_rows(tm, H_A)],
        out_shape=[sds((m, d), BF16), sds((m, d), F32), sds((m, d), F32),
                   sds((m, d), BF16), sds((m, d), BF16), sds((m, H_A), F32)],
        compiler_params=_cp("parallel"),
    )(x, g, w_qkv, w_f, b_f)


def _b_in_kernel(x_ref, g_ref, wq_ref, wkv_ref, wqi_ref, wki_ref, wwi_ref,
                 q_ref, k_ref, v_ref, kb_ref, vb_ref, qi_ref, ki_ref, kib_ref, wi_ref):
    dkv = k_ref.shape[-1]
    h = _rms(x_ref[...], g_ref[...]).astype(BF16)
    q_ref[...] = (_mm(h, wq_ref[...]) * (HEAD_DIM ** -0.5)).astype(BF16)
    kv = _mm(h, wkv_ref[...])
    k_ref[...] = kv[:, 0:dkv]
    kb_ref[...] = kv[:, 0:dkv].astype(BF16)
    v_ref[...] = kv[:, dkv:2 * dkv]
    vb_ref[...] = kv[:, dkv:2 * dkv].astype(BF16)
    qi_ref[...] = (_mm(h, wqi_ref[...]) * (D_IDX ** -0.5)).astype(BF16)
    ki2 = _mm(h, wki_ref[...])
    ki_ref[...] = ki2[:, 0:D_IDX]
    kib_ref[...] = ki2.astype(BF16)
    wi_ref[...] = _mm(h, wwi_ref[...]) * (H_IDX ** -0.5)


def _b_in(x, g, w_q, w_kv, w_qi, w_ki2, w_wi, tm):
    m, d = x.shape
    dkv = w_kv.shape[1] // 2
    dqi = w_qi.shape[1]
    sds = jax.ShapeDtypeStruct
    return pl.pallas_call(
        _b_in_kernel,
        grid=(m // tm,),
        in_specs=[_rows(tm, d)] + [_WHOLE] * 6,
        out_specs=[_rows(tm, d), _rows(tm, dkv), _rows(tm, dkv), _rows(tm, dkv), _rows(tm, dkv),
                   _rows(tm, dqi), _rows(tm, D_IDX), _rows(tm, 2 * D_IDX), _rows(tm, H_IDX)],
        out_shape=[sds((m, d), BF16), sds((m, dkv), F32), sds((m, dkv), F32),
                   sds((m, dkv), BF16), sds((m, dkv), BF16), sds((m, dqi), BF16),
                   sds((m, D_IDX), F32), sds((m, 2 * D_IDX), BF16), sds((m, H_IDX), F32)],
        compiler_params=_cp("parallel"),
    )(x, g, w_q, w_kv, w_qi, w_ki2, w_wi)


def _cumsum_kernel(x_ref, o_ref):
    n = x_ref.shape[-1]
    a = lax.broadcasted_iota(I32, (LANES, LANES), 0)
    b = lax.broadcasted_iota(I32, (LANES, LANES), 1)
    tri = (a <= b).astype(F32)
    carry = jnp.zeros((x_ref.shape[0], 1), F32)
    for c in range(n // LANES):
        sl = slice(c * LANES, (c + 1) * LANES)
        y = jnp.dot(x_ref[:, sl], tri, precision=lax.Precision.HIGHEST,
                    preferred_element_type=F32) + carry
        o_ref[:, sl] = y
        carry = y[:, LANES - 1:LANES]


def _cumsum_lanes(x, rb):
    r, n = x.shape
    return pl.pallas_call(
        _cumsum_kernel,
        grid=(r // rb,),
        in_specs=[_rows(rb, n)],
        out_specs=_rows(rb, n),
        out_shape=jax.ShapeDtypeStruct((r, n), F32),
        compiler_params=_cp("parallel"),
    )(x)


def _fox_kernel(qt_ref, k_ref, vt_ref, fq_ref, fk_ref, o_ref, m_scr, l_scr, acc_scr,
                *, tq, tk, q_off, nkt):
    i = pl.program_id(2)
    row_q = lax.broadcasted_iota(I32, (LANES, tq), 0)
    row_v = lax.broadcasted_iota(I32, (LANES, tk), 0)
    qt = qt_ref[0]
    zq = jnp.zeros_like(qt)
    qh = (jnp.where(row_q < HEAD_DIM, qt, zq), jnp.where(row_q >= HEAD_DIM, qt, zq))
    fq = (fq_ref[0, 0, 0:1, :], fq_ref[0, 0, 1:2, :])
    m_scr[...] = jnp.full(m_scr.shape, NEG_INF, F32)
    l_scr[...] = jnp.zeros(l_scr.shape, F32)
    acc_scr[...] = jnp.zeros(acc_scr.shape, F32)
    q0 = q_off + i * tq
    qpos = q0 + lax.broadcasted_iota(I32, (1, tq), 1)

    def step(tiles, masked):
        scores = [[_mm(k_ref[0, pl.ds(kk * tk, tk), :], qh[e]) for e in range(2)] for kk in tiles]
        for kk, s_pair in zip(tiles, scores):
            vt = vt_ref[0, kk]
            zv = jnp.zeros_like(vt)
            fk = fk_ref[0, 0, pl.ds(kk * tk, tk), :]
            if masked:
                mask = (kk * tk + lax.broadcasted_iota(I32, (tk, 1), 0)) <= qpos
            alphas, pvs = [], []
            for e in range(2):
                s = s_pair[e] + (fq[e] - fk[:, e:e + 1])
                if masked:
                    s = jnp.where(mask, s, NEG_INF)
                m_old = m_scr[e]
                m_new = jnp.maximum(m_old, jnp.max(s, axis=0, keepdims=True))
                alpha = jnp.exp(m_old - m_new)
                p = jnp.exp(s - m_new)
                l_scr[e] = alpha * l_scr[e] + jnp.sum(p, axis=0, keepdims=True)
                m_scr[e] = m_new
                ve = jnp.where((row_v >= HEAD_DIM) == (e == 1), vt, zv)
                pvs.append(_mm(ve, p.astype(BF16)))
                alphas.append(alpha)
            acc_scr[...] = (acc_scr[...] * jnp.where(row_q < HEAD_DIM, alphas[0], alphas[1])
                            + pvs[0] + pvs[1])

    def full_body(kp, carry):
        step([2 * kp, 2 * kp + 1], False)
        return carry

    assert q_off % (2 * tk) == 0 and tq % tk == 0 and (tq // tk == 1 or tq % (2 * tk) == 0)
    n_full = q0 // tk
    lax.fori_loop(0, n_full // 2, full_body, 0)
    step([n_full + t for t in range(tq // tk)], True)
    o_ref[0] = (acc_scr[...] / jnp.where(row_q < HEAD_DIM, l_scr[0], l_scr[1])).astype(BF16)


def _fox_attention(qt, k, vt, fq, fk, *, tq, tk, q_off):
    b, d, t_q = qt.shape
    t_k = k.shape[1]
    nkt = t_k // tk
    kern = functools.partial(_fox_kernel, tq=tq, tk=tk, q_off=q_off, nkt=nkt)
    return pl.pallas_call(
        kern,
        grid=(b, d // LANES, t_q // tq),
        in_specs=[pl.BlockSpec((1, LANES, tq), lambda bb, j, i: (bb, j, i)),
                  pl.BlockSpec((1, t_k, LANES), lambda bb, j, i: (bb, 0, j)),
                  pl.BlockSpec((1, nkt, LANES, tk), lambda bb, j, i: (bb, 0, j, 0)),
                  pl.BlockSpec((1, 1, 2, tq), lambda bb, j, i: (bb, j, 0, i)),
                  pl.BlockSpec((1, 1, t_k, 2), lambda bb, j, i: (bb, j, 0, 0))],
        out_specs=pl.BlockSpec((1, LANES, tq), lambda bb, j, i: (bb, j, i)),
        out_shape=jax.ShapeDtypeStruct((b, d, t_q), BF16),
        scratch_shapes=[pltpu.VMEM((2, 1, tq), F32), pltpu.VMEM((2, 1, tq), F32),
                        pltpu.VMEM((LANES, tq), F32)],
        compiler_params=_cp("parallel", "parallel", "arbitrary"),
    )(qt, k, vt, fq, fk)


def _dsa_kernel(qt_ref, qit_ref, wit_ref, k_ref, vt_ref, ki_ref, bias_ref, o_ref,
                keys_scr, mb_scr, qim_scr, qg_scr, m_scr, l_scr, acc_scr,
                *, q_off, k_top, tk_valid):
    tq = KEY_TILE
    i = pl.program_id(1)
    q0 = q_off + i * tq
    home = q0 // KEY_TILE
    qpos = q0 + lax.broadcasted_iota(I32, (1, tq), 1)
    chunk_shift = int(math.log2(CHUNK))
    qchunk = lax.shift_right_logical(qpos, chunk_shift)
    row = lax.broadcasted_iota(I32, (LANES, tq), 0)
    kf = float(k_top)

    for h in range(H_IDX):
        pair = qit_ref[0, (h // 2) * LANES:(h // 2 + 1) * LANES, :]
        qim_scr[h] = jnp.where((row >= HEAD_DIM) == (h % 2 == 1), pair, jnp.zeros_like(pair))
    qg_scr[...] = jnp.zeros(qg_scr.shape, BF16)
    for g in range(HKV_B):
        r0 = (g % 2) * HEAD_DIM
        for r in range(G_B):
            hq = g * G_B + r
            qg_scr[g, r0:r0 + HEAD_DIM, r * tq:(r + 1) * tq] = (
                qt_ref[0, hq * HEAD_DIM:(hq + 1) * HEAD_DIM, :])

    def score_body(kt, carry):
        kit = ki_ref[0, pl.ds(kt * KEY_TILE, KEY_TILE), :]
        acc = jnp.zeros((KEY_TILE, tq), F32)
        for h in range(H_IDX):
            acc = acc + jnp.maximum(_mm(kit, qim_scr[h]), 0.0) * wit_ref[0, h:h + 1, :]
        acc = jnp.where(acc == 0.0, 0.0, acc)
        bits = lax.bitcast_convert_type(acc, I32)
        key = bits ^ (lax.shift_right_arithmetic(bits, 31) & 0x7FFFFFFF)
        kpos = kt * KEY_TILE + lax.broadcasted_iota(I32, (KEY_TILE, 1), 0)
        adm = (lax.shift_right_logical(kpos, chunk_shift) <= qchunk) & (kpos < tk_valid)
        keys_scr[kt] = jnp.where(adm, key, INT_MIN)
        return carry

    lax.fori_loop(0, home + 1, score_body, 0)

    def count(cand, strict):
        def body(kt, a):
            key = keys_scr[kt]
            hit = (key > cand) if strict else (key >= cand)
            return a + jnp.where(hit, 1.0, 0.0)

        a = lax.fori_loop(0, home + 1, body, jnp.zeros((KEY_TILE, tq), F32))
        return jnp.sum(a, axis=0, keepdims=True)

    zero = jnp.zeros((1, tq), I32)
    thr = jnp.where(count(zero, False) >= kf, zero, jnp.full((1, tq), INT_MIN, I32))

    def bit_body(b, t):
        cand = t + lax.shift_left(jnp.int32(1), 30 - b)
        return jnp.where(count(cand, False) >= kf, cand, t)

    thr = lax.fori_loop(0, 31, bit_body, thr)

    need = kf - count(thr, True)
    ra = lax.broadcasted_iota(I32, (KEY_TILE, KEY_TILE), 0)
    rb = lax.broadcasted_iota(I32, (KEY_TILE, KEY_TILE), 1)
    earlier = (rb < ra).astype(BF16)

    def tie_body(kt, seen):
        key = keys_scr[kt]
        eq = key == thr
        eqf = jnp.where(eq, 1.0, 0.0)
        rank = _mm(earlier, eqf.astype(BF16)) + seen
        sel = ((key > thr) | (eq & (rank < need))) & (key != INT_MIN)
        mb_scr[kt] = jnp.where(sel, 0.0, NEG_INF)
        return seen + jnp.sum(eqf, axis=0, keepdims=True)

    lax.fori_loop(0, home + 1, tie_body, jnp.zeros((1, tq), F32))

    m_scr[...] = jnp.full(m_scr.shape, NEG_INF, F32)
    l_scr[...] = jnp.zeros(l_scr.shape, F32)
    acc_scr[...] = jnp.zeros(acc_scr.shape, F32)

    def attend(kt, near):
        mb = mb_scr[kt]
        scores = [_mm(k_ref[0, pl.ds(kt * KEY_TILE, KEY_TILE), (g // 2) * LANES:(g // 2 + 1) * LANES],
                      qg_scr[g]) for g in range(HKV_B)]
        for g in range(HKV_B):
            st = scores[g]
            vt = vt_ref[0, kt, g * HEAD_DIM:(g + 1) * HEAD_DIM, :]
            for r in range(G_B):
                hq = g * G_B + r
                s = st[:, r * tq:(r + 1) * tq] + mb
                if near is not None:
                    s = s + bias_ref[hq, near]
                m_old = m_scr[hq]
                m_new = jnp.maximum(m_old, jnp.max(s, axis=0, keepdims=True))
                m_use = jnp.where(m_new == NEG_INF, 0.0, m_new)
                alpha = jnp.exp(m_old - m_use)
                p = jnp.exp(s - m_use)
                l_scr[hq] = alpha * l_scr[hq] + jnp.sum(p, axis=0, keepdims=True)
                m_scr[hq] = m_new
                acc_scr[hq] = acc_scr[hq] * alpha + _mm(vt, p.astype(BF16))

    def far_body(kt, carry):
        attend(kt, None)
        return carry

    lax.fori_loop(0, jnp.maximum(home - 1, 0), far_body, 0)

    @pl.when(home >= 1)
    def _():
        attend(home - 1, 0)

    attend(home, 1)

    for hq in range(H_B):
        o_ref[0, hq * HEAD_DIM:(hq + 1) * HEAD_DIM, :] = (acc_scr[hq] / l_scr[hq]).astype(BF16)


def _dsa_attention(qt, qit, wit, k, vt, ki, bias, *, q_off, k_top, tk_valid):
    b, d, t_q = qt.shape
    t_k = k.shape[1]
    nkt = t_k // KEY_TILE
    tq = KEY_TILE
    kern = functools.partial(_dsa_kernel, q_off=q_off, k_top=k_top, tk_valid=tk_valid)
    qblk = lambda n: pl.BlockSpec((1, n, tq), lambda bb, i: (bb, 0, i))
    kblk = lambda n: pl.BlockSpec((1, t_k, n), lambda bb, i: (bb, 0, 0))
    return pl.pallas_call(
        kern,
        grid=(b, t_q // tq),
        in_specs=[qblk(d), qblk(qit.shape[1]), qblk(H_IDX), kblk(k.shape[2]),
                  pl.BlockSpec((1, nkt, vt.shape[2], KEY_TILE), lambda bb, i: (bb, 0, 0, 0)),
                  kblk(ki.shape[2]), _WHOLE],
        out_specs=qblk(d),
        out_shape=jax.ShapeDtypeStruct((b, d, t_q), BF16),
        scratch_shapes=[pltpu.VMEM((nkt, KEY_TILE, tq), I32), pltpu.VMEM((nkt, KEY_TILE, tq), F32),
                        pltpu.VMEM((H_IDX, LANES, tq), BF16),
                        pltpu.VMEM((HKV_B, LANES, G_B * tq), BF16),
                        pltpu.VMEM((H_B, 1, tq), F32), pltpu.VMEM((H_B, 1, tq), F32),
                        pltpu.VMEM((H_B, HEAD_DIM, tq), F32)],
        compiler_params=_cp("parallel", "arbitrary"),
    )(qt, qit, wit, k, vt, ki, bias)


def _conv3(u, w_ref, cols, prev):
    tm = u.shape[0]
    row = lax.broadcasted_iota(I32, (tm, 1), 0)
    r1 = pltpu.roll(u, 1, 0)
    r2 = pltpu.roll(u, 2, 0)
    if prev[0] == "stream":
        carry = prev[1]
        c6, c7 = carry[6:7, :], carry[7:8, :]
        um1 = jnp.where(row == 0, c7, r1)
        um2 = jnp.where(row == 0, c6, jnp.where(row == 1, c7, r2))
    else:
        _, seq, pm1, pm2 = prev
        t = row % seq
        um1 = jnp.where(t == 0, pm1, r1)
        um2 = jnp.where(t < 2, pm2, r2)
    return w_ref[0:1, cols] * um2 + w_ref[1:2, cols] * um1 + w_ref[2:3, cols] * u


def _col_chunks(n, width):
    out, c = [], 0
    while c < n:
        out.append((c, min(width, n - c)))
        c += width
    return out


def _ffn_kernel(*refs, mode, seq, tpb, dff, cw):
    if mode == "stream":
        x_ref, g_ref, wup_ref, wc_ref, bc_ref, wdn_ref, o_ref, tail_ref, carry_scr = refs
    else:
        x_ref, g_ref, wup_ref, wc_ref, bc_ref, wdn_ref, pm1_ref, pm2_ref, o_ref, up_ref = refs
    x = x_ref[...]
    tm = x.shape[0]
    h = _rms(x, g_ref[...]).astype(BF16)
    if mode == "stream":
        @pl.when(pl.program_id(0) % tpb == 0)
        def _():
            carry_scr[...] = jnp.zeros(carry_scr.shape, F32)
    acc = x
    for c0, w in _col_chunks(dff, cw):
        ys = []
        for base in (0, dff):
            cols = slice(base + c0, base + c0 + w)
            up = _mm(h, wup_ref[:, cols])
            if mode == "stream":
                y = _conv3(up, wc_ref, cols, ("stream", carry_scr[:, cols]))
                carry_scr[:, cols] = up[tm - SUBLANES:tm, :]
                tail_ref[0, :, cols] = up[tm - SUBLANES:tm, :]
            else:
                y = _conv3(up, wc_ref, cols, ("seq", seq, pm1_ref[:, cols], pm2_ref[:, cols]))
                up_ref[:, cols] = up
            ys.append(y + bc_ref[:, cols])
        gate, val = ys
        act = (gate / (1.0 + jnp.exp(-gate))) * val
        acc = acc + _mm(act.astype(BF16), wdn_ref[c0:c0 + w, :])
    o_ref[...] = acc


def _ffn_stream(x, g, w_up, w_conv, b_conv, w_down, *, tm, t_len):
    m, d = x.shape
    c2 = w_up.shape[1]
    tpb = t_len // tm
    kern = functools.partial(_ffn_kernel, mode="stream", seq=None, tpb=tpb, dff=c2 // 2, cw=512)
    return pl.pallas_call(
        kern,
        grid=(m // tm,),
        in_specs=[_rows(tm, d)] + [_WHOLE] * 5,
        out_specs=[_rows(tm, d), pl.BlockSpec((1, SUBLANES, c2), lambda i: (i // tpb, 0, 0))],
        out_shape=[jax.ShapeDtypeStruct((m, d), F32),
                   jax.ShapeDtypeStruct((m // t_len, SUBLANES, c2), F32)],
        scratch_shapes=[pltpu.VMEM((SUBLANES, c2), F32)],
        compiler_params=_cp("arbitrary"),
    )(x, g, w_up, w_conv, b_conv, w_down)


def _ffn_seq(x, g, w_up, w_conv, b_conv, w_down, pm1, pm2, *, tm, seq):
    m, d = x.shape
    c2 = w_up.shape[1]
    kern = functools.partial(_ffn_kernel, mode="seq", seq=seq, tpb=None, dff=c2 // 2, cw=512)
    return pl.pallas_call(
        kern,
        grid=(m // tm,),
        in_specs=[_rows(tm, d)] + [_WHOLE] * 5 + [_rows(tm, c2), _rows(tm, c2)],
        out_specs=[_rows(tm, d), _rows(tm, c2)],
        out_shape=[jax.ShapeDtypeStruct((m, d), F32), jax.ShapeDtypeStruct((m, c2), F32)],
        compiler_params=_cp("parallel"),
    )(x, g, w_up, w_conv, b_conv, w_down, pm1, pm2)


def _sconv_kernel(*refs, mode, seq, tpb, cw):
    if mode == "stream":
        x_ref, g_ref, win_ref, wc_ref, wout_ref, o_ref, tail_ref, carry_scr = refs
    else:
        x_ref, g_ref, win_ref, wc_ref, wout_ref, pm1_ref, pm2_ref, o_ref, p_ref = refs
    x = x_ref[...]
    tm, d = x.shape
    h = _rms(x, g_ref[...]).astype(BF16)
    if mode == "stream":
        @pl.when(pl.program_id(0) % tpb == 0)
        def _():
            carry_scr[...] = jnp.zeros(carry_scr.shape, F32)
    acc = x
    for c0, w in _col_chunks(d, cw):
        cols = slice(c0, c0 + w)
        gb = _mm(h, win_ref[:, c0:c0 + w])
        gc = _mm(h, win_ref[:, d + c0:d + c0 + w])
        u = _mm(h, win_ref[:, 2 * d + c0:2 * d + c0 + w])
        p = gc * u
        if mode == "stream":
            y = _conv3(p, wc_ref, cols, ("stream", carry_scr[:, cols]))
            carry_scr[:, cols] = p[tm - SUBLANES:tm, :]
            tail_ref[0, :, cols] = p[tm - SUBLANES:tm, :]
        else:
            y = _conv3(p, wc_ref, cols, ("seq", seq, pm1_ref[:, cols], pm2_ref[:, cols]))
            p_ref[:, cols] = p
        acc = acc + _mm((gb * y).astype(BF16), wout_ref[c0:c0 + w, :])
    o_ref[...] = acc


def _sconv_stream(x, g, w_in, w_conv, w_out, *, tm, t_len):
    m, d = x.shape
    tpb = t_len // tm
    kern = functools.partial(_sconv_kernel, mode="stream", seq=None, tpb=tpb, cw=512)
    return pl.pallas_call(
        kern,
        grid=(m // tm,),
        in_specs=[_rows(tm, d)] + [_WHOLE] * 4,
        out_specs=[_rows(tm, d), pl.BlockSpec((1, SUBLANES, d), lambda i: (i // tpb, 0, 0))],
        out_shape=[jax.ShapeDtypeStruct((m, d), F32),
                   jax.ShapeDtypeStruct((m // t_len, SUBLANES, d), F32)],
        scratch_shapes=[pltpu.VMEM((SUBLANES, d), F32)],
        compiler_params=_cp("arbitrary"),
    )(x, g, w_in, w_conv, w_out)


def _sconv_seq(x, g, w_in, w_conv, w_out, pm1, pm2, *, tm, seq):
    m, d = x.shape
    kern = functools.partial(_sconv_kernel, mode="seq", seq=seq, tpb=None, cw=512)
    return pl.pallas_call(
        kern,
        grid=(m // tm,),
        in_specs=[_rows(tm, d)] + [_WHOLE] * 4 + [_rows(tm, d), _rows(tm, d)],
        out_specs=[_rows(tm, d), _rows(tm, d)],
        out_shape=[jax.ShapeDtypeStruct((m, d), F32), jax.ShapeDtypeStruct((m, d), F32)],
        compiler_params=_cp("parallel"),
    )(x, g, w_in, w_conv, w_out, pm1, pm2)


def _post_kernel(*refs, nb, has_mix):
    if has_mix:
        x_ref, a_ref, wmix_ref, g_ref, wq_ref, mk_ref, mv_ref, wo_ref, o_ref, oc_scr = refs
        x = x_ref[...] + _mm(a_ref[...], wmix_ref[...])
    else:
        x_ref, g_ref, wq_ref, mk_ref, mv_ref, wo_ref, o_ref, oc_scr = refs
        x = x_ref[...]
    tm, d = x.shape
    dh = d // XA_HEADS
    rpb = tm // nb
    h = _rms(x, g_ref[...]).astype(BF16)
    q = (_mm(h, wq_ref[...]) * (dh ** -0.5)).astype(BF16)
    for b in range(nb):
        rows = slice(b * rpb, (b + 1) * rpb)
        for hh in range(XA_HEADS):
            cols = slice(hh * dh, (hh + 1) * dh)
            s = _mm_nt(q[rows, cols], mk_ref[b, :, cols].astype(BF16))
            e = jnp.exp(s - jnp.max(s, axis=1, keepdims=True))
            p = e / jnp.sum(e, axis=1, keepdims=True)
            oc_scr[rows, cols] = _mm(p.astype(BF16), mv_ref[b, :, cols].astype(BF16)).astype(BF16)
    o_ref[...] = x + _mm(oc_scr[...], wo_ref[...])


def _post(x, a, w_mix, g, w_q, mk, mv, w_o, *, tm, nb, tiles_per_mem):
    m, d = x.shape
    n_mem = mk.shape[1]
    has_mix = a is not None
    mem_spec = pl.BlockSpec((nb, n_mem, d), lambda i: (i // tiles_per_mem, 0, 0))
    ins = [x] + ([a, w_mix] if has_mix else []) + [g, w_q, mk, mv, w_o]
    specs = ([_rows(tm, d)] + ([_rows(tm, d), _WHOLE] if has_mix else [])
             + [_WHOLE, _WHOLE, mem_spec, mem_spec, _WHOLE])
    return pl.pallas_call(
        functools.partial(_post_kernel, nb=nb, has_mix=has_mix),
        grid=(m // tm,),
        in_specs=specs,
        out_specs=_rows(tm, d),
        out_shape=jax.ShapeDtypeStruct((m, d), F32),
        scratch_shapes=[pltpu.VMEM((tm, d), BF16)],
        compiler_params=_cp("parallel"),
    )(*ins)


def _norm_kernel(x_ref, g_ref, o_ref):
    o_ref[...] = _rms(x_ref[...], g_ref[...])


def _final_norm(x, g, tm):
    m, d = x.shape
    return pl.pallas_call(
        _norm_kernel, grid=(m // tm,), in_specs=[_rows(tm, d), _WHOLE], out_specs=_rows(tm, d),
        out_shape=jax.ShapeDtypeStruct((m, d), F32), compiler_params=_cp("parallel"),
    )(x, g)


def _t5_bucket(rel):
    half = NUM_BUCKETS // 2
    ret = jnp.where(rel > 0, half, 0)
    n = jnp.abs(rel)
    max_exact = half // 2
    nf = jnp.maximum(n, 1).astype(jnp.float32)
    large = max_exact + (jnp.log(nf / max_exact) / math.log(MAX_DISTANCE / max_exact)
                         * (half - max_exact)).astype(jnp.int32)
    large = jnp.minimum(large, half - 1)
    return ret + jnp.where(n < max_exact, n, large)


def _near_bias(rel_bias):
    sl = jnp.arange(KEY_TILE, dtype=jnp.int32)[None, :, None]
    ql = jnp.arange(KEY_TILE, dtype=jnp.int32)[None, None, :]
    off = jnp.array([-KEY_TILE, 0], jnp.int32)[:, None, None]
    rel = sl + off - ql
    far = _t5_bucket(jnp.array(-2 * KEY_TILE, jnp.int32))
    tab = rel_bias[_t5_bucket(rel)] - rel_bias[far]
    return jnp.transpose(tab, (3, 0, 1, 2)).astype(F32)


def _seq_context(prev, seq):
    b, _, c = prev.shape
    z = jnp.zeros((b, seq, c), prev.dtype)
    pm1 = z.at[:, 0].set(prev[:, 1])
    pm2 = z.at[:, 0].set(prev[:, 0]).at[:, 1].set(prev[:, 1])
    return pm1.reshape(b * seq, c), pm2.reshape(b * seq, c)


def _pad_keys(cache, new, t_pad):
    b, p, c = cache.shape
    t = new.shape[1]
    pad = jnp.zeros((b, t_pad - p - t, c), BF16)
    return jnp.concatenate([cache.astype(BF16), new.astype(BF16), pad], axis=1)


def _feature_major(a, t_pad):
    a = jnp.swapaxes(a, 1, 2)
    return jnp.pad(a, ((0, 0), (0, 0), (0, t_pad - a.shape[2])))


def _key_tiles_feature_major(a, tk):
    b, t, c = a.shape
    return jnp.swapaxes(a.reshape(b, t // tk, tk, c), 2, 3)


def kernel(x_prompt, x_sample, mem_prompt, cache_k_A, cache_v_A, cache_logf_A, cache_k_B, cache_v_B,
           cache_kidx_B, state_conv_C, state_ffconv, cache_mem_k, cache_mem_v, g_mix, g_xa, g_ffn,
           g_mem, g_final, w_in_A, b_f_A, w_out_A, w_in_B, w_out_B, rel_bias, w_in_C, w_conv_C,
           w_out_C, w_q_xa, w_k_xa, w_v_xa, w_o_xa, w_up, w_conv_ff, b_conv_ff, w_down):
    bp, tp, d = x_prompt.shape
    bs, ts, _ = x_sample.shape
    depth = g_mix.shape[0]
    past = cache_k_A.shape[2]
    n_mem = mem_prompt.shape[1]
    assert d == H_A * HEAD_DIM == H_B * HEAD_DIM and ts % SUBLANES == 0 and ts <= KEY_TILE
    assert past % KEY_TILE == 0 and tp % KEY_TILE == 0
    mixers = tuple("ABC"[l % 3] for l in range(depth))
    slot = tuple(mixers[:l].count(mixers[l]) for l in range(depth))
    mp, ms = bp * tp, bs * ts
    tm_p = min(256, tp)
    tm_s = min(ms, 8 * ts)
    tq_a = min(512, tp)
    tk_a = min(256, tp)
    t_all = past + ts
    t_pad = -(-t_all // KEY_TILE) * KEY_TILE
    k_top_p = min(TOPK_MAX, tp // 4)
    k_top_s = min(TOPK_MAX, t_all // 4)
    nb_s = min(4, bs)

    bf = lambda a: a.astype(BF16)
    row = lambda a: a.reshape(1, -1)
    xp = x_prompt.reshape(mp, d)
    xs = x_sample.reshape(ms, d)
    mem = mem_prompt.reshape(bp * n_mem, d)
    bias_near = _near_bias(rel_bias)

    outs = {n: [] for n in ("kA_p", "vA_p", "fA_p", "kB_p", "vB_p", "iB_p", "cC_p", "ff_p", "mk_p",
                            "mv_p", "kA_s", "vA_s", "fA_s", "kB_s", "vB_s", "iB_s", "cC_s", "ff_s")}

    def fox(x, g, j, b, t, cache):
        m = b * t
        q, k, v, kb, vb, lf = _a_in(x, g, bf(w_in_A[j][:, :3 * d]), bf(w_in_A[j][:, 3 * d:]),
                                    row(b_f_A[j]), min(256, m))
        lf_t = jnp.swapaxes(lf.reshape(b, t, H_A), 1, 2)
        if cache is None:
            q_off, t_k, tq, tk, tq_pad = 0, t, tq_a, tk_a, t
            lf_all = lf_t
            k_all, v_all = kb.reshape(b, t, d), vb.reshape(b, t, d)
        else:
            ck, cv, clf = cache
            q_off, t_k, tq, tk, tq_pad = past, t_pad, KEY_TILE, KEY_TILE, KEY_TILE
            lf_all = jnp.concatenate([jnp.swapaxes(clf, 1, 2).astype(F32), lf_t,
                                      jnp.zeros((b, H_A, t_pad - t_all), F32)], axis=2)
            k_all = _pad_keys(ck.reshape(b, past, d), kb.reshape(b, t, d), t_pad)
            v_all = _pad_keys(cv.reshape(b, past, d), vb.reshape(b, t, d), t_pad)
        f_all = _cumsum_lanes(lf_all.reshape(b * H_A, t_k), min(64, b * H_A)).reshape(b, H_A, t_k)
        f_q = jnp.pad(f_all[:, :, q_off:q_off + t], ((0, 0), (0, 0), (0, tq_pad - t)))
        f_q = f_q.reshape(b, H_A // 2, 2, tq_pad)
        f_k = jnp.swapaxes(f_all.reshape(b, H_A // 2, 2, t_k), 2, 3)
        ot = _fox_attention(_feature_major(q.reshape(b, t, d), tq_pad), k_all,
                            _key_tiles_feature_major(v_all, tk), f_q, f_k,
                            tq=tq, tk=tk, q_off=q_off)
        o = jnp.swapaxes(ot[:, :, :t], 1, 2)
        return o.reshape(m, d), k, v, lf

    def dsa(x, g, j, b, t, cache):
        m = b * t
        w = w_in_B[j]
        dkv = HKV_B * HEAD_DIM
        c0, c2, c3 = d, d + 2 * dkv, d + 2 * dkv + H_IDX * D_IDX
        q, k, v, kb, vb, qi, ki, kib, wi = _b_in(
            x, g, bf(w[:, :c0]), bf(w[:, c0:c2]), bf(w[:, c2:c3]),
            bf(jnp.concatenate([w[:, c3:c3 + D_IDX]] * 2, axis=1)), bf(w[:, c3 + D_IDX:]), min(256, m))
        if cache is None:
            q_off, k_top, tk_valid, tq_pad = 0, k_top_p, t, t
            k_all, v_all, ki_all = kb.reshape(b, t, dkv), vb.reshape(b, t, dkv), kib.reshape(b, t, -1)
        else:
            ck, cv, cki = cache
            q_off, k_top, tk_valid, tq_pad = past, k_top_s, t_all, KEY_TILE
            k_all = _pad_keys(ck.reshape(b, past, dkv), kb.reshape(b, t, dkv), t_pad)
            v_all = _pad_keys(cv.reshape(b, past, dkv), vb.reshape(b, t, dkv), t_pad)
            cki2 = jnp.concatenate([cki, cki], axis=-1)
            ki_all = _pad_keys(cki2, kib.reshape(b, t, -1), t_pad)
        ot = _dsa_attention(_feature_major(q.reshape(b, t, d), tq_pad),
                            _feature_major(qi.reshape(b, t, -1), tq_pad),
                            _feature_major(wi.reshape(b, t, H_IDX), tq_pad),
                            k_all, _key_tiles_feature_major(v_all, KEY_TILE), ki_all, bias_near,
                            q_off=q_off, k_top=k_top, tk_valid=tk_valid)
        o = jnp.swapaxes(ot[:, :, :t], 1, 2)
        return o.reshape(m, d), k, v, ki

    for l in range(depth):
        mix, j = mixers[l], slot[l]
        gm, gx, gf = row(g_mix[l]), row(g_xa[l]), row(g_ffn[l])
        mk, mv = _proj(mem, row(g_mem[l]), [bf(w_k_xa[l]), bf(w_v_xa[l])], min(256, bp * n_mem))
        outs["mk_p"].append(mk.reshape(bp, n_mem, XA_HEADS, d // XA_HEADS))
        outs["mv_p"].append(mv.reshape(bp, n_mem, XA_HEADS, d // XA_HEADS))
        mks, mvs = cache_mem_k[l].reshape(bs, n_mem, d), cache_mem_v[l].reshape(bs, n_mem, d)
        post = dict(g=gx, w_q=bf(w_q_xa[l]), w_o=bf(w_o_xa[l]))

        if mix == "A":
            ap, k, v, f = fox(xp, gm, j, bp, tp, None)
            outs["kA_p"].append(k.reshape(bp, tp, H_A, HEAD_DIM))
            outs["vA_p"].append(v.reshape(bp, tp, H_A, HEAD_DIM))
            outs["fA_p"].append(f.reshape(bp, tp, H_A))
            a_s, k, v, f = fox(xs, gm, j, bs, ts, (cache_k_A[j], cache_v_A[j], cache_logf_A[j]))
            outs["kA_s"].append(k.reshape(bs, ts, H_A, HEAD_DIM))
            outs["vA_s"].append(v.reshape(bs, ts, H_A, HEAD_DIM))
            outs["fA_s"].append(f.reshape(bs, ts, H_A))
            w_mix = bf(w_out_A[j])
        elif mix == "B":
            ap, k, v, ki = dsa(xp, gm, j, bp, tp, None)
            outs["kB_p"].append(k.reshape(bp, tp, HKV_B, HEAD_DIM))
            outs["vB_p"].append(v.reshape(bp, tp, HKV_B, HEAD_DIM))
            outs["iB_p"].append(ki.reshape(bp, tp, D_IDX))
            a_s, k, v, ki = dsa(xs, gm, j, bs, ts, (cache_k_B[j], cache_v_B[j], cache_kidx_B[j]))
            outs["kB_s"].append(k.reshape(bs, ts, HKV_B, HEAD_DIM))
            outs["vB_s"].append(v.reshape(bs, ts, HKV_B, HEAD_DIM))
            outs["iB_s"].append(ki.reshape(bs, ts, D_IDX))
            w_mix = bf(w_out_B[j])
        else:
            xp, tail = _sconv_stream(xp, gm, bf(w_in_C[j]), w_conv_C[j], bf(w_out_C[j]),
                                     tm=tm_p, t_len=tp)
            outs["cC_p"].append(tail[:, SUBLANES - 2:, :])
            pm1, pm2 = _seq_context(state_conv_C[j], ts)
            xs, p_all = _sconv_seq(xs, gm, bf(w_in_C[j]), w_conv_C[j], bf(w_out_C[j]), pm1, pm2,
                                   tm=tm_s, seq=ts)
            outs["cC_s"].append(p_all.reshape(bs, ts, d)[:, ts - 2:, :])
            ap = a_s = w_mix = None

        xp = _post(xp, ap, w_mix, mk=mk.reshape(bp, n_mem, d), mv=mv.reshape(bp, n_mem, d),
                   tm=tm_p, nb=1, tiles_per_mem=tp // tm_p, **post)
        xs = _post(xs, a_s, w_mix, mk=mks, mv=mvs, tm=nb_s * ts, nb=nb_s, tiles_per_mem=1, **post)

        wu, wc, bc, wd = bf(w_up[l]), w_conv_ff[l], row(b_conv_ff[l]), bf(w_down[l])
        xp, tail = _ffn_stream(xp, gf, wu, wc, bc, wd, tm=tm_p, t_len=tp)
        outs["ff_p"].append(tail[:, SUBLANES - 2:, :])
        pm1, pm2 = _seq_context(state_ffconv[l], ts)
        xs, up_all = _ffn_seq(xs, gf, wu, wc, bc, wd, pm1, pm2, tm=tm_s, seq=ts)
        outs["ff_s"].append(up_all.reshape(bs, ts, -1)[:, ts - 2:, :])

    y_p = _final_norm(xp, row(g_final), tm_p).reshape(bp, tp, d)
    y_s = _final_norm(xs, row(g_final), min(256, ms)).reshape(bs, ts, d)
    st = lambda n: jnp.stack(outs[n])
    return (y_p, y_s, st("kA_p"), st("vA_p"), st("fA_p"), st("kB_p"), st("vB_p"), st("iB_p"),
            st("cC_p"), st("ff_p"), st("mk_p"), st("mv_p"), st("kA_s"), st("vA_s"), st("fA_s"),
            st("kB_s"), st("vB_s"), st("iB_s"), st("cC_s"), st("ff_s"))
```

```python
import functools
import math

import jax
import jax.numpy as jnp
from jax import lax
from jax.experimental import pallas as pl
from jax.experimental.pallas import tpu as pltpu

F32, BF16, I32 = jnp.float32, jnp.bfloat16, jnp.int32

CHUNK = 64
H_A = 16
H_B = 16
HKV_B = 4
G_B = H_B // HKV_B
H_IDX = 8
D_IDX = 64
TOPK_MAX = 256
NUM_BUCKETS = 32
MAX_DISTANCE = 128
XA_HEADS = 4
EPS = 1e-6

LANES = 128
SUBLANES = 8
HEAD_DIM = 64
KEY_TILE = 128
VMEM_LIMIT_BYTES = 56 * 1024 * 1024

INT_MIN = -2 ** 31
NEG_INF = float("-inf")
LOG2E = 1.0 / math.log(2.0)
Q_SCALE = HEAD_DIM ** -0.5 * LOG2E

_WHOLE = pl.BlockSpec(memory_space=pltpu.VMEM)


def _cp(*sem):
    return pltpu.CompilerParams(dimension_semantics=sem, vmem_limit_bytes=VMEM_LIMIT_BYTES)


def _rows(tm, n):
    return pl.BlockSpec((tm, n), lambda i: (i, 0))


def _rms(x, g):
    return x * lax.rsqrt(jnp.mean(x * x, axis=-1, keepdims=True) + EPS) * g


def _mm(a, b):
    return jnp.dot(a, b, preferred_element_type=F32)


def _mm_nt(a, b):
    return lax.dot_general(a, b, (((1,), (1,)), ((), ())), preferred_element_type=F32)


def _proj_kernel(x_ref, g_ref, *refs, n_w):
    w_refs, o_refs = refs[:n_w], refs[n_w:]
    h = _rms(x_ref[...], g_ref[...]).astype(BF16)
    for w_ref, o_ref in zip(w_refs, o_refs):
        o_ref[...] = _mm(h, w_ref[...])


def _proj(x, g, ws, tm):
    m, d = x.shape
    return pl.pallas_call(
        functools.partial(_proj_kernel, n_w=len(ws)),
        grid=(m // tm,),
        in_specs=[_rows(tm, d), _WHOLE] + [_WHOLE] * len(ws),
        out_specs=[_rows(tm, w.shape[1]) for w in ws],
        out_shape=[jax.ShapeDtypeStruct((m, w.shape[1]), F32) for w in ws],
        compiler_params=_cp("parallel"),
    )(x, g, *ws)


def _a_in_kernel(x_ref, g_ref, w_ref, wf_ref, bf_ref, q_ref, k_ref, v_ref, kb_ref, vb_ref, lf_ref):
    d = q_ref.shape[-1]
    h = _rms(x_ref[...], g_ref[...]).astype(BF16)
    q_ref[...] = (_mm(h, w_ref[:, 0:d]) * Q_SCALE).astype(BF16)
    k = _mm(h, w_ref[:, d:2 * d])
    k_ref[...] = k
    kb_ref[...] = k.astype(BF16)
    v = _mm(h, w_ref[:, 2 * d:3 * d])
    v_ref[...] = v
    vb_ref[...] = v.astype(BF16)
    fl = _mm(h, wf_ref[...]) + bf_ref[...]
    lf_ref[...] = jnp.minimum(fl, 0.0) - jnp.log1p(jnp.exp(-jnp.abs(fl)))


def _a_in(x, g, w_qkv, w_f, b_f, tm):
    m, d = x.shape
    sds = jax.ShapeDtypeStruct
    return pl.pallas_call(
        _a_in_kernel,
        grid=(m // tm,),
        in_specs=[_rows(tm, d), _WHOLE, _WHOLE, _WHOLE, _WHOLE],
        out_specs=[_rows(tm, d)] * 5 + [_rows(tm, H_A)],
        out_shape=[sds((m, d), BF16), sds((m, d), F32), sds((m, d), F32),
                   sds((m, d), BF16), sds((m, d), BF16), sds((m, H_A), F32)],
        compiler_params=_cp("parallel"),
    )(x, g, w_qkv, w_f, b_f)


def _b_in_kernel(x_ref, g_ref, wq_ref, wkv_ref, wqi_ref, wki_ref, wwi_ref,
                 q_ref, k_ref, v_ref, kb_ref, vb_ref, qi_ref, ki_ref, kib_ref, wi_ref):
    dkv = k_ref.shape[-1]
    h = _rms(x_ref[...], g_ref[...]).astype(BF16)
    q_ref[...] = (_mm(h, wq_ref[...]) * Q_SCALE).astype(BF16)
    kv = _mm(h, wkv_ref[...])
    k_ref[...] = kv[:, 0:dkv]
    kb_ref[...] = kv[:, 0:dkv].astype(BF16)
    v_ref[...] = kv[:, dkv:2 * dkv]
    vb_ref[...] = kv[:, dkv:2 * dkv].astype(BF16)
    qi_ref[...] = (_mm(h, wqi_ref[...]) * (D_IDX ** -0.5)).astype(BF16)
    ki2 = _mm(h, wki_ref[...])
    ki_ref[...] = ki2[:, 0:D_IDX]
    kib_ref[...] = ki2.astype(BF16)
    wi_ref[...] = _mm(h, wwi_ref[...]) * (H_IDX ** -0.5)


def _b_in(x, g, w_q, w_kv, w_qi, w_ki2, w_wi, tm):
    m, d = x.shape
    dkv = w_kv.shape[1] // 2
    dqi = w_qi.shape[1]
    sds = jax.ShapeDtypeStruct
    return pl.pallas_call(
        _b_in_kernel,
        grid=(m // tm,),
        in_specs=[_rows(tm, d)] + [_WHOLE] * 6,
        out_specs=[_rows(tm, d), _rows(tm, dkv), _rows(tm, dkv), _rows(tm, dkv), _rows(tm, dkv),
                   _rows(tm, dqi), _rows(tm, D_IDX), _rows(tm, 2 * D_IDX), _rows(tm, H_IDX)],
        out_shape=[sds((m, d), BF16), sds((m, dkv), F32), sds((m, dkv), F32),
                   sds((m, dkv), BF16), sds((m, dkv), BF16), sds((m, dqi), BF16),
                   sds((m, D_IDX), F32), sds((m, 2 * D_IDX), BF16), sds((m, H_IDX), F32)],
        compiler_params=_cp("parallel"),
    )(x, g, w_q, w_kv, w_qi, w_ki2, w_wi)


def _cumsum_kernel(x_ref, o_ref):
    n = x_ref.shape[-1]
    a = lax.broadcasted_iota(I32, (LANES, LANES), 0)
    b = lax.broadcasted_iota(I32, (LANES, LANES), 1)
    tri = (a <= b).astype(F32)
    carry = jnp.zeros((x_ref.shape[0], 1), F32)
    for c in range(n // LANES):
        sl = slice(c * LANES, (c + 1) * LANES)
        y = jnp.dot(x_ref[:, sl], tri, precision=lax.Precision.HIGHEST,
                    preferred_element_type=F32) + carry
        o_ref[:, sl] = y
        carry = y[:, LANES - 1:LANES]


def _cumsum_lanes(x, rb):
    r, n = x.shape
    return pl.pallas_call(
        _cumsum_kernel,
        grid=(r // rb,),
        in_specs=[_rows(rb, n)],
        out_specs=_rows(rb, n),
        out_shape=jax.ShapeDtypeStruct((r, n), F32),
        compiler_params=_cp("parallel"),
    )(x)


def _lookahead(units, depth):
    pending = []
    for idx in range(len(units) + depth):
        if idx < len(units):
            pending.append(units[idx][0]())
        if idx >= depth:
            units[idx - depth][1](pending[idx - depth])


def _fox_kernel(qa_ref, ka_ref, vt_ref, o_ref, m_scr, l_scr, acc_scr, *, tq, tk, tu, tpi, q_off):
    i = pl.program_id(2)
    m_scr[...] = jnp.full(m_scr.shape, NEG_INF, F32)
    l_scr[...] = jnp.zeros(l_scr.shape, F32)
    acc_scr[...] = jnp.zeros(acc_scr.shape, F32)
    q0 = q_off + i * tq

    def unit(kk, e, c, masked):
        cols = slice(c * tu, (c + 1) * tu)
        slab = slice(e * LANES, (e + 1) * LANES)

        def issue():
            return _mm(ka_ref[0, pl.ds(kk * tk, tk), slab], qa_ref[0, slab, cols])

        def consume(s):
            if masked:
                kpos = kk * tk + lax.broadcasted_iota(I32, (tk, 1), 0)
                qpos = q0 + c * tu + lax.broadcasted_iota(I32, (1, tu), 1)
                s = jnp.where(kpos <= qpos, s, NEG_INF)
            m_old = m_scr[e, :, cols]
            m_new = jnp.maximum(m_old, jnp.max(s, axis=0, keepdims=True))
            alpha = jnp.exp2(m_old - m_new)
            p = jnp.exp2(s - m_new)
            l_scr[e, :, cols] = alpha * l_scr[e, :, cols] + jnp.sum(p, axis=0, keepdims=True)
            m_scr[e, :, cols] = m_new
            ve = vt_ref[0, kk, e * HEAD_DIM:(e + 1) * HEAD_DIM, :]
            acc_scr[e, :, cols] = acc_scr[e, :, cols] * alpha + _mm(ve, p.astype(BF16))

        return issue, consume

    def full_body(kp, carry):
        _lookahead([unit(tpi * kp + t, e, c, False)
                    for t in range(tpi) for c in range(tq // tu) for e in range(2)], 2)
        return carry

    assert tpi % 2 == 0 and q_off % (2 * tk) == 0 and tq % tk == 0 and tq % tu == 0
    assert tq // tk == 1 or tq % (2 * tk) == 0
    n_full = q0 // tk
    trips = n_full // tpi
    lax.fori_loop(0, trips, full_body, 0)
    for r in range(2, tpi, 2):
        @pl.when(n_full - trips * tpi == r)
        def _():
            _lookahead([unit(trips * tpi + t, e, c, False)
                        for t in range(r) for c in range(tq // tu) for e in range(2)], 2)
    diag = []
    for t in range(tq // tk):
        for c in range(tq // tu):
            if t * tk > (c + 1) * tu - 1:
                continue
            masked = (t + 1) * tk - 1 > c * tu
            diag += [unit(n_full + t, e, c, masked) for e in range(2)]
    _lookahead(diag, 2)
    for e in range(2):
        o_ref[0, e * HEAD_DIM:(e + 1) * HEAD_DIM, :] = (acc_scr[e] / l_scr[e]).astype(BF16)


def _fox_attention(qa, ka, vt, *, tq, tk, q_off):
    b, d2, t_q = qa.shape
    d = d2 // 2
    t_k = ka.shape[1]
    nkt = t_k // tk
    tu = tq
    tpi = 4
    kern = functools.partial(_fox_kernel, tq=tq, tk=tk, tu=tu, tpi=tpi, q_off=q_off)
    return pl.pallas_call(
        kern,
        grid=(b, d // LANES, t_q // tq),
        in_specs=[pl.BlockSpec((1, 2 * LANES, tq), lambda bb, j, i: (bb, j, i)),
                  pl.BlockSpec((1, t_k, 2 * LANES), lambda bb, j, i: (bb, 0, j)),
                  pl.BlockSpec((1, nkt, LANES, tk), lambda bb, j, i: (bb, 0, j, 0))],
        out_specs=pl.BlockSpec((1, LANES, tq), lambda bb, j, i: (bb, j, i)),
        out_shape=jax.ShapeDtypeStruct((b, d, t_q), BF16),
        scratch_shapes=[pltpu.VMEM((2, 1, tq), F32), pltpu.VMEM((2, 1, tq), F32),
                        pltpu.VMEM((2, HEAD_DIM, tq), F32)],
        compiler_params=_cp("parallel", "parallel", "arbitrary"),
    )(qa, ka, vt)


def _dsa_kernel(qt_ref, qit_ref, wit_ref, k_ref, vt_ref, ki_ref, bias_ref, o_ref,
                keys_scr, mb_scr, qim_scr, qg_scr, m_scr, l_scr, acc_scr,
                *, q_off, k_top, tk_valid):
    tq = KEY_TILE
    i = pl.program_id(1)
    q0 = q_off + i * tq
    home = q0 // KEY_TILE
    qpos = q0 + lax.broadcasted_iota(I32, (1, tq), 1)
    chunk_shift = int(math.log2(CHUNK))
    qchunk = lax.shift_right_logical(qpos, chunk_shift)
    row = lax.broadcasted_iota(I32, (LANES, tq), 0)
    kf = float(k_top)

    for h in range(H_IDX):
        pair = qit_ref[0, (h // 2) * LANES:(h // 2 + 1) * LANES, :]
        qim_scr[h] = jnp.where((row >= HEAD_DIM) == (h % 2 == 1), pair, jnp.zeros_like(pair))
    qg_scr[...] = jnp.zeros(qg_scr.shape, BF16)
    for g in range(HKV_B):
        r0 = (g % 2) * HEAD_DIM
        for r in range(G_B):
            hq = g * G_B + r
            qg_scr[g, r0:r0 + HEAD_DIM, r * tq:(r + 1) * tq] = (
                qt_ref[0, hq * HEAD_DIM:(hq + 1) * HEAD_DIM, :])

    def score_body(kt, carry):
        kit = ki_ref[0, pl.ds(kt * KEY_TILE, KEY_TILE), :]
        acc = jnp.zeros((KEY_TILE, tq), F32)
        for h in range(H_IDX):
            acc = acc + jnp.maximum(_mm(kit, qim_scr[h]), 0.0) * wit_ref[0, h:h + 1, :]
        acc = jnp.where(acc == 0.0, 0.0, acc)
        bits = lax.bitcast_convert_type(acc, I32)
        key = bits ^ (lax.shift_right_arithmetic(bits, 31) & 0x7FFFFFFF)
        kpos = kt * KEY_TILE + lax.broadcasted_iota(I32, (KEY_TILE, 1), 0)
        adm = (lax.shift_right_logical(kpos, chunk_shift) <= qchunk) & (kpos < tk_valid)
        keys_scr[kt] = jnp.where(adm, key, INT_MIN)
        return carry

    lax.fori_loop(0, home + 1, score_body, 0)
    keys_scr[home + 1] = jnp.full((KEY_TILE, tq), INT_MIN, I32)

    def count(cand, strict):
        def body(kp, a):
            for t in range(2):
                key = keys_scr[2 * kp + t]
                hit = (key > cand) if strict else (key >= cand)
                a = jnp.where(hit, a + 1.0, a)
            return a

        a = lax.fori_loop(0, (home + 2) // 2, body, jnp.zeros((KEY_TILE, tq), F32))
        return jnp.sum(a, axis=0, keepdims=True)

    zero = jnp.zeros((1, tq), I32)
    thr = jnp.where(count(zero, False) >= kf, zero, jnp.full((1, tq), INT_MIN, I32))

    def bit_body(b, t):
        cand = t + lax.shift_left(jnp.int32(1), 30 - b)
        return jnp.where(count(cand, False) >= kf, cand, t)

    thr = lax.fori_loop(0, 31, bit_body, thr)

    no_ties = jnp.max(jnp.abs(count(thr, False) - kf)) == 0.0

    @pl.when(no_ties)
    def _():
        def body(kt, carry):
            key = keys_scr[kt]
            mb_scr[kt] = jnp.where((key >= thr) & (key != INT_MIN), 0.0, NEG_INF)
            return carry

        lax.fori_loop(0, home + 1, body, 0)

    @pl.when(jnp.logical_not(no_ties))
    def _():
        need = kf - count(thr, True)
        ra = lax.broadcasted_iota(I32, (KEY_TILE, KEY_TILE), 0)
        rb = lax.broadcasted_iota(I32, (KEY_TILE, KEY_TILE), 1)
        earlier = (rb < ra).astype(BF16)

        def tie_body(kt, seen):
            key = keys_scr[kt]
            eq = key == thr
            eqf = jnp.where(eq, 1.0, 0.0)
            rank = _mm(earlier, eqf.astype(BF16)) + seen
            sel = ((key > thr) | (eq & (rank < need))) & (key != INT_MIN)
            mb_scr[kt] = jnp.where(sel, 0.0, NEG_INF)
            return seen + jnp.sum(eqf, axis=0, keepdims=True)

        lax.fori_loop(0, home + 1, tie_body, jnp.zeros((1, tq), F32))

    m_scr[...] = jnp.full(m_scr.shape, NEG_INF, F32)
    l_scr[...] = jnp.zeros(l_scr.shape, F32)
    acc_scr[...] = jnp.zeros(acc_scr.shape, F32)

    def unit(kt, g, near):
        def issue():
            ks = k_ref[0, pl.ds(kt * KEY_TILE, KEY_TILE), (g // 2) * LANES:(g // 2 + 1) * LANES]
            return _mm(ks, qg_scr[g])

        def consume(st):
            mb = mb_scr[kt]
            vt = vt_ref[0, kt, g * HEAD_DIM:(g + 1) * HEAD_DIM, :]
            for r in range(G_B):
                hq = g * G_B + r
                s = st[:, r * tq:(r + 1) * tq] + mb
                if near is not None:
                    s = s + bias_ref[hq, near]
                m_old = m_scr[hq]
                m_new = jnp.maximum(m_old, jnp.max(s, axis=0, keepdims=True))
                m_use = jnp.where(m_new == NEG_INF, 0.0, m_new)
                alpha = jnp.exp2(m_old - m_use)
                p = jnp.exp2(s - m_use)
                l_scr[hq] = alpha * l_scr[hq] + jnp.sum(p, axis=0, keepdims=True)
                m_scr[hq] = m_new
                acc_scr[hq] = acc_scr[hq] * alpha + _mm(vt, p.astype(BF16))

        return issue, consume

    def attend(tiles):
        _lookahead([unit(kt, g, near) for kt, near in tiles for g in range(HKV_B)], 2)

    far_tpi = 4

    def far_body(kp, carry):
        attend([(far_tpi * kp + t, None) for t in range(far_tpi)])
        return carry

    n_far = jnp.maximum(home - 1, 0)
    trips = n_far // far_tpi
    lax.fori_loop(0, trips, far_body, 0)
    for r in range(1, far_tpi):
        @pl.when(n_far - trips * far_tpi == r)
        def _():
            attend([(trips * far_tpi + t, None) for t in range(r)])

    @pl.when(home >= 1)
    def _():
        attend([(home - 1, 0), (home, 1)])

    @pl.when(home == 0)
    def _():
        attend([(home, 1)])

    for hq in range(H_B):
        o_ref[0, hq * HEAD_DIM:(hq + 1) * HEAD_DIM, :] = (acc_scr[hq] / l_scr[hq]).astype(BF16)


def _dsa_attention(qt, qit, wit, k, vt, ki, bias, *, q_off, k_top, tk_valid):
    b, d, t_q = qt.shape
    t_k = k.shape[1]
    nkt = t_k // KEY_TILE
    tq = KEY_TILE
    kern = functools.partial(_dsa_kernel, q_off=q_off, k_top=k_top, tk_valid=tk_valid)
    qblk = lambda n: pl.BlockSpec((1, n, tq), lambda bb, i: (bb, 0, i))
    kblk = lambda n: pl.BlockSpec((1, t_k, n), lambda bb, i: (bb, 0, 0))
    return pl.pallas_call(
        kern,
        grid=(b, t_q // tq),
        in_specs=[qblk(d), qblk(qit.shape[1]), qblk(H_IDX), kblk(k.shape[2]),
                  pl.BlockSpec((1, nkt, vt.shape[2], KEY_TILE), lambda bb, i: (bb, 0, 0, 0)),
                  kblk(ki.shape[2]), _WHOLE],
        out_specs=qblk(d),
        out_shape=jax.ShapeDtypeStruct((b, d, t_q), BF16),
        scratch_shapes=[pltpu.VMEM((nkt + 1, KEY_TILE, tq), I32), pltpu.VMEM((nkt, KEY_TILE, tq), F32),
                        pltpu.VMEM((H_IDX, LANES, tq), BF16),
                        pltpu.VMEM((HKV_B, LANES, G_B * tq), BF16),
                        pltpu.VMEM((H_B, 1, tq), F32), pltpu.VMEM((H_B, 1, tq), F32),
                        pltpu.VMEM((H_B, HEAD_DIM, tq), F32)],
        compiler_params=_cp("parallel", "arbitrary"),
    )(qt, qit, wit, k, vt, ki, bias)


def _conv3(u, w_ref, cols, prev):
    tm = u.shape[0]
    row = lax.broadcasted_iota(I32, (tm, 1), 0)
    r1 = pltpu.roll(u, 1, 0)
    r2 = pltpu.roll(u, 2, 0)
    if prev[0] == "stream":
        carry = prev[1]
        c6, c7 = carry[6:7, :], carry[7:8, :]
        um1 = jnp.where(row == 0, c7, r1)
        um2 = jnp.where(row == 0, c6, jnp.where(row == 1, c7, r2))
    else:
        _, seq, pm1, pm2 = prev
        t = row % seq
        um1 = jnp.where(t == 0, pm1, r1)
        um2 = jnp.where(t < 2, pm2, r2)
    return w_ref[0:1, cols] * um2 + w_ref[1:2, cols] * um1 + w_ref[2:3, cols] * u


def _col_chunks(n, width):
    out, c = [], 0
    while c < n:
        out.append((c, min(width, n - c)))
        c += width
    return out


def _ffn_kernel(*refs, mode, seq, tpb, dff, cw):
    if mode == "stream":
        x_ref, g_ref, wup_ref, wc_ref, bc_ref, wdn_ref, o_ref, tail_ref, carry_scr = refs
    else:
        x_ref, g_ref, wup_ref, wc_ref, bc_ref, wdn_ref, pm1_ref, pm2_ref, o_ref, up_ref = refs
    x = x_ref[...]
    tm = x.shape[0]
    h = _rms(x, g_ref[...]).astype(BF16)
    if mode == "stream":
        @pl.when(pl.program_id(0) % tpb == 0)
        def _():
            carry_scr[...] = jnp.zeros(carry_scr.shape, F32)
    acc = [x]

    def chunk(c0, w):
        def issue():
            return [_mm(h, wup_ref[:, base + c0:base + c0 + w]) for base in (0, dff)]

        def consume(ups):
            ys = []
            for base, up in zip((0, dff), ups):
                cols = slice(base + c0, base + c0 + w)
                if mode == "stream":
                    y = _conv3(up, wc_ref, cols, ("stream", carry_scr[:, cols]))
                    carry_scr[:, cols] = up[tm - SUBLANES:tm, :]
                    tail_ref[0, :, cols] = up[tm - SUBLANES:tm, :]
                else:
                    y = _conv3(up, wc_ref, cols, ("seq", seq, pm1_ref[:, cols], pm2_ref[:, cols]))
                    up_ref[:, cols] = up
                ys.append(y + bc_ref[:, cols])
            gate, val = ys
            act = (gate / (1.0 + jnp.exp(-gate))) * val
            acc[0] = acc[0] + _mm(act.astype(BF16), wdn_ref[c0:c0 + w, :])

        return issue, consume

    _lookahead([chunk(c0, w) for c0, w in _col_chunks(dff, cw)], 1)
    o_ref[...] = acc[0]


def _ffn_stream(x, g, w_up, w_conv, b_conv, w_down, *, tm, t_len):
    m, d = x.shape
    c2 = w_up.shape[1]
    tpb = t_len // tm
    kern = functools.partial(_ffn_kernel, mode="stream", seq=None, tpb=tpb, dff=c2 // 2, cw=256)
    return pl.pallas_call(
        kern,
        grid=(m // tm,),
        in_specs=[_rows(tm, d)] + [_WHOLE] * 5,
        out_specs=[_rows(tm, d), pl.BlockSpec((1, SUBLANES, c2), lambda i: (i // tpb, 0, 0))],
        out_shape=[jax.ShapeDtypeStruct((m, d), F32),
                   jax.ShapeDtypeStruct((m // t_len, SUBLANES, c2), F32)],
        scratch_shapes=[pltpu.VMEM((SUBLANES, c2), F32)],
        compiler_params=_cp("arbitrary"),
    )(x, g, w_up, w_conv, b_conv, w_down)


def _ffn_seq(x, g, w_up, w_conv, b_conv, w_down, pm1, pm2, *, tm, seq):
    m, d = x.shape
    c2 = w_up.shape[1]
    kern = functools.partial(_ffn_kernel, mode="seq", seq=seq, tpb=None, dff=c2 // 2, cw=512)
    return pl.pallas_call(
        kern,
        grid=(m // tm,),
        in_specs=[_rows(tm, d)] + [_WHOLE] * 5 + [_rows(tm, c2), _rows(tm, c2)],
        out_specs=[_rows(tm, d), _rows(tm, c2)],
        out_shape=[jax.ShapeDtypeStruct((m, d), F32), jax.ShapeDtypeStruct((m, c2), F32)],
        compiler_params=_cp("parallel"),
    )(x, g, w_up, w_conv, b_conv, w_down, pm1, pm2)


def _sconv_kernel(*refs, mode, seq, tpb, cw):
    if mode == "stream":
        x_ref, g_ref, win_ref, wc_ref, wout_ref, o_ref, tail_ref, carry_scr = refs
    else:
        x_ref, g_ref, win_ref, wc_ref, wout_ref, pm1_ref, pm2_ref, o_ref, p_ref = refs
    x = x_ref[...]
    tm, d = x.shape
    h = _rms(x, g_ref[...]).astype(BF16)
    if mode == "stream":
        @pl.when(pl.program_id(0) % tpb == 0)
        def _():
            carry_scr[...] = jnp.zeros(carry_scr.shape, F32)
    acc = x
    for c0, w in _col_chunks(d, cw):
        cols = slice(c0, c0 + w)
        gb = _mm(h, win_ref[:, c0:c0 + w])
        gc = _mm(h, win_ref[:, d + c0:d + c0 + w])
        u = _mm(h, win_ref[:, 2 * d + c0:2 * d + c0 + w])
        p = gc * u
        if mode == "stream":
            y = _conv3(p, wc_ref, cols, ("stream", carry_scr[:, cols]))
            carry_scr[:, cols] = p[tm - SUBLANES:tm, :]
            tail_ref[0, :, cols] = p[tm - SUBLANES:tm, :]
        else:
            y = _conv3(p, wc_ref, cols, ("seq", seq, pm1_ref[:, cols], pm2_ref[:, cols]))
            p_ref[:, cols] = p
        acc = acc + _mm((gb * y).astype(BF16), wout_ref[c0:c0 + w, :])
    o_ref[...] = acc


def _sconv_stream(x, g, w_in, w_conv, w_out, *, tm, t_len):
    m, d = x.shape
    tpb = t_len // tm
    kern = functools.partial(_sconv_kernel, mode="stream", seq=None, tpb=tpb, cw=512)
    return pl.pallas_call(
        kern,
        grid=(m // tm,),
        in_specs=[_rows(tm, d)] + [_WHOLE] * 4,
        out_specs=[_rows(tm, d), pl.BlockSpec((1, SUBLANES, d), lambda i: (i // tpb, 0, 0))],
        out_shape=[jax.ShapeDtypeStruct((m, d), F32),
                   jax.ShapeDtypeStruct((m // t_len, SUBLANES, d), F32)],
        scratch_shapes=[pltpu.VMEM((SUBLANES, d), F32)],
        compiler_params=_cp("arbitrary"),
    )(x, g, w_in, w_conv, w_out)


def _sconv_seq(x, g, w_in, w_conv, w_out, pm1, pm2, *, tm, seq):
    m, d = x.shape
    kern = functools.partial(_sconv_kernel, mode="seq", seq=seq, tpb=None, cw=512)
    return pl.pallas_call(
        kern,
        grid=(m // tm,),
        in_specs=[_rows(tm, d)] + [_WHOLE] * 4 + [_rows(tm, d), _rows(tm, d)],
        out_specs=[_rows(tm, d), _rows(tm, d)],
        out_shape=[jax.ShapeDtypeStruct((m, d), F32), jax.ShapeDtypeStruct((m, d), F32)],
        compiler_params=_cp("parallel"),
    )(x, g, w_in, w_conv, w_out, pm1, pm2)


def _post_kernel(*refs, nb, has_mix):
    if has_mix:
        x_ref, a_ref, wmix_ref, g_ref, wq_ref, mk_ref, mv_ref, wo_ref, o_ref, oc_scr = refs
        x = x_ref[...] + _mm(a_ref[...], wmix_ref[...])
    else:
        x_ref, g_ref, wq_ref, mk_ref, mv_ref, wo_ref, o_ref, oc_scr = refs
        x = x_ref[...]
    tm, d = x.shape
    dh = d // XA_HEADS
    rpb = tm // nb
    h = _rms(x, g_ref[...]).astype(BF16)
    q = (_mm(h, wq_ref[...]) * (dh ** -0.5)).astype(BF16)
    for b in range(nb):
        rows = slice(b * rpb, (b + 1) * rpb)
        for hh in range(XA_HEADS):
            cols = slice(hh * dh, (hh + 1) * dh)
            s = _mm_nt(q[rows, cols], mk_ref[b, :, cols].astype(BF16))
            e = jnp.exp(s - jnp.max(s, axis=1, keepdims=True))
            p = e / jnp.sum(e, axis=1, keepdims=True)
            oc_scr[rows, cols] = _mm(p.astype(BF16), mv_ref[b, :, cols].astype(BF16)).astype(BF16)
    o_ref[...] = x + _mm(oc_scr[...], wo_ref[...])


def _post(x, a, w_mix, g, w_q, mk, mv, w_o, *, tm, nb, tiles_per_mem):
    m, d = x.shape
    n_mem = mk.shape[1]
    has_mix = a is not None
    mem_spec = pl.BlockSpec((nb, n_mem, d), lambda i: (i // tiles_per_mem, 0, 0))
    ins = [x] + ([a, w_mix] if has_mix else []) + [g, w_q, mk, mv, w_o]
    specs = ([_rows(tm, d)] + ([_rows(tm, d), _WHOLE] if has_mix else [])
             + [_WHOLE, _WHOLE, mem_spec, mem_spec, _WHOLE])
    return pl.pallas_call(
        functools.partial(_post_kernel, nb=nb, has_mix=has_mix),
        grid=(m // tm,),
        in_specs=specs,
        out_specs=_rows(tm, d),
        out_shape=jax.ShapeDtypeStruct((m, d), F32),
        scratch_shapes=[pltpu.VMEM((tm, d), BF16)],
        compiler_params=_cp("parallel"),
    )(*ins)


def _norm_kernel(x_ref, g_ref, o_ref):
    o_ref[...] = _rms(x_ref[...], g_ref[...])


def _final_norm(x, g, tm):
    m, d = x.shape
    return pl.pallas_call(
        _norm_kernel, grid=(m // tm,), in_specs=[_rows(tm, d), _WHOLE], out_specs=_rows(tm, d),
        out_shape=jax.ShapeDtypeStruct((m, d), F32), compiler_params=_cp("parallel"),
    )(x, g)


def _t5_bucket(rel):
    half = NUM_BUCKETS // 2
    ret = jnp.where(rel > 0, half, 0)
    n = jnp.abs(rel)
    max_exact = half // 2
    nf = jnp.maximum(n, 1).astype(jnp.float32)
    large = max_exact + (jnp.log(nf / max_exact) / math.log(MAX_DISTANCE / max_exact)
                         * (half - max_exact)).astype(jnp.int32)
    large = jnp.minimum(large, half - 1)
    return ret + jnp.where(n < max_exact, n, large)


def _near_bias(rel_bias):
    sl = jnp.arange(KEY_TILE, dtype=jnp.int32)[None, :, None]
    ql = jnp.arange(KEY_TILE, dtype=jnp.int32)[None, None, :]
    off = jnp.array([-KEY_TILE, 0], jnp.int32)[:, None, None]
    rel = sl + off - ql
    far = _t5_bucket(jnp.array(-2 * KEY_TILE, jnp.int32))
    tab = (rel_bias[_t5_bucket(rel)] - rel_bias[far]) * LOG2E
    return jnp.transpose(tab, (3, 0, 1, 2)).astype(F32)


def _seq_context(prev, seq):
    b, _, c = prev.shape
    z = jnp.zeros((b, seq, c), prev.dtype)
    pm1 = z.at[:, 0].set(prev[:, 1])
    pm2 = z.at[:, 0].set(prev[:, 0]).at[:, 1].set(prev[:, 1])
    return pm1.reshape(b * seq, c), pm2.reshape(b * seq, c)


def _pad_keys(cache, new, t_pad):
    b, p, c = cache.shape
    t = new.shape[1]
    pad = jnp.zeros((b, t_pad - p - t, c), BF16)
    return jnp.concatenate([cache.astype(BF16), new.astype(BF16), pad], axis=1)


def _split3(x):
    def top(v):
        bits = lax.bitcast_convert_type(v, jnp.uint32) & jnp.uint32(0xFFFF0000)
        return lax.bitcast_convert_type(bits, F32)

    hi = top(x)
    mid = top(x - hi)
    lo = x - hi - mid
    return [hi.astype(BF16), mid.astype(BF16), lo.astype(BF16)]


def _feature_major(a, t_pad):
    a = jnp.swapaxes(a, 1, 2)
    return jnp.pad(a, ((0, 0), (0, 0), (0, t_pad - a.shape[2])))


def _key_tiles_feature_major(a, tk):
    b, t, c = a.shape
    return jnp.swapaxes(a.reshape(b, t // tk, tk, c), 2, 3)


def kernel(x_prompt, x_sample, mem_prompt, cache_k_A, cache_v_A, cache_logf_A, cache_k_B, cache_v_B,
           cache_kidx_B, state_conv_C, state_ffconv, cache_mem_k, cache_mem_v, g_mix, g_xa, g_ffn,
           g_mem, g_final, w_in_A, b_f_A, w_out_A, w_in_B, w_out_B, rel_bias, w_in_C, w_conv_C,
           w_out_C, w_q_xa, w_k_xa, w_v_xa, w_o_xa, w_up, w_conv_ff, b_conv_ff, w_down):
    bp, tp, d = x_prompt.shape
    bs, ts, _ = x_sample.shape
    depth = g_mix.shape[0]
    past = cache_k_A.shape[2]
    n_mem = mem_prompt.shape[1]
    assert d == H_A * HEAD_DIM == H_B * HEAD_DIM and ts % SUBLANES == 0 and ts <= KEY_TILE
    assert past % KEY_TILE == 0 and tp % KEY_TILE == 0
    mixers = tuple("ABC"[l % 3] for l in range(depth))
    slot = tuple(mixers[:l].count(mixers[l]) for l in range(depth))
    mp, ms = bp * tp, bs * ts
    tm_p = min(256, tp)
    tm_s = min(ms, 8 * ts)
    tq_a = min(512, tp)
    tk_a = KEY_TILE
    t_all = past + ts
    t_pad = -(-t_all // KEY_TILE) * KEY_TILE
    k_top_p = min(TOPK_MAX, tp // 4)
    k_top_s = min(TOPK_MAX, t_all // 4)
    nb_s = min(4, bs)

    bf = lambda a: a.astype(BF16)
    row = lambda a: a.reshape(1, -1)
    xp = x_prompt.reshape(mp, d)
    xs = x_sample.reshape(ms, d)
    mem = mem_prompt.reshape(bp * n_mem, d)
    bias_near = _near_bias(rel_bias)

    outs = {n: [] for n in ("kA_p", "vA_p", "fA_p", "kB_p", "vB_p", "iB_p", "cC_p", "ff_p", "mk_p",
                            "mv_p", "kA_s", "vA_s", "fA_s", "kB_s", "vB_s", "iB_s", "cC_s", "ff_s")}

    def fox(x, g, j, b, t, cache):
        m = b * t
        q, k, v, kb, vb, lf = _a_in(x, g, bf(w_in_A[j][:, :3 * d]), bf(w_in_A[j][:, 3 * d:]),
                                    row(b_f_A[j]), min(256, m))
        lf_t = jnp.swapaxes(lf.reshape(b, t, H_A), 1, 2)
        if cache is None:
            q_off, t_k, tq, tk, tq_pad = 0, t, tq_a, tk_a, t
            lf_all = lf_t
            k_all, v_all = kb.reshape(b, t, d), vb.reshape(b, t, d)
        else:
            ck, cv, clf = cache
            q_off, t_k, tq, tk, tq_pad = past, t_pad, KEY_TILE, KEY_TILE, KEY_TILE
            lf_all = jnp.concatenate([jnp.swapaxes(clf, 1, 2).astype(F32), lf_t,
                                      jnp.zeros((b, H_A, t_pad - t_all), F32)], axis=2)
            k_all = _pad_keys(ck.reshape(b, past, d), kb.reshape(b, t, d), t_pad)
            v_all = _pad_keys(cv.reshape(b, past, d), vb.reshape(b, t, d), t_pad)
        f_all = _cumsum_lanes(lf_all.reshape(b * H_A, t_k), min(64, b * H_A)).reshape(b, H_A, t_k)
        fk3 = _split3(f_all * LOG2E)
        fq3 = [jnp.pad(p[:, :, q_off:q_off + t], ((0, 0), (0, 0), (0, tq_pad - t))) for p in fk3]
        one_k, one_q = jnp.ones((b, H_A, t_k), BF16), jnp.ones((b, H_A, tq_pad), BF16)
        spare = HEAD_DIM - 6
        k_slots = jnp.stack([-p for p in fk3] + [one_k] * 3, axis=-1)
        k_slots = jnp.pad(jnp.swapaxes(k_slots, 1, 2), ((0, 0), (0, 0), (0, 0), (0, spare)))
        ka = jnp.concatenate([k_all.reshape(b, t_k, H_A, HEAD_DIM), k_slots], axis=-1)
        q_slots = jnp.stack([one_q] * 3 + fq3, axis=2)
        q_slots = jnp.pad(q_slots, ((0, 0), (0, 0), (0, spare), (0, 0)))
        qt = _feature_major(q.reshape(b, t, d), tq_pad).reshape(b, H_A, HEAD_DIM, tq_pad)
        qa = jnp.concatenate([qt, q_slots], axis=2)
        ot = _fox_attention(qa.reshape(b, 2 * d, tq_pad), ka.reshape(b, t_k, 2 * d),
                            _key_tiles_feature_major(v_all, tk), tq=tq, tk=tk, q_off=q_off)
        o = jnp.swapaxes(ot[:, :, :t], 1, 2)
        return o.reshape(m, d), k, v, lf

    def dsa(x, g, j, b, t, cache):
        m = b * t
        w = w_in_B[j]
        dkv = HKV_B * HEAD_DIM
        c0, c2, c3 = d, d + 2 * dkv, d + 2 * dkv + H_IDX * D_IDX
        q, k, v, kb, vb, qi, ki, kib, wi = _b_in(
            x, g, bf(w[:, :c0]), bf(w[:, c0:c2]), bf(w[:, c2:c3]),
            bf(jnp.concatenate([w[:, c3:c3 + D_IDX]] * 2, axis=1)), bf(w[:, c3 + D_IDX:]), min(256, m))
        if cache is None:
            q_off, k_top, tk_valid, tq_pad = 0, k_top_p, t, t
            k_all, v_all, ki_all = kb.reshape(b, t, dkv), vb.reshape(b, t, dkv), kib.reshape(b, t, -1)
        else:
            ck, cv, cki = cache
            q_off, k_top, tk_valid, tq_pad = past, k_top_s, t_all, KEY_TILE
            k_all = _pad_keys(ck.reshape(b, past, dkv), kb.reshape(b, t, dkv), t_pad)
            v_all = _pad_keys(cv.reshape(b, past, dkv), vb.reshape(b, t, dkv), t_pad)
            cki2 = jnp.concatenate([cki, cki], axis=-1)
            ki_all = _pad_keys(cki2, kib.reshape(b, t, -1), t_pad)
        ot = _dsa_attention(_feature_major(q.reshape(b, t, d), tq_pad),
                            _feature_major(qi.reshape(b, t, -1), tq_pad),
                            _feature_major(wi.reshape(b, t, H_IDX), tq_pad),
                            k_all, _key_tiles_feature_major(v_all, KEY_TILE), ki_all, bias_near,
                            q_off=q_off, k_top=k_top, tk_valid=tk_valid)
        o = jnp.swapaxes(ot[:, :, :t], 1, 2)
        return o.reshape(m, d), k, v, ki

    for l in range(depth):
        mix, j = mixers[l], slot[l]
        gm, gx, gf = row(g_mix[l]), row(g_xa[l]), row(g_ffn[l])
        mk, mv = _proj(mem, row(g_mem[l]), [bf(w_k_xa[l]), bf(w_v_xa[l])], min(256, bp * n_mem))
        outs["mk_p"].append(mk.reshape(bp, n_mem, XA_HEADS, d // XA_HEADS))
        outs["mv_p"].append(mv.reshape(bp, n_mem, XA_HEADS, d // XA_HEADS))
        mks, mvs = cache_mem_k[l].reshape(bs, n_mem, d), cache_mem_v[l].reshape(bs, n_mem, d)
        post = dict(g=gx, w_q=bf(w_q_xa[l]), w_o=bf(w_o_xa[l]))

        if mix == "A":
            ap, k, v, f = fox(xp, gm, j, bp, tp, None)
            outs["kA_p"].append(k.reshape(bp, tp, H_A, HEAD_DIM))
            outs["vA_p"].append(v.reshape(bp, tp, H_A, HEAD_DIM))
            outs["fA_p"].append(f.reshape(bp, tp, H_A))
            a_s, k, v, f = fox(xs, gm, j, bs, ts, (cache_k_A[j], cache_v_A[j], cache_logf_A[j]))
            outs["kA_s"].append(k.reshape(bs, ts, H_A, HEAD_DIM))
            outs["vA_s"].append(v.reshape(bs, ts, H_A, HEAD_DIM))
            outs["fA_s"].append(f.reshape(bs, ts, H_A))
            w_mix = bf(w_out_A[j])
        elif mix == "B":
            ap, k, v, ki = dsa(xp, gm, j, bp, tp, None)
            outs["kB_p"].append(k.reshape(bp, tp, HKV_B, HEAD_DIM))
            outs["vB_p"].append(v.reshape(bp, tp, HKV_B, HEAD_DIM))
            outs["iB_p"].append(ki.reshape(bp, tp, D_IDX))
            a_s, k, v, ki = dsa(xs, gm, j, bs, ts, (cache_k_B[j], cache_v_B[j], cache_kidx_B[j]))
            outs["kB_s"].append(k.reshape(bs, ts, HKV_B, HEAD_DIM))
            outs["vB_s"].append(v.reshape(bs, ts, HKV_B, HEAD_DIM))
            outs["iB_s"].append(ki.reshape(bs, ts, D_IDX))
            w_mix = bf(w_out_B[j])
        else:
            xp, tail = _sconv_stream(xp, gm, bf(w_in_C[j]), w_conv_C[j], bf(w_out_C[j]),
                                     tm=tm_p, t_len=tp)
            outs["cC_p"].append(tail[:, SUBLANES - 2:, :])
            pm1, pm2 = _seq_context(state_conv_C[j], ts)
            xs, p_all = _sconv_seq(xs, gm, bf(w_in_C[j]), w_conv_C[j], bf(w_out_C[j]), pm1, pm2,
                                   tm=tm_s, seq=ts)
            outs["cC_s"].append(p_all.reshape(bs, ts, d)[:, ts - 2:, :])
            ap = a_s = w_mix = None

        xp = _post(xp, ap, w_mix, mk=mk.reshape(bp, n_mem, d), mv=mv.reshape(bp, n_mem, d),
                   tm=tm_p, nb=1, tiles_per_mem=tp // tm_p, **post)
        xs = _post(xs, a_s, w_mix, mk=mks, mv=mvs, tm=nb_s * ts, nb=nb_s, tiles_per_mem=1, **post)

        wu, wc, bc, wd = bf(w_up[l]), w_conv_ff[l], row(b_conv_ff[l]), bf(w_down[l])
        xp, tail = _ffn_stream(xp, gf, wu, wc, bc, wd, tm=tm_p, t_len=tp)
        outs["ff_p"].append(tail[:, SUBLANES - 2:, :])
        pm1, pm2 = _seq_context(state_ffconv[l], ts)
        xs, up_all = _ffn_seq(xs, gf, wu, wc, bc, wd, pm1, pm2, tm=tm_s, seq=ts)
        outs["ff_s"].append(up_all.reshape(bs, ts, -1)[:, ts - 2:, :])

    y_p = _final_norm(xp, row(g_final), tm_p).reshape(bp, tp, d)
    y_s = _final_norm(xs, row(g_final), min(256, ms)).reshape(bs, ts, d)
    st = lambda n: jnp.stack(outs[n])
    return (y_p, y_s, st("kA_p"), st("vA_p"), st("fA_p"), st("kB_p"), st("vB_p"), st("iB_p"),
            st("cC_p"), st("ff_p"), st("mk_p"), st("mv_p"), st("kA_s"), st("vA_s"), st("fA_s"),
            st("kB_s"), st("vB_s"), st("iB_s"), st("cC_s"), st("ff_s"))
```

```python
import functools
import math

import jax
import jax.numpy as jnp
import numpy as np
from jax import lax
from jax.experimental import pallas as pl
from jax.experimental.pallas import tpu as pltpu

F32, BF16, I32 = jnp.float32, jnp.bfloat16, jnp.int32

CHUNK = 64
H_A = 16
H_B = 16
HKV_B = 4
G_B = H_B // HKV_B
H_IDX = 8
D_IDX = 64
TOPK_MAX = 256
NUM_BUCKETS = 32
MAX_DISTANCE = 128
XA_HEADS = 4
EPS = 1e-6

LANES = 128
SUBLANES = 8
HEAD_DIM = 64
KEY_TILE = 128
VMEM_LIMIT_BYTES = 56 * 1024 * 1024

INT_MIN = -2 ** 31
NEG_INF = float("-inf")
LOG2E = 1.0 / math.log(2.0)
Q_SCALE = HEAD_DIM ** -0.5 * LOG2E

_WHOLE = pl.BlockSpec(memory_space=pltpu.VMEM)


def _cp(*sem):
    return pltpu.CompilerParams(dimension_semantics=sem, vmem_limit_bytes=VMEM_LIMIT_BYTES)


def _rows(tm, n):
    return pl.BlockSpec((tm, n), lambda i: (i, 0))


def _rms(x, g):
    return x * lax.rsqrt(jnp.mean(x * x, axis=-1, keepdims=True) + EPS) * g


def _mm(a, b):
    return jnp.dot(a, b, preferred_element_type=F32)


def _mm_nt(a, b):
    return lax.dot_general(a, b, (((1,), (1,)), ((), ())), preferred_element_type=F32)


def _proj_kernel(x_ref, g_ref, *refs, n_w):
    w_refs, o_refs = refs[:n_w], refs[n_w:]
    h = _rms(x_ref[...], g_ref[...]).astype(BF16)
    for w_ref, o_ref in zip(w_refs, o_refs):
        o_ref[...] = _mm(h, w_ref[...])


def _proj(x, g, ws, tm):
    m, d = x.shape
    return pl.pallas_call(
        functools.partial(_proj_kernel, n_w=len(ws)),
        grid=(m // tm,),
        in_specs=[_rows(tm, d), _WHOLE] + [_WHOLE] * len(ws),
        out_specs=[_rows(tm, w.shape[1]) for w in ws],
        out_shape=[jax.ShapeDtypeStruct((m, w.shape[1]), F32) for w in ws],
        compiler_params=_cp("parallel"),
    )(x, g, *ws)


def _a_in_kernel(x_ref, g_ref, w_ref, wf_ref, bf_ref, q_ref, k_ref, v_ref, kb_ref, vb_ref, lf_ref):
    d = q_ref.shape[-1]
    h = _rms(x_ref[...], g_ref[...]).astype(BF16)
    q_ref[...] = (_mm(h, w_ref[:, 0:d]) * Q_SCALE).astype(BF16)
    k = _mm(h, w_ref[:, d:2 * d])
    k_ref[...] = k
    kb_ref[...] = k.astype(BF16)
    v = _mm(h, w_ref[:, 2 * d:3 * d])
    v_ref[...] = v
    vb_ref[...] = v.astype(BF16)
    fl = _mm(h, wf_ref[...]) + bf_ref[...]
    lf_ref[...] = jnp.minimum(fl, 0.0) - jnp.log1p(jnp.exp(-jnp.abs(fl)))


def _a_in(x, g, w_qkv, w_f, b_f, tm):
    m, d = x.shape
    sds = jax.ShapeDtypeStruct
    return pl.pallas_call(
        _a_in_kernel,
        grid=(m // tm,),
        in_specs=[_rows(tm, d), _WHOLE, _WHOLE, _WHOLE, _WHOLE],
        out_specs=[_rows(tm, d)] * 5 + [_rows(tm, H_A)],
        out_shape=[sds((m, d), BF16), sds((m, d), F32), sds((m, d), F32),
                   sds((m, d), BF16), sds((m, d), BF16), sds((m, H_A), F32)],
        compiler_params=_cp("parallel"),
    )(x, g, w_qkv, w_f, b_f)


def _top_bits(v):
    bits = lax.bitcast_convert_type(v, I32) & jnp.int32(-65536)
    return lax.bitcast_convert_type(bits, F32)


def _a_in_seq_kernel(x_ref, g_ref, wqx_ref, wkx_ref, wv_ref, wvt_ref, wf_ref, bf_ref, sq_ref, sk_ref,
                     cq_ref, ck_ref, k_ref, v_ref, lf_ref, qa_ref, ka_ref, vt_ref, carry_scr, *, tpb):
    tm = x_ref.shape[0]
    h = _rms(x_ref[...], g_ref[...]).astype(BF16)
    lane = lax.broadcasted_iota(I32, (tm, LANES), 1)

    @pl.when(pl.program_id(0) % tpb == 0)
    def _():
        carry_scr[...] = jnp.zeros(carry_scr.shape, F32)

    fl = _mm(h, wf_ref[...]) + bf_ref[...]
    lf = jnp.minimum(fl, 0.0) - jnp.log1p(jnp.exp(-jnp.abs(fl)))
    lf = jnp.where(lane < H_A, lf, 0.0)
    lf_ref[...] = lf[:, 0:H_A]
    ra = lax.broadcasted_iota(I32, (tm, tm), 0)
    rb = lax.broadcasted_iota(I32, (tm, tm), 1)
    tri = (rb <= ra).astype(F32)
    f_cum = jnp.dot(tri, lf, precision=lax.Precision.HIGHEST,
                    preferred_element_type=F32) + carry_scr[0:1, :]
    carry_scr[...] = jnp.broadcast_to(f_cum[tm - 1:tm, :], carry_scr.shape)
    f2 = f_cum * LOG2E
    hi = _top_bits(f2)
    mid = _top_bits(f2 - hi)
    lo = f2 - hi - mid
    pieces = jnp.concatenate([hi, mid, lo], axis=1).astype(BF16)

    qa_ref[...] = (_mm(h, wqx_ref[...]) * Q_SCALE + _mm(pieces, sq_ref[...])
                   + cq_ref[...]).astype(BF16)
    kx = _mm(h, wkx_ref[...])
    ka_ref[...] = (kx + _mm(pieces, sk_ref[...]) + ck_ref[...]).astype(BF16)
    for p in range(k_ref.shape[1] // LANES):
        a = kx[:, (2 * p) * LANES:(2 * p + 1) * LANES]
        b = pltpu.roll(kx[:, (2 * p + 1) * LANES:(2 * p + 2) * LANES], HEAD_DIM, 1)
        k_ref[:, p * LANES:(p + 1) * LANES] = jnp.where(lane < HEAD_DIM, a, b)
    v_ref[...] = _mm(h, wv_ref[...])
    vt_ref[0] = _mm_nt(wvt_ref[...], h).astype(BF16)


def _a_in_seq(x, g, w_qx, w_kx, w_v, w_vt, w_f, b_f, s_q, s_k, c_q, c_k, *, tm, t_len):
    m, d = x.shape
    sds = jax.ShapeDtypeStruct
    kern = functools.partial(_a_in_seq_kernel, tpb=t_len // tm)
    return pl.pallas_call(
        kern,
        grid=(m // tm,),
        in_specs=[_rows(tm, d)] + [_WHOLE] * 11,
        out_specs=[_rows(tm, d), _rows(tm, d), _rows(tm, H_A), _rows(tm, 2 * d), _rows(tm, 2 * d),
                   pl.BlockSpec((1, d, tm), lambda i: (i, 0, 0))],
        out_shape=[sds((m, d), F32), sds((m, d), F32), sds((m, H_A), F32),
                   sds((m, 2 * d), BF16), sds((m, 2 * d), BF16), sds((m // tm, d, tm), BF16)],
        scratch_shapes=[pltpu.VMEM((SUBLANES, LANES), F32)],
        compiler_params=_cp("arbitrary"),
    )(x, g, w_qx, w_kx, w_v, w_vt, w_f, b_f, s_q, s_k, c_q, c_k)


def _b_in_kernel(x_ref, g_ref, wq_ref, wkv_ref, wqi_ref, wki_ref, wwi_ref,
                 q_ref, k_ref, v_ref, kb_ref, vb_ref, qi_ref, ki_ref, kib_ref, wi_ref):
    dkv = k_ref.shape[-1]
    h = _rms(x_ref[...], g_ref[...]).astype(BF16)
    q_ref[...] = (_mm(h, wq_ref[...]) * Q_SCALE).astype(BF16)
    kv = _mm(h, wkv_ref[...])
    k_ref[...] = kv[:, 0:dkv]
    kb_ref[...] = kv[:, 0:dkv].astype(BF16)
    v_ref[...] = kv[:, dkv:2 * dkv]
    vb_ref[...] = kv[:, dkv:2 * dkv].astype(BF16)
    qi_ref[...] = (_mm(h, wqi_ref[...]) * (D_IDX ** -0.5)).astype(BF16)
    ki2 = _mm(h, wki_ref[...])
    ki_ref[...] = ki2[:, 0:D_IDX]
    kib_ref[...] = ki2.astype(BF16)
    wi_ref[...] = _mm(h, wwi_ref[...]) * (H_IDX ** -0.5)


def _b_in(x, g, w_q, w_kv, w_qi, w_ki2, w_wi, tm):
    m, d = x.shape
    dkv = w_kv.shape[1] // 2
    dqi = w_qi.shape[1]
    sds = jax.ShapeDtypeStruct
    return pl.pallas_call(
        _b_in_kernel,
        grid=(m // tm,),
        in_specs=[_rows(tm, d)] + [_WHOLE] * 6,
        out_specs=[_rows(tm, d), _rows(tm, dkv), _rows(tm, dkv), _rows(tm, dkv), _rows(tm, dkv),
                   _rows(tm, dqi), _rows(tm, D_IDX), _rows(tm, 2 * D_IDX), _rows(tm, H_IDX)],
        out_shape=[sds((m, d), BF16), sds((m, dkv), F32), sds((m, dkv), F32),
                   sds((m, dkv), BF16), sds((m, dkv), BF16), sds((m, dqi), BF16),
                   sds((m, D_IDX), F32), sds((m, 2 * D_IDX), BF16), sds((m, H_IDX), F32)],
        compiler_params=_cp("parallel"),
    )(x, g, w_q, w_kv, w_qi, w_ki2, w_wi)


def _cumsum_kernel(x_ref, o_ref):
    n = x_ref.shape[-1]
    a = lax.broadcasted_iota(I32, (LANES, LANES), 0)
    b = lax.broadcasted_iota(I32, (LANES, LANES), 1)
    tri = (a <= b).astype(F32)
    carry = jnp.zeros((x_ref.shape[0], 1), F32)
    for c in range(n // LANES):
        sl = slice(c * LANES, (c + 1) * LANES)
        y = jnp.dot(x_ref[:, sl], tri, precision=lax.Precision.HIGHEST,
                    preferred_element_type=F32) + carry
        o_ref[:, sl] = y
        carry = y[:, LANES - 1:LANES]


def _cumsum_lanes(x, rb):
    r, n = x.shape
    return pl.pallas_call(
        _cumsum_kernel,
        grid=(r // rb,),
        in_specs=[_rows(rb, n)],
        out_specs=_rows(rb, n),
        out_shape=jax.ShapeDtypeStruct((r, n), F32),
        compiler_params=_cp("parallel"),
    )(x)


def _lookahead(units, depth):
    pending = []
    for idx in range(len(units) + depth):
        if idx < len(units):
            pending.append(units[idx][0]())
        if idx >= depth:
            units[idx - depth][1](pending[idx - depth])


def _fox_kernel(qa_ref, ka_ref, vt_ref, o_ref, m_scr, l_scr, acc_scr, *, tq, tk, tu, tpi, q_off):
    i = pl.program_id(2)
    m_scr[...] = jnp.full(m_scr.shape, NEG_INF, F32)
    l_scr[...] = jnp.zeros(l_scr.shape, F32)
    acc_scr[...] = jnp.zeros(acc_scr.shape, F32)
    q0 = q_off + i * tq

    def unit(kk, e, c, masked):
        cols = slice(c * tu, (c + 1) * tu)
        slab = slice(e * LANES, (e + 1) * LANES)

        def issue():
            return _mm_nt(ka_ref[pl.ds(kk * tk, tk), slab], qa_ref[cols, slab])

        def consume(s):
            if masked:
                kpos = kk * tk + lax.broadcasted_iota(I32, (tk, 1), 0)
                qpos = q0 + c * tu + lax.broadcasted_iota(I32, (1, tu), 1)
                s = jnp.where(kpos <= qpos, s, NEG_INF)
            m_old = m_scr[e, :, cols]
            m_new = jnp.maximum(m_old, jnp.max(s, axis=0, keepdims=True))
            alpha = jnp.exp2(m_old - m_new)
            p = jnp.exp2(s - m_new)
            l_scr[e, :, cols] = alpha * l_scr[e, :, cols] + jnp.sum(p, axis=0, keepdims=True)
            m_scr[e, :, cols] = m_new
            ve = vt_ref[kk, e * HEAD_DIM:(e + 1) * HEAD_DIM, :]
            acc_scr[e, :, cols] = acc_scr[e, :, cols] * alpha + _mm(ve, p.astype(BF16))

        return issue, consume

    def full_body(kp, carry):
        _lookahead([unit(tpi * kp + t, e, c, False)
                    for t in range(tpi) for c in range(tq // tu) for e in range(2)], 2)
        return carry

    assert tpi % 2 == 0 and q_off % (2 * tk) == 0 and tq % tk == 0 and tq % tu == 0
    assert tq // tk == 1 or tq % (2 * tk) == 0
    n_full = q0 // tk
    trips = n_full // tpi
    lax.fori_loop(0, trips, full_body, 0)
    for r in range(2, tpi, 2):
        @pl.when(n_full - trips * tpi == r)
        def _():
            _lookahead([unit(trips * tpi + t, e, c, False)
                        for t in range(r) for c in range(tq // tu) for e in range(2)], 2)
    diag = []
    for t in range(tq // tk):
        for c in range(tq // tu):
            if t * tk > (c + 1) * tu - 1:
                continue
            masked = (t + 1) * tk - 1 > c * tu
            diag += [unit(n_full + t, e, c, masked) for e in range(2)]
    _lookahead(diag, 2)
    ot = jnp.concatenate([acc_scr[e] / l_scr[e] for e in range(2)], axis=0)
    o_ref[...] = ot.T.astype(BF16)


def _fox_attention(qa, ka, vt, *, b, tq, tk, q_off):
    d = qa.shape[1] // 2
    t_q, t_k = qa.shape[0] // b, ka.shape[0] // b
    nq, nkt = t_q // tq, t_k // tk
    tu = tq
    tpi = 4
    kern = functools.partial(_fox_kernel, tq=tq, tk=tk, tu=tu, tpi=tpi, q_off=q_off)
    return pl.pallas_call(
        kern,
        grid=(b, d // LANES, nq),
        in_specs=[pl.BlockSpec((tq, 2 * LANES), lambda bb, j, i: (bb * nq + i, j)),
                  pl.BlockSpec((t_k, 2 * LANES), lambda bb, j, i: (bb, j)),
                  pl.BlockSpec((nkt, LANES, tk), lambda bb, j, i: (bb, j, 0))],
        out_specs=pl.BlockSpec((tq, LANES), lambda bb, j, i: (bb * nq + i, j)),
        out_shape=jax.ShapeDtypeStruct((b * t_q, d), BF16),
        scratch_shapes=[pltpu.VMEM((2, 1, tq), F32), pltpu.VMEM((2, 1, tq), F32),
                        pltpu.VMEM((2, HEAD_DIM, tq), F32)],
        compiler_params=_cp("parallel", "parallel", "arbitrary"),
    )(qa, ka, vt)


def _dsa_kernel(qt_ref, qit_ref, wit_ref, k_ref, vt_ref, ki_ref, bias_ref, o_ref,
                keys_scr, mb_scr, qim_scr, qg_scr, m_scr, l_scr, acc_scr,
                *, q_off, k_top, tk_valid):
    tq = KEY_TILE
    i = pl.program_id(1)
    q0 = q_off + i * tq
    home = q0 // KEY_TILE
    qpos = q0 + lax.broadcasted_iota(I32, (1, tq), 1)
    chunk_shift = int(math.log2(CHUNK))
    qchunk = lax.shift_right_logical(qpos, chunk_shift)
    row = lax.broadcasted_iota(I32, (LANES, tq), 0)
    kf = float(k_top)

    for h in range(H_IDX):
        pair = qit_ref[0, (h // 2) * LANES:(h // 2 + 1) * LANES, :]
        qim_scr[h] = jnp.where((row >= HEAD_DIM) == (h % 2 == 1), pair, jnp.zeros_like(pair))
    qg_scr[...] = jnp.zeros(qg_scr.shape, BF16)
    for g in range(HKV_B):
        r0 = (g % 2) * HEAD_DIM
        for r in range(G_B):
            hq = g * G_B + r
            qg_scr[g, r0:r0 + HEAD_DIM, r * tq:(r + 1) * tq] = (
                qt_ref[0, hq * HEAD_DIM:(hq + 1) * HEAD_DIM, :])

    def score_body(kt, carry):
        kit = ki_ref[0, pl.ds(kt * KEY_TILE, KEY_TILE), :]
        acc = jnp.zeros((KEY_TILE, tq), F32)
        for h in range(H_IDX):
            acc = acc + jnp.maximum(_mm(kit, qim_scr[h]), 0.0) * wit_ref[0, h:h + 1, :]
        acc = jnp.where(acc == 0.0, 0.0, acc)
        bits = lax.bitcast_convert_type(acc, I32)
        key = bits ^ (lax.shift_right_arithmetic(bits, 31) & 0x7FFFFFFF)
        kpos = kt * KEY_TILE + lax.broadcasted_iota(I32, (KEY_TILE, 1), 0)
        adm = (lax.shift_right_logical(kpos, chunk_shift) <= qchunk) & (kpos < tk_valid)
        keys_scr[kt] = jnp.where(adm, key, INT_MIN)
        return carry

    lax.fori_loop(0, home + 1, score_body, 0)
    keys_scr[home + 1] = jnp.full((KEY_TILE, tq), INT_MIN, I32)

    def count(cand, strict):
        def body(kp, a):
            for t in range(2):
                key = keys_scr[2 * kp + t]
                hit = (key > cand) if strict else (key >= cand)
                a = jnp.where(hit, a + 1.0, a)
            return a

        a = lax.fori_loop(0, (home + 2) // 2, body, jnp.zeros((KEY_TILE, tq), F32))
        return jnp.sum(a, axis=0, keepdims=True)

    zero = jnp.zeros((1, tq), I32)
    thr = jnp.where(count(zero, False) >= kf, zero, jnp.full((1, tq), INT_MIN, I32))

    def bit_body(b, t):
        cand = t + lax.shift_left(jnp.int32(1), 30 - b)
        return jnp.where(count(cand, False) >= kf, cand, t)

    thr = lax.fori_loop(0, 31, bit_body, thr)

    no_ties = jnp.max(jnp.abs(count(thr, False) - kf)) == 0.0

    @pl.when(no_ties)
    def _():
        def body(kt, carry):
            key = keys_scr[kt]
            mb_scr[kt] = jnp.where((key >= thr) & (key != INT_MIN), 0.0, NEG_INF)
            return carry

        lax.fori_loop(0, home + 1, body, 0)

    @pl.when(jnp.logical_not(no_ties))
    def _():
        need = kf - count(thr, True)
        ra = lax.broadcasted_iota(I32, (KEY_TILE, KEY_TILE), 0)
        rb = lax.broadcasted_iota(I32, (KEY_TILE, KEY_TILE), 1)
        earlier = (rb < ra).astype(BF16)

        def tie_body(kt, seen):
            key = keys_scr[kt]
            eq = key == thr
            eqf = jnp.where(eq, 1.0, 0.0)
            rank = _mm(earlier, eqf.astype(BF16)) + seen
            sel = ((key > thr) | (eq & (rank < need))) & (key != INT_MIN)
            mb_scr[kt] = jnp.where(sel, 0.0, NEG_INF)
            return seen + jnp.sum(eqf, axis=0, keepdims=True)

        lax.fori_loop(0, home + 1, tie_body, jnp.zeros((1, tq), F32))

    m_scr[...] = jnp.full(m_scr.shape, NEG_INF, F32)
    l_scr[...] = jnp.zeros(l_scr.shape, F32)
    acc_scr[...] = jnp.zeros(acc_scr.shape, F32)

    def unit(kt, g, near):
        def issue():
            ks = k_ref[0, pl.ds(kt * KEY_TILE, KEY_TILE), (g // 2) * LANES:(g // 2 + 1) * LANES]
            return _mm(ks, qg_scr[g])

        def consume(st):
            mb = mb_scr[kt]
            vt = vt_ref[0, kt, g * HEAD_DIM:(g + 1) * HEAD_DIM, :]
            for r in range(G_B):
                hq = g * G_B + r
                s = st[:, r * tq:(r + 1) * tq] + mb
                if near is not None:
                    s = s + bias_ref[hq, near]
                m_old = m_scr[hq]
                m_new = jnp.maximum(m_old, jnp.max(s, axis=0, keepdims=True))
                m_use = jnp.where(m_new == NEG_INF, 0.0, m_new)
                alpha = jnp.exp2(m_old - m_use)
                p = jnp.exp2(s - m_use)
                l_scr[hq] = alpha * l_scr[hq] + jnp.sum(p, axis=0, keepdims=True)
                m_scr[hq] = m_new
                acc_scr[hq] = acc_scr[hq] * alpha + _mm(vt, p.astype(BF16))

        return issue, consume

    def attend(tiles):
        _lookahead([unit(kt, g, near) for kt, near in tiles for g in range(HKV_B)], 2)

    far_tpi = 4

    def far_body(kp, carry):
        attend([(far_tpi * kp + t, None) for t in range(far_tpi)])
        return carry

    n_far = jnp.maximum(home - 1, 0)
    trips = n_far // far_tpi
    lax.fori_loop(0, trips, far_body, 0)
    for r in range(1, far_tpi):
        @pl.when(n_far - trips * far_tpi == r)
        def _():
            attend([(trips * far_tpi + t, None) for t in range(r)])

    @pl.when(home >= 1)
    def _():
        attend([(home - 1, 0), (home, 1)])

    @pl.when(home == 0)
    def _():
        attend([(home, 1)])

    for hq in range(H_B):
        o_ref[0, hq * HEAD_DIM:(hq + 1) * HEAD_DIM, :] = (acc_scr[hq] / l_scr[hq]).astype(BF16)


def _dsa_attention(qt, qit, wit, k, vt, ki, bias, *, q_off, k_top, tk_valid):
    b, d, t_q = qt.shape
    t_k = k.shape[1]
    nkt = t_k // KEY_TILE
    tq = KEY_TILE
    kern = functools.partial(_dsa_kernel, q_off=q_off, k_top=k_top, tk_valid=tk_valid)
    qblk = lambda n: pl.BlockSpec((1, n, tq), lambda bb, i: (bb, 0, i))
    kblk = lambda n: pl.BlockSpec((1, t_k, n), lambda bb, i: (bb, 0, 0))
    return pl.pallas_call(
        kern,
        grid=(b, t_q // tq),
        in_specs=[qblk(d), qblk(qit.shape[1]), qblk(H_IDX), kblk(k.shape[2]),
                  pl.BlockSpec((1, nkt, vt.shape[2], KEY_TILE), lambda bb, i: (bb, 0, 0, 0)),
                  kblk(ki.shape[2]), _WHOLE],
        out_specs=qblk(d),
        out_shape=jax.ShapeDtypeStruct((b, d, t_q), BF16),
        scratch_shapes=[pltpu.VMEM((nkt + 1, KEY_TILE, tq), I32), pltpu.VMEM((nkt, KEY_TILE, tq), F32),
                        pltpu.VMEM((H_IDX, LANES, tq), BF16),
                        pltpu.VMEM((HKV_B, LANES, G_B * tq), BF16),
                        pltpu.VMEM((H_B, 1, tq), F32), pltpu.VMEM((H_B, 1, tq), F32),
                        pltpu.VMEM((H_B, HEAD_DIM, tq), F32)],
        compiler_params=_cp("parallel", "arbitrary"),
    )(qt, qit, wit, k, vt, ki, bias)


def _conv3(u, w_ref, cols, prev):
    tm = u.shape[0]
    row = lax.broadcasted_iota(I32, (tm, 1), 0)
    r1 = pltpu.roll(u, 1, 0)
    r2 = pltpu.roll(u, 2, 0)
    if prev[0] == "stream":
        carry = prev[1]
        c6, c7 = carry[6:7, :], carry[7:8, :]
        um1 = jnp.where(row == 0, c7, r1)
        um2 = jnp.where(row == 0, c6, jnp.where(row == 1, c7, r2))
    else:
        _, seq, pm1, pm2 = prev
        t = row % seq
        um1 = jnp.where(t == 0, pm1, r1)
        um2 = jnp.where(t < 2, pm2, r2)
    return w_ref[0:1, cols] * um2 + w_ref[1:2, cols] * um1 + w_ref[2:3, cols] * u


def _col_chunks(n, width):
    out, c = [], 0
    while c < n:
        out.append((c, min(width, n - c)))
        c += width
    return out


def _ffn_kernel(*refs, mode, seq, tpb, dff, cw):
    if mode == "stream":
        x_ref, g_ref, wup_ref, wc_ref, bc_ref, wdn_ref, o_ref, tail_ref, carry_scr = refs
    else:
        x_ref, g_ref, wup_ref, wc_ref, bc_ref, wdn_ref, pm1_ref, pm2_ref, o_ref, up_ref = refs
    x = x_ref[...]
    tm = x.shape[0]
    h = _rms(x, g_ref[...]).astype(BF16)
    if mode == "stream":
        @pl.when(pl.program_id(0) % tpb == 0)
        def _():
            carry_scr[...] = jnp.zeros(carry_scr.shape, F32)
    acc = [x]

    def chunk(c0, w):
        def issue():
            return [_mm(h, wup_ref[:, base + c0:base + c0 + w]) for base in (0, dff)]

        def consume(ups):
            ys = []
            for base, up in zip((0, dff), ups):
                cols = slice(base + c0, base + c0 + w)
                if mode == "stream":
                    y = _conv3(up, wc_ref, cols, ("stream", carry_scr[:, cols]))
                    carry_scr[:, cols] = up[tm - SUBLANES:tm, :]
                    tail_ref[0, :, cols] = up[tm - SUBLANES:tm, :]
                else:
                    y = _conv3(up, wc_ref, cols, ("seq", seq, pm1_ref[:, cols], pm2_ref[:, cols]))
                    up_ref[:, cols] = up
                ys.append(y + bc_ref[:, cols])
            gate, val = ys
            act = (gate / (1.0 + jnp.exp(-gate))) * val
            acc[0] = acc[0] + _mm(act.astype(BF16), wdn_ref[c0:c0 + w, :])

        return issue, consume

    _lookahead([chunk(c0, w) for c0, w in _col_chunks(dff, cw)], 1)
    o_ref[...] = acc[0]


def _ffn_stream(x, g, w_up, w_conv, b_conv, w_down, *, tm, t_len):
    m, d = x.shape
    c2 = w_up.shape[1]
    tpb = t_len // tm
    kern = functools.partial(_ffn_kernel, mode="stream", seq=None, tpb=tpb, dff=c2 // 2, cw=256)
    return pl.pallas_call(
        kern,
        grid=(m // tm,),
        in_specs=[_rows(tm, d)] + [_WHOLE] * 5,
        out_specs=[_rows(tm, d), pl.BlockSpec((1, SUBLANES, c2), lambda i: (i // tpb, 0, 0))],
        out_shape=[jax.ShapeDtypeStruct((m, d), F32),
                   jax.ShapeDtypeStruct((m // t_len, SUBLANES, c2), F32)],
        scratch_shapes=[pltpu.VMEM((SUBLANES, c2), F32)],
        compiler_params=_cp("arbitrary"),
    )(x, g, w_up, w_conv, b_conv, w_down)


def _ffn_seq(x, g, w_up, w_conv, b_conv, w_down, pm1, pm2, *, tm, seq):
    m, d = x.shape
    c2 = w_up.shape[1]
    kern = functools.partial(_ffn_kernel, mode="seq", seq=seq, tpb=None, dff=c2 // 2, cw=512)
    return pl.pallas_call(
        kern,
        grid=(m // tm,),
        in_specs=[_rows(tm, d)] + [_WHOLE] * 5 + [_rows(tm, c2), _rows(tm, c2)],
        out_specs=[_rows(tm, d), _rows(tm, c2)],
        out_shape=[jax.ShapeDtypeStruct((m, d), F32), jax.ShapeDtypeStruct((m, c2), F32)],
        compiler_params=_cp("parallel"),
    )(x, g, w_up, w_conv, b_conv, w_down, pm1, pm2)


def _sconv_kernel(*refs, mode, seq, tpb, cw):
    if mode == "stream":
        x_ref, g_ref, win_ref, wc_ref, wout_ref, o_ref, tail_ref, carry_scr = refs
    else:
        x_ref, g_ref, win_ref, wc_ref, wout_ref, pm1_ref, pm2_ref, o_ref, p_ref = refs
    x = x_ref[...]
    tm, d = x.shape
    h = _rms(x, g_ref[...]).astype(BF16)
    if mode == "stream":
        @pl.when(pl.program_id(0) % tpb == 0)
        def _():
            carry_scr[...] = jnp.zeros(carry_scr.shape, F32)
    acc = x
    for c0, w in _col_chunks(d, cw):
        cols = slice(c0, c0 + w)
        gb = _mm(h, win_ref[:, c0:c0 + w])
        gc = _mm(h, win_ref[:, d + c0:d + c0 + w])
        u = _mm(h, win_ref[:, 2 * d + c0:2 * d + c0 + w])
        p = gc * u
        if mode == "stream":
            y = _conv3(p, wc_ref, cols, ("stream", carry_scr[:, cols]))
            carry_scr[:, cols] = p[tm - SUBLANES:tm, :]
            tail_ref[0, :, cols] = p[tm - SUBLANES:tm, :]
        else:
            y = _conv3(p, wc_ref, cols, ("seq", seq, pm1_ref[:, cols], pm2_ref[:, cols]))
            p_ref[:, cols] = p
        acc = acc + _mm((gb * y).astype(BF16), wout_ref[c0:c0 + w, :])
    o_ref[...] = acc


def _sconv_stream(x, g, w_in, w_conv, w_out, *, tm, t_len):
    m, d = x.shape
    tpb = t_len // tm
    kern = functools.partial(_sconv_kernel, mode="stream", seq=None, tpb=tpb, cw=512)
    return pl.pallas_call(
        kern,
        grid=(m // tm,),
        in_specs=[_rows(tm, d)] + [_WHOLE] * 4,
        out_specs=[_rows(tm, d), pl.BlockSpec((1, SUBLANES, d), lambda i: (i // tpb, 0, 0))],
        out_shape=[jax.ShapeDtypeStruct((m, d), F32),
                   jax.ShapeDtypeStruct((m // t_len, SUBLANES, d), F32)],
        scratch_shapes=[pltpu.VMEM((SUBLANES, d), F32)],
        compiler_params=_cp("arbitrary"),
    )(x, g, w_in, w_conv, w_out)


def _sconv_seq(x, g, w_in, w_conv, w_out, pm1, pm2, *, tm, seq):
    m, d = x.shape
    kern = functools.partial(_sconv_kernel, mode="seq", seq=seq, tpb=None, cw=512)
    return pl.pallas_call(
        kern,
        grid=(m // tm,),
        in_specs=[_rows(tm, d)] + [_WHOLE] * 4 + [_rows(tm, d), _rows(tm, d)],
        out_specs=[_rows(tm, d), _rows(tm, d)],
        out_shape=[jax.ShapeDtypeStruct((m, d), F32), jax.ShapeDtypeStruct((m, d), F32)],
        compiler_params=_cp("parallel"),
    )(x, g, w_in, w_conv, w_out, pm1, pm2)


def _post_kernel(*refs, nb, has_mix):
    if has_mix:
        x_ref, a_ref, wmix_ref, g_ref, wq_ref, mk_ref, mv_ref, wo_ref, o_ref, oc_scr = refs
        x = x_ref[...] + _mm(a_ref[...], wmix_ref[...])
    else:
        x_ref, g_ref, wq_ref, mk_ref, mv_ref, wo_ref, o_ref, oc_scr = refs
        x = x_ref[...]
    tm, d = x.shape
    dh = d // XA_HEADS
    rpb = tm // nb
    h = _rms(x, g_ref[...]).astype(BF16)
    q = (_mm(h, wq_ref[...]) * (dh ** -0.5)).astype(BF16)
    for b in range(nb):
        rows = slice(b * rpb, (b + 1) * rpb)
        for hh in range(XA_HEADS):
            cols = slice(hh * dh, (hh + 1) * dh)
            s = _mm_nt(q[rows, cols], mk_ref[b, :, cols].astype(BF16))
            e = jnp.exp(s - jnp.max(s, axis=1, keepdims=True))
            p = e / jnp.sum(e, axis=1, keepdims=True)
            oc_scr[rows, cols] = _mm(p.astype(BF16), mv_ref[b, :, cols].astype(BF16)).astype(BF16)
    o_ref[...] = x + _mm(oc_scr[...], wo_ref[...])


def _post(x, a, w_mix, g, w_q, mk, mv, w_o, *, tm, nb, tiles_per_mem):
    m, d = x.shape
    n_mem = mk.shape[1]
    has_mix = a is not None
    mem_spec = pl.BlockSpec((nb, n_mem, d), lambda i: (i // tiles_per_mem, 0, 0))
    ins = [x] + ([a, w_mix] if has_mix else []) + [g, w_q, mk, mv, w_o]
    specs = ([_rows(tm, d)] + ([_rows(tm, d), _WHOLE] if has_mix else [])
             + [_WHOLE, _WHOLE, mem_spec, mem_spec, _WHOLE])
    return pl.pallas_call(
        functools.partial(_post_kernel, nb=nb, has_mix=has_mix),
        grid=(m // tm,),
        in_specs=specs,
        out_specs=_rows(tm, d),
        out_shape=jax.ShapeDtypeStruct((m, d), F32),
        scratch_shapes=[pltpu.VMEM((tm, d), BF16)],
        compiler_params=_cp("parallel"),
    )(*ins)


def _norm_kernel(x_ref, g_ref, o_ref):
    o_ref[...] = _rms(x_ref[...], g_ref[...])


def _final_norm(x, g, tm):
    m, d = x.shape
    return pl.pallas_call(
        _norm_kernel, grid=(m // tm,), in_specs=[_rows(tm, d), _WHOLE], out_specs=_rows(tm, d),
        out_shape=jax.ShapeDtypeStruct((m, d), F32), compiler_params=_cp("parallel"),
    )(x, g)


def _t5_bucket(rel):
    half = NUM_BUCKETS // 2
    ret = jnp.where(rel > 0, half, 0)
    n = jnp.abs(rel)
    max_exact = half // 2
    nf = jnp.maximum(n, 1).astype(jnp.float32)
    large = max_exact + (jnp.log(nf / max_exact) / math.log(MAX_DISTANCE / max_exact)
                         * (half - max_exact)).astype(jnp.int32)
    large = jnp.minimum(large, half - 1)
    return ret + jnp.where(n < max_exact, n, large)


def _near_bias(rel_bias):
    sl = jnp.arange(KEY_TILE, dtype=jnp.int32)[None, :, None]
    ql = jnp.arange(KEY_TILE, dtype=jnp.int32)[None, None, :]
    off = jnp.array([-KEY_TILE, 0], jnp.int32)[:, None, None]
    rel = sl + off - ql
    far = _t5_bucket(jnp.array(-2 * KEY_TILE, jnp.int32))
    tab = (rel_bias[_t5_bucket(rel)] - rel_bias[far]) * LOG2E
    return jnp.transpose(tab, (3, 0, 1, 2)).astype(F32)


def _seq_context(prev, seq):
    b, _, c = prev.shape
    z = jnp.zeros((b, seq, c), prev.dtype)
    pm1 = z.at[:, 0].set(prev[:, 1])
    pm2 = z.at[:, 0].set(prev[:, 0]).at[:, 1].set(prev[:, 1])
    return pm1.reshape(b * seq, c), pm2.reshape(b * seq, c)


def _pad_keys(cache, new, t_pad):
    b, p, c = cache.shape
    t = new.shape[1]
    pad = jnp.zeros((b, t_pad - p - t, c), BF16)
    return jnp.concatenate([cache.astype(BF16), new.astype(BF16), pad], axis=1)


def _forget_slot_maps(d):
    s_q = np.zeros((3 * LANES, 2 * d), np.float32)
    s_k = np.zeros((3 * LANES, 2 * d), np.float32)
    c_q = np.zeros((1, 2 * d), np.float32)
    c_k = np.zeros((1, 2 * d), np.float32)
    for h in range(H_A):
        base = h * 2 * HEAD_DIM + HEAD_DIM
        for p in range(3):
            s_q[p * LANES + h, base + 3 + p] = 1.0
            s_k[p * LANES + h, base + p] = -1.0
            c_q[0, base + p] = 1.0
            c_k[0, base + 3 + p] = 1.0
    return jnp.asarray(s_q, BF16), jnp.asarray(s_k, BF16), jnp.asarray(c_q), jnp.asarray(c_k)


def _split3(x):
    def top(v):
        bits = lax.bitcast_convert_type(v, jnp.uint32) & jnp.uint32(0xFFFF0000)
        return lax.bitcast_convert_type(bits, F32)

    hi = top(x)
    mid = top(x - hi)
    lo = x - hi - mid
    return [hi.astype(BF16), mid.astype(BF16), lo.astype(BF16)]


def _feature_major(a, t_pad):
    a = jnp.swapaxes(a, 1, 2)
    return jnp.pad(a, ((0, 0), (0, 0), (0, t_pad - a.shape[2])))


def _key_tiles_feature_major(a, tk):
    b, t, c = a.shape
    return jnp.swapaxes(a.reshape(b, t // tk, tk, c), 2, 3)


def kernel(x_prompt, x_sample, mem_prompt, cache_k_A, cache_v_A, cache_logf_A, cache_k_B, cache_v_B,
           cache_kidx_B, state_conv_C, state_ffconv, cache_mem_k, cache_mem_v, g_mix, g_xa, g_ffn,
           g_mem, g_final, w_in_A, b_f_A, w_out_A, w_in_B, w_out_B, rel_bias, w_in_C, w_conv_C,
           w_out_C, w_q_xa, w_k_xa, w_v_xa, w_o_xa, w_up, w_conv_ff, b_conv_ff, w_down):
    bp, tp, d = x_prompt.shape
    bs, ts, _ = x_sample.shape
    depth = g_mix.shape[0]
    past = cache_k_A.shape[2]
    n_mem = mem_prompt.shape[1]
    assert d == H_A * HEAD_DIM == H_B * HEAD_DIM and ts % SUBLANES == 0 and ts <= KEY_TILE
    assert past % KEY_TILE == 0 and tp % KEY_TILE == 0
    mixers = tuple("ABC"[l % 3] for l in range(depth))
    slot = tuple(mixers[:l].count(mixers[l]) for l in range(depth))
    mp, ms = bp * tp, bs * ts
    tm_p = min(256, tp)
    tm_s = min(ms, 8 * ts)
    tq_a = min(512, tp)
    tk_a = min(256, tp)
    t_all = past + ts
    t_pad = -(-t_all // KEY_TILE) * KEY_TILE
    k_top_p = min(TOPK_MAX, tp // 4)
    k_top_s = min(TOPK_MAX, t_all // 4)
    nb_s = min(4, bs)

    bf = lambda a: a.astype(BF16)
    row = lambda a: a.reshape(1, -1)
    xp = x_prompt.reshape(mp, d)
    xs = x_sample.reshape(ms, d)
    mem = mem_prompt.reshape(bp * n_mem, d)
    bias_near = _near_bias(rel_bias)

    outs = {n: [] for n in ("kA_p", "vA_p", "fA_p", "kB_p", "vB_p", "iB_p", "cC_p", "ff_p", "mk_p",
                            "mv_p", "kA_s", "vA_s", "fA_s", "kB_s", "vB_s", "iB_s", "cC_s", "ff_s")}

    slot_q, slot_k, ones_q, ones_k = _forget_slot_maps(d)

    def fox_prompt(x, g, j):
        w = w_in_A[j]
        widen = lambda a: bf(jnp.pad(a.reshape(d, H_A, HEAD_DIM),
                                     ((0, 0), (0, 0), (0, HEAD_DIM))).reshape(d, 2 * d))
        w_f = jnp.pad(w[:, 3 * d:], ((0, 0), (0, LANES - H_A)))
        b_f = jnp.pad(b_f_A[j], (0, LANES - H_A)).reshape(1, LANES)
        k, v, lf, qa, ka, vt = _a_in_seq(
            x, g, widen(w[:, :d]), widen(w[:, d:2 * d]), bf(w[:, 2 * d:3 * d]),
            bf(w[:, 2 * d:3 * d].T), bf(w_f), b_f, slot_q, slot_k, ones_q, ones_k,
            tm=tk_a, t_len=tp)
        o = _fox_attention(qa, ka, vt, b=bp, tq=tq_a, tk=tk_a, q_off=0)
        return o, k, v, lf

    def fox_sample(x, g, j, ck, cv, clf):
        b, t, tq_pad = bs, ts, KEY_TILE
        q, k, v, kb, vb, lf = _a_in(x, g, bf(w_in_A[j][:, :3 * d]), bf(w_in_A[j][:, 3 * d:]),
                                    row(b_f_A[j]), min(256, ms))
        lf_t = jnp.swapaxes(lf.reshape(b, t, H_A), 1, 2)
        lf_all = jnp.concatenate([jnp.swapaxes(clf, 1, 2).astype(F32), lf_t,
                                  jnp.zeros((b, H_A, t_pad - t_all), F32)], axis=2)
        k_all = _pad_keys(ck.reshape(b, past, d), kb.reshape(b, t, d), t_pad)
        v_all = _pad_keys(cv.reshape(b, past, d), vb.reshape(b, t, d), t_pad)
        f_all = _cumsum_lanes(lf_all.reshape(b * H_A, t_pad), min(64, b * H_A))
        fk3 = [jnp.swapaxes(p.reshape(b, H_A, t_pad), 1, 2) for p in _split3(f_all * LOG2E)]
        fq3 = [jnp.pad(p[:, past:past + t], ((0, 0), (0, tq_pad - t), (0, 0))) for p in fk3]
        one_k, one_q = jnp.ones((b, t_pad, H_A), BF16), jnp.ones((b, tq_pad, H_A), BF16)
        spare = ((0, 0), (0, 0), (0, 0), (0, HEAD_DIM - 6))
        k_slots = jnp.pad(jnp.stack([-p for p in fk3] + [one_k] * 3, axis=-1), spare)
        ka = jnp.concatenate([k_all.reshape(b, t_pad, H_A, HEAD_DIM), k_slots], axis=-1)
        q_slots = jnp.pad(jnp.stack([one_q] * 3 + fq3, axis=-1), spare)
        q_pad = jnp.pad(q.reshape(b, t, H_A, HEAD_DIM), ((0, 0), (0, tq_pad - t), (0, 0), (0, 0)))
        qa = jnp.concatenate([q_pad, q_slots], axis=-1)
        vt = _key_tiles_feature_major(v_all, KEY_TILE).reshape(-1, d, KEY_TILE)
        o = _fox_attention(qa.reshape(b * tq_pad, 2 * d), ka.reshape(b * t_pad, 2 * d), vt,
                           b=b, tq=KEY_TILE, tk=KEY_TILE, q_off=past)
        return o.reshape(b, tq_pad, d)[:, :t].reshape(ms, d), k, v, lf

    def dsa(x, g, j, b, t, cache):
        m = b * t
        w = w_in_B[j]
        dkv = HKV_B * HEAD_DIM
        c0, c2, c3 = d, d + 2 * dkv, d + 2 * dkv + H_IDX * D_IDX
        q, k, v, kb, vb, qi, ki, kib, wi = _b_in(
            x, g, bf(w[:, :c0]), bf(w[:, c0:c2]), bf(w[:, c2:c3]),
            bf(jnp.concatenate([w[:, c3:c3 + D_IDX]] * 2, axis=1)), bf(w[:, c3 + D_IDX:]), min(256, m))
        if cache is None:
            q_off, k_top, tk_valid, tq_pad = 0, k_top_p, t, t
            k_all, v_all, ki_all = kb.reshape(b, t, dkv), vb.reshape(b, t, dkv), kib.reshape(b, t, -1)
        else:
            ck, cv, cki = cache
            q_off, k_top, tk_valid, tq_pad = past, k_top_s, t_all, KEY_TILE
            k_all = _pad_keys(ck.reshape(b, past, dkv), kb.reshape(b, t, dkv), t_pad)
            v_all = _pad_keys(cv.reshape(b, past, dkv), vb.reshape(b, t, dkv), t_pad)
            cki2 = jnp.concatenate([cki, cki], axis=-1)
            ki_all = _pad_keys(cki2, kib.reshape(b, t, -1), t_pad)
        ot = _dsa_attention(_feature_major(q.reshape(b, t, d), tq_pad),
                            _feature_major(qi.reshape(b, t, -1), tq_pad),
                            _feature_major(wi.reshape(b, t, H_IDX), tq_pad),
                            k_all, _key_tiles_feature_major(v_all, KEY_TILE), ki_all, bias_near,
                            q_off=q_off, k_top=k_top, tk_valid=tk_valid)
        o = jnp.swapaxes(ot[:, :, :t], 1, 2)
        return o.reshape(m, d), k, v, ki

    for l in range(depth):
        mix, j = mixers[l], slot[l]
        gm, gx, gf = row(g_mix[l]), row(g_xa[l]), row(g_ffn[l])
        mk, mv = _proj(mem, row(g_mem[l]), [bf(w_k_xa[l]), bf(w_v_xa[l])], min(256, bp * n_mem))
        outs["mk_p"].append(mk.reshape(bp, n_mem, XA_HEADS, d // XA_HEADS))
        outs["mv_p"].append(mv.reshape(bp, n_mem, XA_HEADS, d // XA_HEADS))
        mks, mvs = cache_mem_k[l].reshape(bs, n_mem, d), cache_mem_v[l].reshape(bs, n_mem, d)
        post = dict(g=gx, w_q=bf(w_q_xa[l]), w_o=bf(w_o_xa[l]))

        if mix == "A":
            ap, k, v, f = fox_prompt(xp, gm, j)
            outs["kA_p"].append(k.reshape(bp, tp, H_A, HEAD_DIM))
            outs["vA_p"].append(v.reshape(bp, tp, H_A, HEAD_DIM))
            outs["fA_p"].append(f.reshape(bp, tp, H_A))
            a_s, k, v, f = fox_sample(xs, gm, j, cache_k_A[j], cache_v_A[j], cache_logf_A[j])
            outs["kA_s"].append(k.reshape(bs, ts, H_A, HEAD_DIM))
            outs["vA_s"].append(v.reshape(bs, ts, H_A, HEAD_DIM))
            outs["fA_s"].append(f.reshape(bs, ts, H_A))
            w_mix = bf(w_out_A[j])
        elif mix == "B":
            ap, k, v, ki = dsa(xp, gm, j, bp, tp, None)
            outs["kB_p"].append(k.reshape(bp, tp, HKV_B, HEAD_DIM))
            outs["vB_p"].append(v.reshape(bp, tp, HKV_B, HEAD_DIM))
            outs["iB_p"].append(ki.reshape(bp, tp, D_IDX))
            a_s, k, v, ki = dsa(xs, gm, j, bs, ts, (cache_k_B[j], cache_v_B[j], cache_kidx_B[j]))
            outs["kB_s"].append(k.reshape(bs, ts, HKV_B, HEAD_DIM))
            outs["vB_s"].append(v.reshape(bs, ts, HKV_B, HEAD_DIM))
            outs["iB_s"].append(ki.reshape(bs, ts, D_IDX))
            w_mix = bf(w_out_B[j])
        else:
            xp, tail = _sconv_stream(xp, gm, bf(w_in_C[j]), w_conv_C[j], bf(w_out_C[j]),
                                     tm=tm_p, t_len=tp)
            outs["cC_p"].append(tail[:, SUBLANES - 2:, :])
            pm1, pm2 = _seq_context(state_conv_C[j], ts)
            xs, p_all = _sconv_seq(xs, gm, bf(w_in_C[j]), w_conv_C[j], bf(w_out_C[j]), pm1, pm2,
                                   tm=tm_s, seq=ts)
            outs["cC_s"].append(p_all.reshape(bs, ts, d)[:, ts - 2:, :])
            ap = a_s = w_mix = None

        xp = _post(xp, ap, w_mix, mk=mk.reshape(bp, n_mem, d), mv=mv.reshape(bp, n_mem, d),
                   tm=tm_p, nb=1, tiles_per_mem=tp // tm_p, **post)
        xs = _post(xs, a_s, w_mix, mk=mks, mv=mvs, tm=nb_s * ts, nb=nb_s, tiles_per_mem=1, **post)

        wu, wc, bc, wd = bf(w_up[l]), w_conv_ff[l], row(b_conv_ff[l]), bf(w_down[l])
        xp, tail = _ffn_stream(xp, gf, wu, wc, bc, wd, tm=tm_p, t_len=tp)
        outs["ff_p"].append(tail[:, SUBLANES - 2:, :])
        pm1, pm2 = _seq_context(state_ffconv[l], ts)
        xs, up_all = _ffn_seq(xs, gf, wu, wc, bc, wd, pm1, pm2, tm=tm_s, seq=ts)
        outs["ff_s"].append(up_all.reshape(bs, ts, -1)[:, ts - 2:, :])

    y_p = _final_norm(xp, row(g_final), tm_p).reshape(bp, tp, d)
    y_s = _final_norm(xs, row(g_final), min(256, ms)).reshape(bs, ts, d)
    st = lambda n: jnp.stack(outs[n])
    return (y_p, y_s, st("kA_p"), st("vA_p"), st("fA_p"), st("kB_p"), st("vB_p"), st("iB_p"),
            st("cC_p"), st("ff_p"), st("mk_p"), st("mv_p"), st("kA_s"), st("vA_s"), st("fA_s"),
            st("kB_s"), st("vB_s"), st("iB_s"), st("cC_s"), st("ff_s"))
```

```python
import functools
import math

import jax
import jax.numpy as jnp
import numpy as np
from jax import lax
from jax.experimental import pallas as pl
from jax.experimental.pallas import tpu as pltpu

F32, BF16, I32 = jnp.float32, jnp.bfloat16, jnp.int32

CHUNK = 64
H_A = 16
H_B = 16
HKV_B = 4
G_B = H_B // HKV_B
H_IDX = 8
D_IDX = 64
TOPK_MAX = 256
NUM_BUCKETS = 32
MAX_DISTANCE = 128
XA_HEADS = 4
EPS = 1e-6

LANES = 128
SUBLANES = 8
HEAD_DIM = 64
KEY_TILE = 128
VMEM_LIMIT_BYTES = 56 * 1024 * 1024

INT_MIN = -2 ** 31
NEG_INF = float("-inf")
LOG2E = 1.0 / math.log(2.0)
Q_SCALE = HEAD_DIM ** -0.5 * LOG2E

_WHOLE = pl.BlockSpec(memory_space=pltpu.VMEM)


def _cp(*sem):
    return pltpu.CompilerParams(dimension_semantics=sem, vmem_limit_bytes=VMEM_LIMIT_BYTES)


def _rows(tm, n):
    return pl.BlockSpec((tm, n), lambda i: (i, 0))


def _rms(x, g):
    return x * lax.rsqrt(jnp.mean(x * x, axis=-1, keepdims=True) + EPS) * g


def _mm(a, b):
    return jnp.dot(a, b, preferred_element_type=F32)


def _mm_nt(a, b):
    return lax.dot_general(a, b, (((1,), (1,)), ((), ())), preferred_element_type=F32)


def _proj_kernel(x_ref, g_ref, *refs, n_w):
    w_refs, o_refs = refs[:n_w], refs[n_w:]
    h = _rms(x_ref[...], g_ref[...]).astype(BF16)
    for w_ref, o_ref in zip(w_refs, o_refs):
        o_ref[...] = _mm(h, w_ref[...])


def _proj(x, g, ws, tm):
    m, d = x.shape
    return pl.pallas_call(
        functools.partial(_proj_kernel, n_w=len(ws)),
        grid=(m // tm,),
        in_specs=[_rows(tm, d), _WHOLE] + [_WHOLE] * len(ws),
        out_specs=[_rows(tm, w.shape[1]) for w in ws],
        out_shape=[jax.ShapeDtypeStruct((m, w.shape[1]), F32) for w in ws],
        compiler_params=_cp("parallel"),
    )(x, g, *ws)


def _a_in_kernel(x_ref, g_ref, w_ref, wf_ref, bf_ref, q_ref, k_ref, v_ref, kb_ref, vb_ref, lf_ref):
    d = q_ref.shape[-1]
    h = _rms(x_ref[...], g_ref[...]).astype(BF16)
    q_ref[...] = (_mm(h, w_ref[:, 0:d]) * Q_SCALE).astype(BF16)
    k = _mm(h, w_ref[:, d:2 * d])
    k_ref[...] = k
    kb_ref[...] = k.astype(BF16)
    v = _mm(h, w_ref[:, 2 * d:3 * d])
    v_ref[...] = v
    vb_ref[...] = v.astype(BF16)
    fl = _mm(h, wf_ref[...]) + bf_ref[...]
    lf_ref[...] = jnp.minimum(fl, 0.0) - jnp.log1p(jnp.exp(-jnp.abs(fl)))


def _a_in(x, g, w_qkv, w_f, b_f, tm):
    m, d = x.shape
    sds = jax.ShapeDtypeStruct
    return pl.pallas_call(
        _a_in_kernel,
        grid=(m // tm,),
        in_specs=[_rows(tm, d), _WHOLE, _WHOLE, _WHOLE, _WHOLE],
        out_specs=[_rows(tm, d)] * 5 + [_rows(tm, H_A)],
        out_shape=[sds((m, d), BF16), sds((m, d), F32), sds((m, d), F32),
                   sds((m, d), BF16), sds((m, d), BF16), sds((m, H_A), F32)],
        compiler_params=_cp("parallel"),
    )(x, g, w_qkv, w_f, b_f)


def _top_bits(v):
    bits = lax.bitcast_convert_type(v, I32) & jnp.int32(-65536)
    return lax.bitcast_convert_type(bits, F32)


def _a_in_seq_kernel(x_ref, g_ref, wqx_ref, wkx_ref, wv_ref, wvt_ref, wf_ref, bf_ref, sq_ref, sk_ref,
                     cq_ref, ck_ref, k_ref, v_ref, lf_ref, qa_ref, ka_ref, vt_ref, carry_scr, *, tpb):
    tm = x_ref.shape[0]
    h = _rms(x_ref[...], g_ref[...]).astype(BF16)
    lane = lax.broadcasted_iota(I32, (tm, LANES), 1)

    @pl.when(pl.program_id(0) % tpb == 0)
    def _():
        carry_scr[...] = jnp.zeros(carry_scr.shape, F32)

    fl = _mm(h, wf_ref[...]) + bf_ref[...]
    lf = jnp.minimum(fl, 0.0) - jnp.log1p(jnp.exp(-jnp.abs(fl)))
    lf = jnp.where(lane < H_A, lf, 0.0)
    lf_ref[...] = lf[:, 0:H_A]
    ra = lax.broadcasted_iota(I32, (tm, tm), 0)
    rb = lax.broadcasted_iota(I32, (tm, tm), 1)
    tri = (rb <= ra).astype(F32)
    f_cum = jnp.dot(tri, lf, precision=lax.Precision.HIGHEST,
                    preferred_element_type=F32) + carry_scr[0:1, :]
    carry_scr[...] = jnp.broadcast_to(f_cum[tm - 1:tm, :], carry_scr.shape)
    f2 = f_cum * LOG2E
    hi = _top_bits(f2)
    mid = _top_bits(f2 - hi)
    lo = f2 - hi - mid
    pieces = jnp.concatenate([hi, mid, lo], axis=1).astype(BF16)

    qa_ref[...] = (_mm(h, wqx_ref[...]) * Q_SCALE + _mm(pieces, sq_ref[...])
                   + cq_ref[...]).astype(BF16)
    kx = _mm(h, wkx_ref[...])
    ka_ref[...] = (kx + _mm(pieces, sk_ref[...]) + ck_ref[...]).astype(BF16)
    for p in range(k_ref.shape[1] // LANES):
        a = kx[:, (2 * p) * LANES:(2 * p + 1) * LANES]
        b = pltpu.roll(kx[:, (2 * p + 1) * LANES:(2 * p + 2) * LANES], HEAD_DIM, 1)
        k_ref[:, p * LANES:(p + 1) * LANES] = jnp.where(lane < HEAD_DIM, a, b)
    v_ref[...] = _mm(h, wv_ref[...])
    vt_ref[0] = _mm_nt(wvt_ref[...], h).astype(BF16)


def _a_in_seq(x, g, w_qx, w_kx, w_v, w_vt, w_f, b_f, s_q, s_k, c_q, c_k, *, tm, t_len):
    m, d = x.shape
    sds = jax.ShapeDtypeStruct
    kern = functools.partial(_a_in_seq_kernel, tpb=t_len // tm)
    return pl.pallas_call(
        kern,
        grid=(m // tm,),
        in_specs=[_rows(tm, d)] + [_WHOLE] * 11,
        out_specs=[_rows(tm, d), _rows(tm, d), _rows(tm, H_A), _rows(tm, 2 * d), _rows(tm, 2 * d),
                   pl.BlockSpec((1, d, tm), lambda i: (i, 0, 0))],
        out_shape=[sds((m, d), F32), sds((m, d), F32), sds((m, H_A), F32),
                   sds((m, 2 * d), BF16), sds((m, 2 * d), BF16), sds((m // tm, d, tm), BF16)],
        scratch_shapes=[pltpu.VMEM((SUBLANES, LANES), F32)],
        compiler_params=_cp("arbitrary"),
    )(x, g, w_qx, w_kx, w_v, w_vt, w_f, b_f, s_q, s_k, c_q, c_k)


def _b_in_kernel(x_ref, g_ref, wq_ref, wkv_ref, wqi_ref, wki_ref, wwi_ref,
                 q_ref, k_ref, v_ref, kb_ref, vb_ref, qi_ref, ki_ref, kib_ref, wi_ref):
    dkv = k_ref.shape[-1]
    h = _rms(x_ref[...], g_ref[...]).astype(BF16)
    q_ref[...] = (_mm(h, wq_ref[...]) * Q_SCALE).astype(BF16)
    kv = _mm(h, wkv_ref[...])
    k_ref[...] = kv[:, 0:dkv]
    kb_ref[...] = kv[:, 0:dkv].astype(BF16)
    v_ref[...] = kv[:, dkv:2 * dkv]
    vb_ref[...] = kv[:, dkv:2 * dkv].astype(BF16)
    qi_ref[...] = (_mm(h, wqi_ref[...]) * (D_IDX ** -0.5)).astype(BF16)
    ki2 = _mm(h, wki_ref[...])
    ki_ref[...] = ki2[:, 0:D_IDX]
    kib_ref[...] = ki2.astype(BF16)
    wi_ref[...] = _mm(h, wwi_ref[...]) * (H_IDX ** -0.5)


def _b_in(x, g, w_q, w_kv, w_qi, w_ki2, w_wi, tm):
    m, d = x.shape
    dkv = w_kv.shape[1] // 2
    dqi = w_qi.shape[1]
    sds = jax.ShapeDtypeStruct
    return pl.pallas_call(
        _b_in_kernel,
        grid=(m // tm,),
        in_specs=[_rows(tm, d)] + [_WHOLE] * 6,
        out_specs=[_rows(tm, d), _rows(tm, dkv), _rows(tm, dkv), _rows(tm, dkv), _rows(tm, dkv),
                   _rows(tm, dqi), _rows(tm, D_IDX), _rows(tm, 2 * D_IDX), _rows(tm, H_IDX)],
        out_shape=[sds((m, d), BF16), sds((m, dkv), F32), sds((m, dkv), F32),
                   sds((m, dkv), BF16), sds((m, dkv), BF16), sds((m, dqi), BF16),
                   sds((m, D_IDX), F32), sds((m, 2 * D_IDX), BF16), sds((m, H_IDX), F32)],
        compiler_params=_cp("parallel"),
    )(x, g, w_q, w_kv, w_qi, w_ki2, w_wi)


def _cumsum_kernel(x_ref, o_ref):
    n = x_ref.shape[-1]
    a = lax.broadcasted_iota(I32, (LANES, LANES), 0)
    b = lax.broadcasted_iota(I32, (LANES, LANES), 1)
    tri = (a <= b).astype(F32)
    carry = jnp.zeros((x_ref.shape[0], 1), F32)
    for c in range(n // LANES):
        sl = slice(c * LANES, (c + 1) * LANES)
        y = jnp.dot(x_ref[:, sl], tri, precision=lax.Precision.HIGHEST,
                    preferred_element_type=F32) + carry
        o_ref[:, sl] = y
        carry = y[:, LANES - 1:LANES]


def _cumsum_lanes(x, rb):
    r, n = x.shape
    return pl.pallas_call(
        _cumsum_kernel,
        grid=(r // rb,),
        in_specs=[_rows(rb, n)],
        out_specs=_rows(rb, n),
        out_shape=jax.ShapeDtypeStruct((r, n), F32),
        compiler_params=_cp("parallel"),
    )(x)


def _lookahead(units, depth):
    pending = []
    for idx in range(len(units) + depth):
        if idx < len(units):
            pending.append(units[idx][0]())
        if idx >= depth:
            units[idx - depth][1](pending[idx - depth])


def _fox_kernel(qa_ref, ka_ref, vt_ref, o_ref, m_scr, l_scr, acc_scr, *, tq, tk, tu, tpi, q_off):
    i = pl.program_id(2)
    m_scr[...] = jnp.full(m_scr.shape, NEG_INF, F32)
    l_scr[...] = jnp.zeros(l_scr.shape, F32)
    acc_scr[...] = jnp.zeros(acc_scr.shape, F32)
    q0 = q_off + i * tq

    def unit(kk, e, c, masked):
        cols = slice(c * tu, (c + 1) * tu)
        slab = slice(e * LANES, (e + 1) * LANES)

        def issue():
            return _mm_nt(ka_ref[pl.ds(kk * tk, tk), slab], qa_ref[cols, slab])

        def consume(s):
            if masked:
                kpos = kk * tk + lax.broadcasted_iota(I32, (tk, 1), 0)
                qpos = q0 + c * tu + lax.broadcasted_iota(I32, (1, tu), 1)
                s = jnp.where(kpos <= qpos, s, NEG_INF)
            m_old = m_scr[e, :, cols]
            m_new = jnp.maximum(m_old, jnp.max(s, axis=0, keepdims=True))
            alpha = jnp.exp2(m_old - m_new)
            p = jnp.exp2(s - m_new)
            l_scr[e, :, cols] = alpha * l_scr[e, :, cols] + jnp.sum(p, axis=0, keepdims=True)
            m_scr[e, :, cols] = m_new
            ve = vt_ref[kk, e * HEAD_DIM:(e + 1) * HEAD_DIM, :]
            acc_scr[e, :, cols] = acc_scr[e, :, cols] * alpha + _mm(ve, p.astype(BF16))

        return issue, consume

    def full_body(kp, carry):
        _lookahead([unit(tpi * kp + t, e, c, False)
                    for t in range(tpi) for c in range(tq // tu) for e in range(2)], 2)
        return carry

    assert tpi % 2 == 0 and q_off % (2 * tk) == 0 and tq % tk == 0 and tq % tu == 0
    assert tq // tk == 1 or tq % (2 * tk) == 0
    n_full = q0 // tk
    trips = n_full // tpi
    lax.fori_loop(0, trips, full_body, 0)
    for r in range(2, tpi, 2):
        @pl.when(n_full - trips * tpi == r)
        def _():
            _lookahead([unit(trips * tpi + t, e, c, False)
                        for t in range(r) for c in range(tq // tu) for e in range(2)], 2)
    diag = []
    for t in range(tq // tk):
        for c in range(tq // tu):
            if t * tk > (c + 1) * tu - 1:
                continue
            masked = (t + 1) * tk - 1 > c * tu
            diag += [unit(n_full + t, e, c, masked) for e in range(2)]
    _lookahead(diag, 2)
    ot = jnp.concatenate([acc_scr[e] / l_scr[e] for e in range(2)], axis=0)
    o_ref[...] = ot.T.astype(BF16)


def _fox_attention(qa, ka, vt, *, b, tq, tk, q_off):
    d = qa.shape[1] // 2
    t_q, t_k = qa.shape[0] // b, ka.shape[0] // b
    nq, nkt = t_q // tq, t_k // tk
    tu = tq
    tpi = 4
    kern = functools.partial(_fox_kernel, tq=tq, tk=tk, tu=tu, tpi=tpi, q_off=q_off)
    return pl.pallas_call(
        kern,
        grid=(b, d // LANES, nq),
        in_specs=[pl.BlockSpec((tq, 2 * LANES), lambda bb, j, i: (bb * nq + i, j)),
                  pl.BlockSpec((t_k, 2 * LANES), lambda bb, j, i: (bb, j)),
                  pl.BlockSpec((nkt, LANES, tk), lambda bb, j, i: (bb, j, 0))],
        out_specs=pl.BlockSpec((tq, LANES), lambda bb, j, i: (bb * nq + i, j)),
        out_shape=jax.ShapeDtypeStruct((b * t_q, d), BF16),
        scratch_shapes=[pltpu.VMEM((2, 1, tq), F32), pltpu.VMEM((2, 1, tq), F32),
                        pltpu.VMEM((2, HEAD_DIM, tq), F32)],
        compiler_params=_cp("parallel", "parallel", "arbitrary"),
    )(qa, ka, vt)


def _fox_cache_kernel(ck_ref, cv_ref, kslot_ref, knew_ref, vnew_ref, qbd_ref, o_ref, s_scr, of_scr,
                      *, past, ts, kc):
    d = ck_ref.shape[2]
    nl = qbd_ref.shape[2]
    q_feat, q_slot = qbd_ref[0, 0:d, :], qbd_ref[0, d:d + LANES, :]
    for c in range(past // kc):
        rows = slice(c * kc, (c + 1) * kc)
        s_scr[rows, :] = (_mm(ck_ref[0, rows, :].astype(BF16), q_feat)
                          + _mm(kslot_ref[0, rows, :], q_slot))
    s_new = _mm(knew_ref[0], q_feat) + _mm(kslot_ref[0, past:past + LANES, :], q_slot)
    key_j = lax.broadcasted_iota(I32, (LANES, nl), 0)
    query = lax.broadcasted_iota(I32, (LANES, nl), 1) % ts
    s_scr[past:past + LANES, :] = jnp.where(key_j <= query, s_new, NEG_INF)
    s = s_scr[...]
    p = jnp.exp2(s - jnp.max(s, axis=0, keepdims=True))
    p = p / jnp.sum(p, axis=0, keepdims=True)
    pt = p.T.astype(BF16)
    of_scr[...] = _mm(pt[:, past:past + LANES], vnew_ref[0])
    for c in range(past // kc):
        rows = slice(c * kc, (c + 1) * kc)
        of_scr[...] += _mm(pt[:, rows], cv_ref[0, rows, :].astype(BF16))
    for h in range(d // HEAD_DIM):
        cols = slice(h * HEAD_DIM, (h + 1) * HEAD_DIM)
        o_ref[0, :, cols] = of_scr[h * ts:(h + 1) * ts, cols].astype(BF16)


def _fox_cache_attention(ck, cv, kslot, knew, vnew, qbd, *, ts):
    b, past, d = ck.shape
    nl = qbd.shape[2]
    blk = lambda *s: pl.BlockSpec((1,) + s, lambda i: (i, 0, 0))
    return pl.pallas_call(
        functools.partial(_fox_cache_kernel, past=past, ts=ts, kc=min(512, past)),
        grid=(b,),
        in_specs=[blk(past, d), blk(past, d), blk(past + LANES, LANES), blk(LANES, d), blk(LANES, d),
                  blk(d + LANES, nl)],
        out_specs=blk(ts, d),
        out_shape=jax.ShapeDtypeStruct((b, ts, d), BF16),
        scratch_shapes=[pltpu.VMEM((past + LANES, nl), F32), pltpu.VMEM((nl, d), F32)],
        compiler_params=_cp("parallel"),
    )(ck, cv, kslot, knew, vnew, qbd)


def _dsa_kernel(qt_ref, qit_ref, wit_ref, k_ref, vt_ref, ki_ref, bias_ref, o_ref,
                keys_scr, mb_scr, qim_scr, qg_scr, m_scr, l_scr, acc_scr,
                *, q_off, k_top, tk_valid, nkt):
    tq = KEY_TILE
    i = pl.program_id(1)
    q0 = q_off + i * tq
    home = q0 // KEY_TILE
    qpos = q0 + lax.broadcasted_iota(I32, (1, tq), 1)
    chunk_shift = int(math.log2(CHUNK))
    qchunk = lax.shift_right_logical(qpos, chunk_shift)
    row = lax.broadcasted_iota(I32, (LANES, tq), 0)
    kf = float(k_top)

    for h in range(H_IDX):
        pair = qit_ref[0, (h // 2) * LANES:(h // 2 + 1) * LANES, :]
        qim_scr[h] = jnp.where((row >= HEAD_DIM) == (h % 2 == 1), pair, jnp.zeros_like(pair))
    qg_scr[...] = jnp.zeros(qg_scr.shape, BF16)
    for g in range(HKV_B):
        r0 = (g % 2) * HEAD_DIM
        for r in range(G_B):
            hq = g * G_B + r
            qg_scr[g, r0:r0 + HEAD_DIM, r * tq:(r + 1) * tq] = (
                qt_ref[0, hq * HEAD_DIM:(hq + 1) * HEAD_DIM, :])

    def score_unit(kt):
        def issue():
            kit = ki_ref[0, pl.ds(jnp.minimum(kt, nkt - 1) * KEY_TILE, KEY_TILE), :]
            return [_mm(kit, qim_scr[h]) for h in range(H_IDX)]

        def consume(dots):
            acc = jnp.zeros((KEY_TILE, tq), F32)
            for h in range(H_IDX):
                acc = acc + jnp.maximum(dots[h], 0.0) * wit_ref[0, h:h + 1, :]
            acc = jnp.where(acc == 0.0, 0.0, acc)
            bits = lax.bitcast_convert_type(acc, I32)
            key = bits ^ (lax.shift_right_arithmetic(bits, 31) & 0x7FFFFFFF)
            kpos = kt * KEY_TILE + lax.broadcasted_iota(I32, (KEY_TILE, 1), 0)
            adm = (lax.shift_right_logical(kpos, chunk_shift) <= qchunk) & (kpos < tk_valid)
            keys_scr[kt] = jnp.where(adm, key, INT_MIN)

        return issue, consume

    def score_body(kp, carry):
        _lookahead([score_unit(2 * kp), score_unit(2 * kp + 1)], 1)
        return carry

    lax.fori_loop(0, (home + 2) // 2, score_body, 0)
    keys_scr[home + 1] = jnp.full((KEY_TILE, tq), INT_MIN, I32)

    def count(cand, strict):
        def body(kp, a):
            for t in range(2):
                key = keys_scr[2 * kp + t]
                hit = (key > cand) if strict else (key >= cand)
                a = jnp.where(hit, a + 1.0, a)
            return a

        a = lax.fori_loop(0, (home + 2) // 2, body, jnp.zeros((KEY_TILE, tq), F32))
        return jnp.sum(a, axis=0, keepdims=True)

    zero = jnp.zeros((1, tq), I32)
    thr = jnp.where(count(zero, False) >= kf, zero, jnp.full((1, tq), INT_MIN, I32))

    def bit_body(b, t):
        cand = t + lax.shift_left(jnp.int32(1), 30 - b)
        return jnp.where(count(cand, False) >= kf, cand, t)

    thr = lax.fori_loop(0, 31, bit_body, thr)

    no_ties = jnp.max(jnp.abs(count(thr, False) - kf)) == 0.0

    @pl.when(no_ties)
    def _():
        def body(kt, carry):
            key = keys_scr[kt]
            mb_scr[kt] = jnp.where((key >= thr) & (key != INT_MIN), 0.0, NEG_INF)
            return carry

        lax.fori_loop(0, home + 1, body, 0)

    @pl.when(jnp.logical_not(no_ties))
    def _():
        need = kf - count(thr, True)
        ra = lax.broadcasted_iota(I32, (KEY_TILE, KEY_TILE), 0)
        rb = lax.broadcasted_iota(I32, (KEY_TILE, KEY_TILE), 1)
        earlier = (rb < ra).astype(BF16)

        def tie_body(kt, seen):
            key = keys_scr[kt]
            eq = key == thr
            eqf = jnp.where(eq, 1.0, 0.0)
            rank = _mm(earlier, eqf.astype(BF16)) + seen
            sel = ((key > thr) | (eq & (rank < need))) & (key != INT_MIN)
            mb_scr[kt] = jnp.where(sel, 0.0, NEG_INF)
            return seen + jnp.sum(eqf, axis=0, keepdims=True)

        lax.fori_loop(0, home + 1, tie_body, jnp.zeros((1, tq), F32))

    m_scr[...] = jnp.full(m_scr.shape, NEG_INF, F32)
    l_scr[...] = jnp.zeros(l_scr.shape, F32)
    acc_scr[...] = jnp.zeros(acc_scr.shape, F32)

    def unit(kt, g, near):
        def issue():
            ks = k_ref[0, pl.ds(kt * KEY_TILE, KEY_TILE), (g // 2) * LANES:(g // 2 + 1) * LANES]
            return _mm(ks, qg_scr[g])

        def consume(st):
            mb = mb_scr[kt]
            vt = vt_ref[0, kt, g * HEAD_DIM:(g + 1) * HEAD_DIM, :]
            for r in range(G_B):
                hq = g * G_B + r
                s = st[:, r * tq:(r + 1) * tq] + mb
                if near is not None:
                    s = s + bias_ref[hq, near]
                m_old = m_scr[hq]
                m_new = jnp.maximum(m_old, jnp.max(s, axis=0, keepdims=True))
                m_use = jnp.where(m_new == NEG_INF, 0.0, m_new)
                alpha = jnp.exp2(m_old - m_use)
                p = jnp.exp2(s - m_use)
                l_scr[hq] = alpha * l_scr[hq] + jnp.sum(p, axis=0, keepdims=True)
                m_scr[hq] = m_new
                acc_scr[hq] = acc_scr[hq] * alpha + _mm(vt, p.astype(BF16))

        return issue, consume

    def attend(tiles):
        _lookahead([unit(kt, g, near) for kt, near in tiles for g in range(HKV_B)], 2)

    far_tpi = 4

    def far_body(kp, carry):
        attend([(far_tpi * kp + t, None) for t in range(far_tpi)])
        return carry

    n_far = jnp.maximum(home - 1, 0)
    trips = n_far // far_tpi
    lax.fori_loop(0, trips, far_body, 0)
    for r in range(1, far_tpi):
        @pl.when(n_far - trips * far_tpi == r)
        def _():
            attend([(trips * far_tpi + t, None) for t in range(r)])

    @pl.when(home >= 1)
    def _():
        attend([(home - 1, 0), (home, 1)])

    @pl.when(home == 0)
    def _():
        attend([(home, 1)])

    for hq in range(H_B):
        o_ref[0, hq * HEAD_DIM:(hq + 1) * HEAD_DIM, :] = (acc_scr[hq] / l_scr[hq]).astype(BF16)


def _dsa_attention(qt, qit, wit, k, vt, ki, bias, *, q_off, k_top, tk_valid):
    b, d, t_q = qt.shape
    t_k = k.shape[1]
    nkt = t_k // KEY_TILE
    tq = KEY_TILE
    kern = functools.partial(_dsa_kernel, q_off=q_off, k_top=k_top, tk_valid=tk_valid, nkt=nkt)
    qblk = lambda n: pl.BlockSpec((1, n, tq), lambda bb, i: (bb, 0, i))
    kblk = lambda n: pl.BlockSpec((1, t_k, n), lambda bb, i: (bb, 0, 0))
    return pl.pallas_call(
        kern,
        grid=(b, t_q // tq),
        in_specs=[qblk(d), qblk(qit.shape[1]), qblk(H_IDX), kblk(k.shape[2]),
                  pl.BlockSpec((1, nkt, vt.shape[2], KEY_TILE), lambda bb, i: (bb, 0, 0, 0)),
                  kblk(ki.shape[2]), _WHOLE],
        out_specs=qblk(d),
        out_shape=jax.ShapeDtypeStruct((b, d, t_q), BF16),
        scratch_shapes=[pltpu.VMEM((nkt + 1, KEY_TILE, tq), I32), pltpu.VMEM((nkt, KEY_TILE, tq), F32),
                        pltpu.VMEM((H_IDX, LANES, tq), BF16),
                        pltpu.VMEM((HKV_B, LANES, G_B * tq), BF16),
                        pltpu.VMEM((H_B, 1, tq), F32), pltpu.VMEM((H_B, 1, tq), F32),
                        pltpu.VMEM((H_B, HEAD_DIM, tq), F32)],
        compiler_params=_cp("parallel", "arbitrary"),
    )(qt, qit, wit, k, vt, ki, bias)


def _conv3(u, w_ref, cols, prev):
    tm = u.shape[0]
    row = lax.broadcasted_iota(I32, (tm, 1), 0)
    r1 = pltpu.roll(u, 1, 0)
    r2 = pltpu.roll(u, 2, 0)
    if prev[0] == "stream":
        carry = prev[1]
        c6, c7 = carry[6:7, :], carry[7:8, :]
        um1 = jnp.where(row == 0, c7, r1)
        um2 = jnp.where(row == 0, c6, jnp.where(row == 1, c7, r2))
    else:
        _, seq, pm1, pm2 = prev
        t = row % seq
        um1 = jnp.where(t == 0, pm1, r1)
        um2 = jnp.where(t < 2, pm2, r2)
    return w_ref[0:1, cols] * um2 + w_ref[1:2, cols] * um1 + w_ref[2:3, cols] * u


def _col_chunks(n, width):
    out, c = [], 0
    while c < n:
        out.append((c, min(width, n - c)))
        c += width
    return out


def _ffn_kernel(*refs, mode, seq, tpb, dff, cw):
    if mode == "stream":
        x_ref, g_ref, wup_ref, wc_ref, bc_ref, wdn_ref, o_ref, tail_ref, carry_scr = refs
    else:
        x_ref, g_ref, wup_ref, wc_ref, bc_ref, wdn_ref, pm1_ref, pm2_ref, o_ref, up_ref = refs
    x = x_ref[...]
    tm = x.shape[0]
    h = _rms(x, g_ref[...]).astype(BF16)
    if mode == "stream":
        @pl.when(pl.program_id(0) % tpb == 0)
        def _():
            carry_scr[...] = jnp.zeros(carry_scr.shape, F32)
    acc = [x]

    def chunk(c0, w):
        def issue():
            return [_mm(h, wup_ref[:, base + c0:base + c0 + w]) for base in (0, dff)]

        def consume(ups):
            ys = []
            for base, up in zip((0, dff), ups):
                cols = slice(base + c0, base + c0 + w)
                if mode == "stream":
                    y = _conv3(up, wc_ref, cols, ("stream", carry_scr[:, cols]))
                    carry_scr[:, cols] = up[tm - SUBLANES:tm, :]
                    tail_ref[0, :, cols] = up[tm - SUBLANES:tm, :]
                else:
                    y = _conv3(up, wc_ref, cols, ("seq", seq, pm1_ref[:, cols], pm2_ref[:, cols]))
                    up_ref[:, cols] = up
                ys.append(y + bc_ref[:, cols])
            gate, val = ys
            act = (gate / (1.0 + jnp.exp(-gate))) * val
            acc[0] = acc[0] + _mm(act.astype(BF16), wdn_ref[c0:c0 + w, :])

        return issue, consume

    _lookahead([chunk(c0, w) for c0, w in _col_chunks(dff, cw)], 2)
    o_ref[...] = acc[0]


def _ffn_stream(x, g, w_up, w_conv, b_conv, w_down, *, tm, t_len):
    m, d = x.shape
    c2 = w_up.shape[1]
    tpb = t_len // tm
    kern = functools.partial(_ffn_kernel, mode="stream", seq=None, tpb=tpb, dff=c2 // 2, cw=256)
    return pl.pallas_call(
        kern,
        grid=(m // tm,),
        in_specs=[_rows(tm, d)] + [_WHOLE] * 5,
        out_specs=[_rows(tm, d), pl.BlockSpec((1, SUBLANES, c2), lambda i: (i // tpb, 0, 0))],
        out_shape=[jax.ShapeDtypeStruct((m, d), F32),
                   jax.ShapeDtypeStruct((m // t_len, SUBLANES, c2), F32)],
        scratch_shapes=[pltpu.VMEM((SUBLANES, c2), F32)],
        compiler_params=_cp("arbitrary"),
    )(x, g, w_up, w_conv, b_conv, w_down)


def _ffn_seq(x, g, w_up, w_conv, b_conv, w_down, pm1, pm2, *, tm, seq):
    m, d = x.shape
    c2 = w_up.shape[1]
    kern = functools.partial(_ffn_kernel, mode="seq", seq=seq, tpb=None, dff=c2 // 2, cw=512)
    return pl.pallas_call(
        kern,
        grid=(m // tm,),
        in_specs=[_rows(tm, d)] + [_WHOLE] * 5 + [_rows(tm, c2), _rows(tm, c2)],
        out_specs=[_rows(tm, d), _rows(tm, c2)],
        out_shape=[jax.ShapeDtypeStruct((m, d), F32), jax.ShapeDtypeStruct((m, c2), F32)],
        compiler_params=_cp("parallel"),
    )(x, g, w_up, w_conv, b_conv, w_down, pm1, pm2)


def _sconv_kernel(*refs, mode, seq, tpb, cw):
    if mode == "stream":
        x_ref, g_ref, win_ref, wc_ref, wout_ref, o_ref, tail_ref, carry_scr = refs
    else:
        x_ref, g_ref, win_ref, wc_ref, wout_ref, pm1_ref, pm2_ref, o_ref, p_ref = refs
    x = x_ref[...]
    tm, d = x.shape
    h = _rms(x, g_ref[...]).astype(BF16)
    if mode == "stream":
        @pl.when(pl.program_id(0) % tpb == 0)
        def _():
            carry_scr[...] = jnp.zeros(carry_scr.shape, F32)
    acc = x
    for c0, w in _col_chunks(d, cw):
        cols = slice(c0, c0 + w)
        gb = _mm(h, win_ref[:, c0:c0 + w])
        gc = _mm(h, win_ref[:, d + c0:d + c0 + w])
        u = _mm(h, win_ref[:, 2 * d + c0:2 * d + c0 + w])
        p = gc * u
        if mode == "stream":
            y = _conv3(p, wc_ref, cols, ("stream", carry_scr[:, cols]))
            carry_scr[:, cols] = p[tm - SUBLANES:tm, :]
            tail_ref[0, :, cols] = p[tm - SUBLANES:tm, :]
        else:
            y = _conv3(p, wc_ref, cols, ("seq", seq, pm1_ref[:, cols], pm2_ref[:, cols]))
            p_ref[:, cols] = p
        acc = acc + _mm((gb * y).astype(BF16), wout_ref[c0:c0 + w, :])
    o_ref[...] = acc


def _sconv_stream(x, g, w_in, w_conv, w_out, *, tm, t_len):
    m, d = x.shape
    tpb = t_len // tm
    kern = functools.partial(_sconv_kernel, mode="stream", seq=None, tpb=tpb, cw=512)
    return pl.pallas_call(
        kern,
        grid=(m // tm,),
        in_specs=[_rows(tm, d)] + [_WHOLE] * 4,
        out_specs=[_rows(tm, d), pl.BlockSpec((1, SUBLANES, d), lambda i: (i // tpb, 0, 0))],
        out_shape=[jax.ShapeDtypeStruct((m, d), F32),
                   jax.ShapeDtypeStruct((m // t_len, SUBLANES, d), F32)],
        scratch_shapes=[pltpu.VMEM((SUBLANES, d), F32)],
        compiler_params=_cp("arbitrary"),
    )(x, g, w_in, w_conv, w_out)


def _sconv_seq(x, g, w_in, w_conv, w_out, pm1, pm2, *, tm, seq):
    m, d = x.shape
    kern = functools.partial(_sconv_kernel, mode="seq", seq=seq, tpb=None, cw=512)
    return pl.pallas_call(
        kern,
        grid=(m // tm,),
        in_specs=[_rows(tm, d)] + [_WHOLE] * 4 + [_rows(tm, d), _rows(tm, d)],
        out_specs=[_rows(tm, d), _rows(tm, d)],
        out_shape=[jax.ShapeDtypeStruct((m, d), F32), jax.ShapeDtypeStruct((m, d), F32)],
        compiler_params=_cp("parallel"),
    )(x, g, w_in, w_conv, w_out, pm1, pm2)


def _post_kernel(*refs, nb, has_mix):
    if has_mix:
        x_ref, a_ref, wmix_ref, g_ref, wq_ref, mk_ref, mv_ref, wo_ref, o_ref, oc_scr = refs
        x = x_ref[...] + _mm(a_ref[...], wmix_ref[...])
    else:
        x_ref, g_ref, wq_ref, mk_ref, mv_ref, wo_ref, o_ref, oc_scr = refs
        x = x_ref[...]
    tm, d = x.shape
    dh = d // XA_HEADS
    rpb = tm // nb
    h = _rms(x, g_ref[...]).astype(BF16)
    q = (_mm(h, wq_ref[...]) * (dh ** -0.5)).astype(BF16)
    for b in range(nb):
        rows = slice(b * rpb, (b + 1) * rpb)
        for hh in range(XA_HEADS):
            cols = slice(hh * dh, (hh + 1) * dh)
            s = _mm_nt(q[rows, cols], mk_ref[b, :, cols].astype(BF16))
            e = jnp.exp(s - jnp.max(s, axis=1, keepdims=True))
            p = e / jnp.sum(e, axis=1, keepdims=True)
            oc_scr[rows, cols] = _mm(p.astype(BF16), mv_ref[b, :, cols].astype(BF16)).astype(BF16)
    o_ref[...] = x + _mm(oc_scr[...], wo_ref[...])


def _post(x, a, w_mix, g, w_q, mk, mv, w_o, *, tm, nb, tiles_per_mem):
    m, d = x.shape
    n_mem = mk.shape[1]
    has_mix = a is not None
    mem_spec = pl.BlockSpec((nb, n_mem, d), lambda i: (i // tiles_per_mem, 0, 0))
    ins = [x] + ([a, w_mix] if has_mix else []) + [g, w_q, mk, mv, w_o]
    specs = ([_rows(tm, d)] + ([_rows(tm, d), _WHOLE] if has_mix else [])
             + [_WHOLE, _WHOLE, mem_spec, mem_spec, _WHOLE])
    return pl.pallas_call(
        functools.partial(_post_kernel, nb=nb, has_mix=has_mix),
        grid=(m // tm,),
        in_specs=specs,
        out_specs=_rows(tm, d),
        out_shape=jax.ShapeDtypeStruct((m, d), F32),
        scratch_shapes=[pltpu.VMEM((tm, d), BF16)],
        compiler_params=_cp("parallel"),
    )(*ins)


def _norm_kernel(x_ref, g_ref, o_ref):
    o_ref[...] = _rms(x_ref[...], g_ref[...])


def _final_norm(x, g, tm):
    m, d = x.shape
    return pl.pallas_call(
        _norm_kernel, grid=(m // tm,), in_specs=[_rows(tm, d), _WHOLE], out_specs=_rows(tm, d),
        out_shape=jax.ShapeDtypeStruct((m, d), F32), compiler_params=_cp("parallel"),
    )(x, g)


def _t5_bucket(rel):
    half = NUM_BUCKETS // 2
    ret = jnp.where(rel > 0, half, 0)
    n = jnp.abs(rel)
    max_exact = half // 2
    nf = jnp.maximum(n, 1).astype(jnp.float32)
    large = max_exact + (jnp.log(nf / max_exact) / math.log(MAX_DISTANCE / max_exact)
                         * (half - max_exact)).astype(jnp.int32)
    large = jnp.minimum(large, half - 1)
    return ret + jnp.where(n < max_exact, n, large)


def _near_bias(rel_bias):
    sl = jnp.arange(KEY_TILE, dtype=jnp.int32)[None, :, None]
    ql = jnp.arange(KEY_TILE, dtype=jnp.int32)[None, None, :]
    off = jnp.array([-KEY_TILE, 0], jnp.int32)[:, None, None]
    rel = sl + off - ql
    far = _t5_bucket(jnp.array(-2 * KEY_TILE, jnp.int32))
    tab = (rel_bias[_t5_bucket(rel)] - rel_bias[far]) * LOG2E
    return jnp.transpose(tab, (3, 0, 1, 2)).astype(F32)


def _seq_context(prev, seq):
    b, _, c = prev.shape
    z = jnp.zeros((b, seq, c), prev.dtype)
    pm1 = z.at[:, 0].set(prev[:, 1])
    pm2 = z.at[:, 0].set(prev[:, 0]).at[:, 1].set(prev[:, 1])
    return pm1.reshape(b * seq, c), pm2.reshape(b * seq, c)


def _pad_keys(cache, new, t_pad):
    b, p, c = cache.shape
    t = new.shape[1]
    pad = jnp.zeros((b, t_pad - p - t, c), BF16)
    return jnp.concatenate([cache.astype(BF16), new.astype(BF16), pad], axis=1)


def _forget_slot_maps(d):
    s_q = np.zeros((3 * LANES, 2 * d), np.float32)
    s_k = np.zeros((3 * LANES, 2 * d), np.float32)
    c_q = np.zeros((1, 2 * d), np.float32)
    c_k = np.zeros((1, 2 * d), np.float32)
    for h in range(H_A):
        base = h * 2 * HEAD_DIM + HEAD_DIM
        for p in range(3):
            s_q[p * LANES + h, base + 3 + p] = 1.0
            s_k[p * LANES + h, base + p] = -1.0
            c_q[0, base + p] = 1.0
            c_k[0, base + 3 + p] = 1.0
    return jnp.asarray(s_q, BF16), jnp.asarray(s_k, BF16), jnp.asarray(c_q), jnp.asarray(c_k)


def _split3(x):
    def top(v):
        bits = lax.bitcast_convert_type(v, jnp.uint32) & jnp.uint32(0xFFFF0000)
        return lax.bitcast_convert_type(bits, F32)

    hi = top(x)
    mid = top(x - hi)
    lo = x - hi - mid
    return [hi.astype(BF16), mid.astype(BF16), lo.astype(BF16)]


def _feature_major(a, t_pad):
    a = jnp.swapaxes(a, 1, 2)
    return jnp.pad(a, ((0, 0), (0, 0), (0, t_pad - a.shape[2])))


def _key_tiles_feature_major(a, tk):
    b, t, c = a.shape
    return jnp.swapaxes(a.reshape(b, t // tk, tk, c), 2, 3)


def kernel(x_prompt, x_sample, mem_prompt, cache_k_A, cache_v_A, cache_logf_A, cache_k_B, cache_v_B,
           cache_kidx_B, state_conv_C, state_ffconv, cache_mem_k, cache_mem_v, g_mix, g_xa, g_ffn,
           g_mem, g_final, w_in_A, b_f_A, w_out_A, w_in_B, w_out_B, rel_bias, w_in_C, w_conv_C,
           w_out_C, w_q_xa, w_k_xa, w_v_xa, w_o_xa, w_up, w_conv_ff, b_conv_ff, w_down):
    bp, tp, d = x_prompt.shape
    bs, ts, _ = x_sample.shape
    depth = g_mix.shape[0]
    past = cache_k_A.shape[2]
    n_mem = mem_prompt.shape[1]
    assert d == H_A * HEAD_DIM == H_B * HEAD_DIM and ts % SUBLANES == 0 and ts <= KEY_TILE
    assert past % KEY_TILE == 0 and tp % KEY_TILE == 0
    mixers = tuple("ABC"[l % 3] for l in range(depth))
    slot = tuple(mixers[:l].count(mixers[l]) for l in range(depth))
    mp, ms = bp * tp, bs * ts
    tm_p = min(256, tp)
    tm_s = min(ms, 8 * ts)
    tq_a = min(512, tp)
    tk_a = min(256, tp)
    t_all = past + ts
    t_pad = -(-t_all // KEY_TILE) * KEY_TILE
    k_top_p = min(TOPK_MAX, tp // 4)
    k_top_s = min(TOPK_MAX, t_all // 4)
    nb_s = min(4, bs)

    bf = lambda a: a.astype(BF16)
    row = lambda a: a.reshape(1, -1)
    xp = x_prompt.reshape(mp, d)
    xs = x_sample.reshape(ms, d)
    mem = mem_prompt.reshape(bp * n_mem, d)
    bias_near = _near_bias(rel_bias)

    outs = {n: [] for n in ("kA_p", "vA_p", "fA_p", "kB_p", "vB_p", "iB_p", "cC_p", "ff_p", "mk_p",
                            "mv_p", "kA_s", "vA_s", "fA_s", "kB_s", "vB_s", "iB_s", "cC_s", "ff_s")}

    slot_q, slot_k, ones_q, ones_k = _forget_slot_maps(d)

    def fox_prompt(x, g, j):
        w = w_in_A[j]
        widen = lambda a: bf(jnp.pad(a.reshape(d, H_A, HEAD_DIM),
                                     ((0, 0), (0, 0), (0, HEAD_DIM))).reshape(d, 2 * d))
        w_f = jnp.pad(w[:, 3 * d:], ((0, 0), (0, LANES - H_A)))
        b_f = jnp.pad(b_f_A[j], (0, LANES - H_A)).reshape(1, LANES)
        k, v, lf, qa, ka, vt = _a_in_seq(
            x, g, widen(w[:, :d]), widen(w[:, d:2 * d]), bf(w[:, 2 * d:3 * d]),
            bf(w[:, 2 * d:3 * d].T), bf(w_f), b_f, slot_q, slot_k, ones_q, ones_k,
            tm=tk_a, t_len=tp)
        o = _fox_attention(qa, ka, vt, b=bp, tq=tq_a, tk=tk_a, q_off=0)
        return o, k, v, lf

    def fox_sample(x, g, j, ck, cv, clf):
        b, t, nl = bs, ts, H_A * ts
        assert nl % LANES == 0 and 3 * H_A + 3 <= LANES
        q, k, v, kb, vb, lf = _a_in(x, g, bf(w_in_A[j][:, :3 * d]), bf(w_in_A[j][:, 3 * d:]),
                                    row(b_f_A[j]), min(256, ms))
        t_f = past + LANES
        lf_t = jnp.swapaxes(lf.reshape(b, t, H_A), 1, 2)
        lf_all = jnp.concatenate([jnp.swapaxes(clf, 1, 2).astype(F32), lf_t,
                                  jnp.zeros((b, H_A, LANES - t), F32)], axis=2)
        f_all = _cumsum_lanes(lf_all.reshape(b * H_A, t_f), min(64, b * H_A)).reshape(b, H_A, t_f)
        f3 = _split3(f_all * LOG2E)
        spare = LANES - 3 * H_A - 3
        kslot = jnp.concatenate([jnp.swapaxes(p, 1, 2) for p in f3]
                                + [jnp.ones((b, t_f, 3), BF16), jnp.zeros((b, t_f, spare), BF16)],
                                axis=-1)
        eye = jnp.eye(H_A, dtype=BF16)
        own = jnp.broadcast_to(-jnp.repeat(eye, t, axis=1), (b, H_A, nl))
        fq3 = [p[:, :, past:past + t].reshape(b, 1, nl) for p in f3]
        q_slot = jnp.concatenate([own] * 3 + fq3 + [jnp.zeros((b, spare, nl), BF16)], axis=1)
        q_feat = jnp.einsum("bthd,hg->bhdgt", q.reshape(b, t, H_A, HEAD_DIM), eye)
        qbd = jnp.concatenate([q_feat.reshape(b, d, nl), q_slot], axis=1)
        rows128 = lambda a: jnp.pad(a.reshape(b, t, d), ((0, 0), (0, LANES - t), (0, 0)))
        o = _fox_cache_attention(ck.reshape(b, past, d), cv.reshape(b, past, d), kslot,
                                 rows128(kb), rows128(vb), qbd, ts=t)
        return o.reshape(ms, d), k, v, lf

    def dsa(x, g, j, b, t, cache):
        m = b * t
        w = w_in_B[j]
        dkv = HKV_B * HEAD_DIM
        c0, c2, c3 = d, d + 2 * dkv, d + 2 * dkv + H_IDX * D_IDX
        q, k, v, kb, vb, qi, ki, kib, wi = _b_in(
            x, g, bf(w[:, :c0]), bf(w[:, c0:c2]), bf(w[:, c2:c3]),
            bf(jnp.concatenate([w[:, c3:c3 + D_IDX]] * 2, axis=1)), bf(w[:, c3 + D_IDX:]), min(256, m))
        if cache is None:
            q_off, k_top, tk_valid, tq_pad = 0, k_top_p, t, t
            k_all, v_all, ki_all = kb.reshape(b, t, dkv), vb.reshape(b, t, dkv), kib.reshape(b, t, -1)
        else:
            ck, cv, cki = cache
            q_off, k_top, tk_valid, tq_pad = past, k_top_s, t_all, KEY_TILE
            k_all = _pad_keys(ck.reshape(b, past, dkv), kb.reshape(b, t, dkv), t_pad)
            v_all = _pad_keys(cv.reshape(b, past, dkv), vb.reshape(b, t, dkv), t_pad)
            cki2 = jnp.concatenate([cki, cki], axis=-1)
            ki_all = _pad_keys(cki2, kib.reshape(b, t, -1), t_pad)
        ot = _dsa_attention(_feature_major(q.reshape(b, t, d), tq_pad),
                            _feature_major(qi.reshape(b, t, -1), tq_pad),
                            _feature_major(wi.reshape(b, t, H_IDX), tq_pad),
                            k_all, _key_tiles_feature_major(v_all, KEY_TILE), ki_all, bias_near,
                            q_off=q_off, k_top=k_top, tk_valid=tk_valid)
        o = jnp.swapaxes(ot[:, :, :t], 1, 2)
        return o.reshape(m, d), k, v, ki

    for l in range(depth):
        mix, j = mixers[l], slot[l]
        gm, gx, gf = row(g_mix[l]), row(g_xa[l]), row(g_ffn[l])
        mk, mv = _proj(mem, row(g_mem[l]), [bf(w_k_xa[l]), bf(w_v_xa[l])], min(256, bp * n_mem))
        outs["mk_p"].append(mk.reshape(bp, n_mem, XA_HEADS, d // XA_HEADS))
        outs["mv_p"].append(mv.reshape(bp, n_mem, XA_HEADS, d // XA_HEADS))
        mks, mvs = cache_mem_k[l].reshape(bs, n_mem, d), cache_mem_v[l].reshape(bs, n_mem, d)
        post = dict(g=gx, w_q=bf(w_q_xa[l]), w_o=bf(w_o_xa[l]))

        if mix == "A":
            ap, k, v, f = fox_prompt(xp, gm, j)
            outs["kA_p"].append(k.reshape(bp, tp, H_A, HEAD_DIM))
            outs["vA_p"].append(v.reshape(bp, tp, H_A, HEAD_DIM))
            outs["fA_p"].append(f.reshape(bp, tp, H_A))
            a_s, k, v, f = fox_sample(xs, gm, j, cache_k_A[j], cache_v_A[j], cache_logf_A[j])
            outs["kA_s"].append(k.reshape(bs, ts, H_A, HEAD_DIM))
            outs["vA_s"].append(v.reshape(bs, ts, H_A, HEAD_DIM))
            outs["fA_s"].append(f.reshape(bs, ts, H_A))
            w_mix = bf(w_out_A[j])
        elif mix == "B":
            ap, k, v, ki = dsa(xp, gm, j, bp, tp, None)
            outs["kB_p"].append(k.reshape(bp, tp, HKV_B, HEAD_DIM))
            outs["vB_p"].append(v.reshape(bp, tp, HKV_B, HEAD_DIM))
            outs["iB_p"].append(ki.reshape(bp, tp, D_IDX))
            a_s, k, v, ki = dsa(xs, gm, j, bs, ts, (cache_k_B[j], cache_v_B[j], cache_kidx_B[j]))
            outs["kB_s"].append(k.reshape(bs, ts, HKV_B, HEAD_DIM))
            outs["vB_s"].append(v.reshape(bs, ts, HKV_B, HEAD_DIM))
            outs["iB_s"].append(ki.reshape(bs, ts, D_IDX))
            w_mix = bf(w_out_B[j])
        else:
            xp, tail = _sconv_stream(xp, gm, bf(w_in_C[j]), w_conv_C[j], bf(w_out_C[j]),
                                     tm=tm_p, t_len=tp)
            outs["cC_p"].append(tail[:, SUBLANES - 2:, :])
            pm1, pm2 = _seq_context(state_conv_C[j], ts)
            xs, p_all = _sconv_seq(xs, gm, bf(w_in_C[j]), w_conv_C[j], bf(w_out_C[j]), pm1, pm2,
                                   tm=tm_s, seq=ts)
            outs["cC_s"].append(p_all.reshape(bs, ts, d)[:, ts - 2:, :])
            ap = a_s = w_mix = None

        xp = _post(xp, ap, w_mix, mk=mk.reshape(bp, n_mem, d), mv=mv.reshape(bp, n_mem, d),
                   tm=tm_p, nb=1, tiles_per_mem=tp // tm_p, **post)
        xs = _post(xs, a_s, w_mix, mk=mks, mv=mvs, tm=nb_s * ts, nb=nb_s, tiles_per_mem=1, **post)

        wu, wc, bc, wd = bf(w_up[l]), w_conv_ff[l], row(b_conv_ff[l]), bf(w_down[l])
        xp, tail = _ffn_stream(xp, gf, wu, wc, bc, wd, tm=tm_p, t_len=tp)
        outs["ff_p"].append(tail[:, SUBLANES - 2:, :])
        pm1, pm2 = _seq_context(state_ffconv[l], ts)
        xs, up_all = _ffn_seq(xs, gf, wu, wc, bc, wd, pm1, pm2, tm=tm_s, seq=ts)
        outs["ff_s"].append(up_all.reshape(bs, ts, -1)[:, ts - 2:, :])

    y_p = _final_norm(xp, row(g_final), tm_p).reshape(bp, tp, d)
    y_s = _final_norm(xs, row(g_final), min(256, ms)).reshape(bs, ts, d)
    st = lambda n: jnp.stack(outs[n])
    return (y_p, y_s, st("kA_p"), st("vA_p"), st("fA_p"), st("kB_p"), st("vB_p"), st("iB_p"),
            st("cC_p"), st("ff_p"), st("mk_p"), st("mv_p"), st("kA_s"), st("vA_s"), st("fA_s"),
            st("kB_s"), st("vB_s"), st("iB_s"), st("cC_s"), st("ff_s"))
```

```python
import functools
import math

import jax
import jax.numpy as jnp
import numpy as np
from jax import lax
from jax.experimental import pallas as pl
from jax.experimental.pallas import tpu as pltpu

F32, BF16, I32 = jnp.float32, jnp.bfloat16, jnp.int32

CHUNK = 64
H_A = 16
H_B = 16
HKV_B = 4
G_B = H_B // HKV_B
H_IDX = 8
D_IDX = 64
TOPK_MAX = 256
NUM_BUCKETS = 32
MAX_DISTANCE = 128
XA_HEADS = 4
EPS = 1e-6

LANES = 128
SUBLANES = 8
HEAD_DIM = 64
KEY_TILE = 128
VMEM_LIMIT_BYTES = 56 * 1024 * 1024

INT_MIN = -2 ** 31
NEG_INF = float("-inf")
LOG2E = 1.0 / math.log(2.0)
Q_SCALE = HEAD_DIM ** -0.5 * LOG2E

_WHOLE = pl.BlockSpec(memory_space=pltpu.VMEM)


def _cp(*sem):
    return pltpu.CompilerParams(dimension_semantics=sem, vmem_limit_bytes=VMEM_LIMIT_BYTES)


def _rows(tm, n):
    return pl.BlockSpec((tm, n), lambda i: (i, 0))


def _rms(x, g):
    return x * lax.rsqrt(jnp.mean(x * x, axis=-1, keepdims=True) + EPS) * g


def _mm(a, b):
    return jnp.dot(a, b, preferred_element_type=F32)


def _mm_nt(a, b):
    return lax.dot_general(a, b, (((1,), (1,)), ((), ())), preferred_element_type=F32)


def _proj_kernel(x_ref, g_ref, *refs, n_w):
    w_refs, o_refs = refs[:n_w], refs[n_w:]
    h = _rms(x_ref[...], g_ref[...]).astype(BF16)
    for w_ref, o_ref in zip(w_refs, o_refs):
        o_ref[...] = _mm(h, w_ref[...])


def _proj(x, g, ws, tm):
    m, d = x.shape
    return pl.pallas_call(
        functools.partial(_proj_kernel, n_w=len(ws)),
        grid=(m // tm,),
        in_specs=[_rows(tm, d), _WHOLE] + [_WHOLE] * len(ws),
        out_specs=[_rows(tm, w.shape[1]) for w in ws],
        out_shape=[jax.ShapeDtypeStruct((m, w.shape[1]), F32) for w in ws],
        compiler_params=_cp("parallel"),
    )(x, g, *ws)


def _a_in_kernel(x_ref, g_ref, w_ref, wf_ref, bf_ref, q_ref, k_ref, v_ref, kb_ref, vb_ref, lf_ref):
    d = q_ref.shape[-1]
    h = _rms(x_ref[...], g_ref[...]).astype(BF16)
    q_ref[...] = (_mm(h, w_ref[:, 0:d]) * Q_SCALE).astype(BF16)
    k = _mm(h, w_ref[:, d:2 * d])
    k_ref[...] = k
    kb_ref[...] = k.astype(BF16)
    v = _mm(h, w_ref[:, 2 * d:3 * d])
    v_ref[...] = v
    vb_ref[...] = v.astype(BF16)
    fl = _mm(h, wf_ref[...]) + bf_ref[...]
    lf_ref[...] = jnp.minimum(fl, 0.0) - jnp.log1p(jnp.exp(-jnp.abs(fl)))


def _a_in(x, g, w_qkv, w_f, b_f, tm):
    m, d = x.shape
    sds = jax.ShapeDtypeStruct
    return pl.pallas_call(
        _a_in_kernel,
        grid=(m // tm,),
        in_specs=[_rows(tm, d), _WHOLE, _WHOLE, _WHOLE, _WHOLE],
        out_specs=[_rows(tm, d)] * 5 + [_rows(tm, H_A)],
        out_shape=[sds((m, d), BF16), sds((m, d), F32), sds((m, d), F32),
                   sds((m, d), BF16), sds((m, d), BF16), sds((m, H_A), F32)],
        compiler_params=_cp("parallel"),
    )(x, g, w_qkv, w_f, b_f)


def _top_bits(v):
    bits = lax.bitcast_convert_type(v, I32) & jnp.int32(-65536)
    return lax.bitcast_convert_type(bits, F32)


def _a_in_seq_kernel(x_ref, g_ref, wqx_ref, wkx_ref, wv_ref, wvt_ref, wf_ref, bf_ref, sq_ref, sk_ref,
                     cq_ref, ck_ref, k_ref, v_ref, lf_ref, qa_ref, ka_ref, vt_ref, carry_scr, *, tpb):
    tm = x_ref.shape[0]
    h = _rms(x_ref[...], g_ref[...]).astype(BF16)
    lane = lax.broadcasted_iota(I32, (tm, LANES), 1)

    @pl.when(pl.program_id(0) % tpb == 0)
    def _():
        carry_scr[...] = jnp.zeros(carry_scr.shape, F32)

    fl = _mm(h, wf_ref[...]) + bf_ref[...]
    lf = jnp.minimum(fl, 0.0) - jnp.log1p(jnp.exp(-jnp.abs(fl)))
    lf = jnp.where(lane < H_A, lf, 0.0)
    lf_ref[...] = lf[:, 0:H_A]
    ra = lax.broadcasted_iota(I32, (tm, tm), 0)
    rb = lax.broadcasted_iota(I32, (tm, tm), 1)
    tri = (rb <= ra).astype(F32)
    f_cum = jnp.dot(tri, lf, precision=lax.Precision.HIGHEST,
                    preferred_element_type=F32) + carry_scr[0:1, :]
    carry_scr[...] = jnp.broadcast_to(f_cum[tm - 1:tm, :], carry_scr.shape)
    f2 = f_cum * LOG2E
    hi = _top_bits(f2)
    mid = _top_bits(f2 - hi)
    lo = f2 - hi - mid
    pieces = jnp.concatenate([hi, mid, lo], axis=1).astype(BF16)

    qa_ref[...] = (_mm(h, wqx_ref[...]) * Q_SCALE + _mm(pieces, sq_ref[...])
                   + cq_ref[...]).astype(BF16)
    kx = _mm(h, wkx_ref[...])
    ka_ref[...] = (kx + _mm(pieces, sk_ref[...]) + ck_ref[...]).astype(BF16)
    for p in range(k_ref.shape[1] // LANES):
        a = kx[:, (2 * p) * LANES:(2 * p + 1) * LANES]
        b = pltpu.roll(kx[:, (2 * p + 1) * LANES:(2 * p + 2) * LANES], HEAD_DIM, 1)
        k_ref[:, p * LANES:(p + 1) * LANES] = jnp.where(lane < HEAD_DIM, a, b)
    v_ref[...] = _mm(h, wv_ref[...])
    vt_ref[0] = _mm_nt(wvt_ref[...], h).astype(BF16)


def _a_in_seq(x, g, w_qx, w_kx, w_v, w_vt, w_f, b_f, s_q, s_k, c_q, c_k, *, tm, t_len):
    m, d = x.shape
    sds = jax.ShapeDtypeStruct
    kern = functools.partial(_a_in_seq_kernel, tpb=t_len // tm)
    return pl.pallas_call(
        kern,
        grid=(m // tm,),
        in_specs=[_rows(tm, d)] + [_WHOLE] * 11,
        out_specs=[_rows(tm, d), _rows(tm, d), _rows(tm, H_A), _rows(tm, 2 * d), _rows(tm, 2 * d),
                   pl.BlockSpec((1, d, tm), lambda i: (i, 0, 0))],
        out_shape=[sds((m, d), F32), sds((m, d), F32), sds((m, H_A), F32),
                   sds((m, 2 * d), BF16), sds((m, 2 * d), BF16), sds((m // tm, d, tm), BF16)],
        scratch_shapes=[pltpu.VMEM((SUBLANES, LANES), F32)],
        compiler_params=_cp("arbitrary"),
    )(x, g, w_qx, w_kx, w_v, w_vt, w_f, b_f, s_q, s_k, c_q, c_k)


def _b_in_kernel(x_ref, g_ref, wq_ref, wkv_ref, wqi_ref, wki_ref, wwi_ref, wvt_ref,
                 q_ref, k_ref, v_ref, kb_ref, vb_ref, qi_ref, ki_ref, kib_ref, wi_ref, vt_ref):
    dkv = k_ref.shape[-1]
    h = _rms(x_ref[...], g_ref[...]).astype(BF16)
    n_vt = x_ref.shape[0] // KEY_TILE
    if n_vt == 0:
        vt_ref[...] = jnp.zeros(vt_ref.shape, BF16)
    for c in range(n_vt):
        vt_ref[c] = _mm_nt(wvt_ref[...], h[c * KEY_TILE:(c + 1) * KEY_TILE, :]).astype(BF16)
    q_ref[...] = (_mm(h, wq_ref[...]) * Q_SCALE).astype(BF16)
    kv = _mm(h, wkv_ref[...])
    k_ref[...] = kv[:, 0:dkv]
    kb_ref[...] = kv[:, 0:dkv].astype(BF16)
    v_ref[...] = kv[:, dkv:2 * dkv]
    vb_ref[...] = kv[:, dkv:2 * dkv].astype(BF16)
    qi_ref[...] = (_mm(h, wqi_ref[...]) * (D_IDX ** -0.5)).astype(BF16)
    ki2 = _mm(h, wki_ref[...])
    ki_ref[...] = ki2[:, 0:D_IDX]
    kib_ref[...] = ki2.astype(BF16)
    wi_ref[...] = _mm(h, wwi_ref[...]) * (H_IDX ** -0.5)


def _b_in(x, g, w_q, w_kv, w_qi, w_ki2, w_wi, w_vt, tm):
    m, d = x.shape
    dkv = w_kv.shape[1] // 2
    dqi = w_qi.shape[1]
    n_vt = tm // KEY_TILE
    sds = jax.ShapeDtypeStruct
    return pl.pallas_call(
        _b_in_kernel,
        grid=(m // tm,),
        in_specs=[_rows(tm, d)] + [_WHOLE] * 7,
        out_specs=[_rows(tm, d), _rows(tm, dkv), _rows(tm, dkv), _rows(tm, dkv), _rows(tm, dkv),
                   _rows(tm, dqi), _rows(tm, D_IDX), _rows(tm, 2 * D_IDX), _rows(tm, H_IDX),
                   pl.BlockSpec((max(n_vt, 1), dkv, KEY_TILE), lambda i: (i, 0, 0))],
        out_shape=[sds((m, d), BF16), sds((m, dkv), F32), sds((m, dkv), F32),
                   sds((m, dkv), BF16), sds((m, dkv), BF16), sds((m, dqi), BF16),
                   sds((m, D_IDX), F32), sds((m, 2 * D_IDX), BF16), sds((m, H_IDX), F32),
                   sds((max(n_vt, 1) * (m // tm), dkv, KEY_TILE), BF16)],
        compiler_params=_cp("parallel"),
    )(x, g, w_q, w_kv, w_qi, w_ki2, w_wi, w_vt)


def _cumsum_kernel(x_ref, o_ref):
    n = x_ref.shape[-1]
    a = lax.broadcasted_iota(I32, (LANES, LANES), 0)
    b = lax.broadcasted_iota(I32, (LANES, LANES), 1)
    tri = (a <= b).astype(F32)
    carry = jnp.zeros((x_ref.shape[0], 1), F32)
    for c in range(n // LANES):
        sl = slice(c * LANES, (c + 1) * LANES)
        y = jnp.dot(x_ref[:, sl], tri, precision=lax.Precision.HIGHEST,
                    preferred_element_type=F32) + carry
        o_ref[:, sl] = y
        carry = y[:, LANES - 1:LANES]


def _cumsum_lanes(x, rb):
    r, n = x.shape
    return pl.pallas_call(
        _cumsum_kernel,
        grid=(r // rb,),
        in_specs=[_rows(rb, n)],
        out_specs=_rows(rb, n),
        out_shape=jax.ShapeDtypeStruct((r, n), F32),
        compiler_params=_cp("parallel"),
    )(x)


def _lookahead(units, depth):
    pending = []
    for idx in range(len(units) + depth):
        if idx < len(units):
            pending.append(units[idx][0]())
        if idx >= depth:
            units[idx - depth][1](pending[idx - depth])


def _fox_kernel(qa_ref, ka_ref, vt_ref, o_ref, qt_scr, m_scr, l_scr, acc_scr,
                *, tq, tk, tu, tpi, q_off):
    i = pl.program_id(2)
    for e in range(2):
        qt_scr[e] = qa_ref[:, e * LANES:(e + 1) * LANES].astype(F32).T.astype(BF16)
    m_scr[...] = jnp.full(m_scr.shape, NEG_INF, F32)
    l_scr[...] = jnp.zeros(l_scr.shape, F32)
    acc_scr[...] = jnp.zeros(acc_scr.shape, F32)
    q0 = q_off + i * tq

    def unit(kk, e, c, masked):
        cols = slice(c * tu, (c + 1) * tu)
        slab = slice(e * LANES, (e + 1) * LANES)

        def issue():
            return _mm(ka_ref[pl.ds(kk * tk, tk), slab], qt_scr[e, :, cols])

        def consume(s):
            if masked:
                kpos = kk * tk + lax.broadcasted_iota(I32, (tk, 1), 0)
                qpos = q0 + c * tu + lax.broadcasted_iota(I32, (1, tu), 1)
                s = jnp.where(kpos <= qpos, s, NEG_INF)
            m_old = m_scr[e, :, cols]
            m_new = jnp.maximum(m_old, jnp.max(s, axis=0, keepdims=True))
            alpha = jnp.exp2(m_old - m_new)
            p = jnp.exp2(s - m_new)
            l_scr[e, :, cols] = alpha * l_scr[e, :, cols] + jnp.sum(p, axis=0, keepdims=True)
            m_scr[e, :, cols] = m_new
            ve = vt_ref[kk, e * HEAD_DIM:(e + 1) * HEAD_DIM, :]
            acc_scr[e, :, cols] = acc_scr[e, :, cols] * alpha + _mm(ve, p.astype(BF16))

        return issue, consume

    def full_body(kp, carry):
        _lookahead([unit(tpi * kp + t, e, c, False)
                    for t in range(tpi) for c in range(tq // tu) for e in range(2)], 2)
        return carry

    assert tpi % 2 == 0 and q_off % (2 * tk) == 0 and tq % tk == 0 and tq % tu == 0
    assert tq // tk == 1 or tq % (2 * tk) == 0
    n_full = q0 // tk
    trips = n_full // tpi
    lax.fori_loop(0, trips, full_body, 0)
    for r in range(0, tpi, 2):
        @pl.when(n_full - trips * tpi == r)
        def _():
            tail = [unit(trips * tpi + t, e, c, False)
                    for t in range(r) for c in range(tq // tu) for e in range(2)]
            for t in range(tq // tk):
                for c in range(tq // tu):
                    if t * tk > (c + 1) * tu - 1:
                        continue
                    masked = (t + 1) * tk - 1 > c * tu
                    tail += [unit(n_full + t, e, c, masked) for e in range(2)]
            _lookahead(tail, 2)
    ot = jnp.concatenate([acc_scr[e] / l_scr[e] for e in range(2)], axis=0)
    o_ref[...] = ot.T.astype(BF16)


def _fox_attention(qa, ka, vt, *, b, tq, tk, q_off):
    d = qa.shape[1] // 2
    t_q, t_k = qa.shape[0] // b, ka.shape[0] // b
    nq, nkt = t_q // tq, t_k // tk
    tu = tq
    tpi = 4
    kern = functools.partial(_fox_kernel, tq=tq, tk=tk, tu=tu, tpi=tpi, q_off=q_off)
    return pl.pallas_call(
        kern,
        grid=(b, d // LANES, nq),
        in_specs=[pl.BlockSpec((tq, 2 * LANES), lambda bb, j, i: (bb * nq + i, j)),
                  pl.BlockSpec((t_k, 2 * LANES), lambda bb, j, i: (bb, j)),
                  pl.BlockSpec((nkt, LANES, tk), lambda bb, j, i: (bb, j, 0))],
        out_specs=pl.BlockSpec((tq, LANES), lambda bb, j, i: (bb * nq + i, j)),
        out_shape=jax.ShapeDtypeStruct((b * t_q, d), BF16),
        scratch_shapes=[pltpu.VMEM((2, LANES, tq), BF16),
                        pltpu.VMEM((2, 1, tq), F32), pltpu.VMEM((2, 1, tq), F32),
                        pltpu.VMEM((2, HEAD_DIM, tq), F32)],
        compiler_params=_cp("parallel", "parallel", "arbitrary"),
    )(qa, ka, vt)


def _fox_cache_kernel(ck_ref, cv_ref, kslot_ref, knew_ref, vnew_ref, qbd_ref, o_ref, s_scr, of_scr,
                      *, past, ts, kc):
    d = ck_ref.shape[2]
    nl = qbd_ref.shape[2]
    q_feat, q_slot = qbd_ref[0, 0:d, :], qbd_ref[0, d:d + LANES, :]
    for c in range(past // kc):
        rows = slice(c * kc, (c + 1) * kc)
        s_scr[rows, :] = (_mm(ck_ref[0, rows, :].astype(BF16), q_feat)
                          + _mm(kslot_ref[0, rows, :], q_slot))
    s_new = _mm(knew_ref[0], q_feat) + _mm(kslot_ref[0, past:past + LANES, :], q_slot)
    key_j = lax.broadcasted_iota(I32, (LANES, nl), 0)
    query = lax.broadcasted_iota(I32, (LANES, nl), 1) % ts
    s_scr[past:past + LANES, :] = jnp.where(key_j <= query, s_new, NEG_INF)
    s = s_scr[...]
    p = jnp.exp2(s - jnp.max(s, axis=0, keepdims=True))
    p = p / jnp.sum(p, axis=0, keepdims=True)
    pt = p.T.astype(BF16)
    of_scr[...] = _mm(pt[:, past:past + LANES], vnew_ref[0])
    for c in range(past // kc):
        rows = slice(c * kc, (c + 1) * kc)
        of_scr[...] += _mm(pt[:, rows], cv_ref[0, rows, :].astype(BF16))
    for h in range(d // HEAD_DIM):
        cols = slice(h * HEAD_DIM, (h + 1) * HEAD_DIM)
        o_ref[0, :, cols] = of_scr[h * ts:(h + 1) * ts, cols].astype(BF16)


def _fox_cache_attention(ck, cv, kslot, knew, vnew, qbd, *, ts):
    b, past, d = ck.shape
    nl = qbd.shape[2]
    blk = lambda *s: pl.BlockSpec((1,) + s, lambda i: (i, 0, 0))
    return pl.pallas_call(
        functools.partial(_fox_cache_kernel, past=past, ts=ts, kc=min(512, past)),
        grid=(b,),
        in_specs=[blk(past, d), blk(past, d), blk(past + LANES, LANES), blk(LANES, d), blk(LANES, d),
                  blk(d + LANES, nl)],
        out_specs=blk(ts, d),
        out_shape=jax.ShapeDtypeStruct((b, ts, d), BF16),
        scratch_shapes=[pltpu.VMEM((past + LANES, nl), F32), pltpu.VMEM((nl, d), F32)],
        compiler_params=_cp("parallel"),
    )(ck, cv, kslot, knew, vnew, qbd)


def _dsa_kernel(q_ref, qi_ref, wit_ref, k_ref, vt_ref, ki_ref, bias_ref, o_ref,
                keys_scr, mb_scr, qim_scr, qg_scr, m_scr, l_scr, acc_scr,
                *, q_off, k_top, tk_valid, nkt):
    tq = KEY_TILE
    i = pl.program_id(1)
    q0 = q_off + i * tq
    home = q0 // KEY_TILE
    qpos = q0 + lax.broadcasted_iota(I32, (1, tq), 1)
    chunk_shift = int(math.log2(CHUNK))
    qchunk = lax.shift_right_logical(qpos, chunk_shift)
    lane = lax.broadcasted_iota(I32, (tq, LANES), 1)
    kf = float(k_top)

    for h in range(H_IDX):
        pair = qi_ref[0, :, (h // 2) * LANES:(h // 2 + 1) * LANES].astype(F32)
        pair = jnp.where((lane >= HEAD_DIM) == (h % 2 == 1), pair, 0.0)
        qim_scr[h] = pair.T.astype(BF16)
    for g in range(HKV_B):
        for r in range(G_B):
            hq = g * G_B + r
            slab = q_ref[0, :, (hq // 2) * LANES:(hq // 2 + 1) * LANES].astype(F32)
            if hq % 2 != g % 2:
                slab = pltpu.roll(slab, HEAD_DIM, 1)
            slab = jnp.where((lane >= HEAD_DIM) == (g % 2 == 1), slab, 0.0)
            qg_scr[g, :, r * tq:(r + 1) * tq] = slab.T.astype(BF16)

    def score_unit(kt):
        def issue():
            kit = ki_ref[0, pl.ds(jnp.minimum(kt, nkt - 1) * KEY_TILE, KEY_TILE), :]
            return [_mm(kit, qim_scr[h]) for h in range(H_IDX)]

        def consume(dots):
            acc = jnp.zeros((KEY_TILE, tq), F32)
            for h in range(H_IDX):
                acc = acc + jnp.maximum(dots[h], 0.0) * wit_ref[0, h:h + 1, :]
            acc = jnp.where(acc == 0.0, 0.0, acc)
            bits = lax.bitcast_convert_type(acc, I32)
            key = bits ^ (lax.shift_right_arithmetic(bits, 31) & 0x7FFFFFFF)
            kpos = kt * KEY_TILE + lax.broadcasted_iota(I32, (KEY_TILE, 1), 0)
            adm = (lax.shift_right_logical(kpos, chunk_shift) <= qchunk) & (kpos < tk_valid)
            keys_scr[kt] = jnp.where(adm, key, INT_MIN)

        return issue, consume

    def score_body(kp, carry):
        _lookahead([score_unit(2 * kp), score_unit(2 * kp + 1)], 1)
        return carry

    lax.fori_loop(0, (home + 2) // 2, score_body, 0)
    keys_scr[home + 1] = jnp.full((KEY_TILE, tq), INT_MIN, I32)

    def count(cand, strict):
        def body(kp, a):
            for t in range(2):
                key = keys_scr[2 * kp + t]
                hit = (key > cand) if strict else (key >= cand)
                a = jnp.where(hit, a + 1.0, a)
            return a

        a = lax.fori_loop(0, (home + 2) // 2, body, jnp.zeros((KEY_TILE, tq), F32))
        return jnp.sum(a, axis=0, keepdims=True)

    zero = jnp.zeros((1, tq), I32)
    thr = jnp.where(count(zero, False) >= kf, zero, jnp.full((1, tq), INT_MIN, I32))

    def bit_body(b, t):
        cand = t + lax.shift_left(jnp.int32(1), 30 - b)
        return jnp.where(count(cand, False) >= kf, cand, t)

    thr = lax.fori_loop(0, 31, bit_body, thr)

    no_ties = jnp.max(jnp.abs(count(thr, False) - kf)) == 0.0

    @pl.when(no_ties)
    def _():
        def body(kt, carry):
            key = keys_scr[kt]
            mb_scr[kt] = jnp.where((key >= thr) & (key != INT_MIN), 0.0, NEG_INF)
            return carry

        lax.fori_loop(0, home + 1, body, 0)

    @pl.when(jnp.logical_not(no_ties))
    def _():
        need = kf - count(thr, True)
        ra = lax.broadcasted_iota(I32, (KEY_TILE, KEY_TILE), 0)
        rb = lax.broadcasted_iota(I32, (KEY_TILE, KEY_TILE), 1)
        earlier = (rb < ra).astype(BF16)

        def tie_body(kt, seen):
            key = keys_scr[kt]
            eq = key == thr
            eqf = jnp.where(eq, 1.0, 0.0)
            rank = _mm(earlier, eqf.astype(BF16)) + seen
            sel = ((key > thr) | (eq & (rank < need))) & (key != INT_MIN)
            mb_scr[kt] = jnp.where(sel, 0.0, NEG_INF)
            return seen + jnp.sum(eqf, axis=0, keepdims=True)

        lax.fori_loop(0, home + 1, tie_body, jnp.zeros((1, tq), F32))

    m_scr[...] = jnp.full(m_scr.shape, NEG_INF, F32)
    l_scr[...] = jnp.zeros(l_scr.shape, F32)
    acc_scr[...] = jnp.zeros(acc_scr.shape, F32)

    def unit(kt, g, near):
        def issue():
            ks = k_ref[0, pl.ds(kt * KEY_TILE, KEY_TILE), (g // 2) * LANES:(g // 2 + 1) * LANES]
            return _mm(ks, qg_scr[g])

        def consume(st):
            mb = mb_scr[kt]
            vt = vt_ref[kt, g * HEAD_DIM:(g + 1) * HEAD_DIM, :]
            for r in range(G_B):
                hq = g * G_B + r
                s = st[:, r * tq:(r + 1) * tq] + mb
                if near is not None:
                    s = s + bias_ref[hq, near]
                m_old = m_scr[hq]
                m_new = jnp.maximum(m_old, jnp.max(s, axis=0, keepdims=True))
                m_use = jnp.where(m_new == NEG_INF, 0.0, m_new)
                alpha = jnp.exp2(m_old - m_use)
                p = jnp.exp2(s - m_use)
                l_scr[hq] = alpha * l_scr[hq] + jnp.sum(p, axis=0, keepdims=True)
                m_scr[hq] = m_new
                acc_scr[hq] = acc_scr[hq] * alpha + _mm(vt, p.astype(BF16))

        return issue, consume

    def attend(tiles):
        _lookahead([unit(kt, g, near) for kt, near in tiles for g in range(HKV_B)], 2)

    far_tpi = 4

    def far_body(kp, carry):
        attend([(far_tpi * kp + t, None) for t in range(far_tpi)])
        return carry

    n_far = jnp.maximum(home - 1, 0)
    trips = n_far // far_tpi
    lax.fori_loop(0, trips, far_body, 0)
    for r in range(1, far_tpi):
        @pl.when(n_far - trips * far_tpi == r)
        def _():
            attend([(trips * far_tpi + t, None) for t in range(r)])

    @pl.when(home >= 1)
    def _():
        attend([(home - 1, 0), (home, 1)])

    @pl.when(home == 0)
    def _():
        attend([(home, 1)])

    for s in range(H_B // 2):
        ot = jnp.concatenate([acc_scr[2 * s + e] / l_scr[2 * s + e] for e in range(2)], axis=0)
        o_ref[0, :, s * LANES:(s + 1) * LANES] = ot.T.astype(BF16)


def _dsa_attention(q, qi, wit, k, vt, ki, bias, *, q_off, k_top, tk_valid):
    b, t_q, d = q.shape
    t_k = k.shape[1]
    nkt = t_k // KEY_TILE
    tq = KEY_TILE
    kern = functools.partial(_dsa_kernel, q_off=q_off, k_top=k_top, tk_valid=tk_valid, nkt=nkt)
    qblk = lambda n: pl.BlockSpec((1, tq, n), lambda bb, i: (bb, i, 0))
    kblk = lambda n: pl.BlockSpec((1, t_k, n), lambda bb, i: (bb, 0, 0))
    return pl.pallas_call(
        kern,
        grid=(b, t_q // tq),
        in_specs=[qblk(d), qblk(qi.shape[2]),
                  pl.BlockSpec((1, H_IDX, tq), lambda bb, i: (bb, 0, i)), kblk(k.shape[2]),
                  pl.BlockSpec((nkt, vt.shape[1], KEY_TILE), lambda bb, i: (bb, 0, 0)),
                  kblk(ki.shape[2]), _WHOLE],
        out_specs=qblk(d),
        out_shape=jax.ShapeDtypeStruct((b, t_q, d), BF16),
        scratch_shapes=[pltpu.VMEM((nkt + 1, KEY_TILE, tq), I32), pltpu.VMEM((nkt, KEY_TILE, tq), F32),
                        pltpu.VMEM((H_IDX, LANES, tq), BF16),
                        pltpu.VMEM((HKV_B, LANES, G_B * tq), BF16),
                        pltpu.VMEM((H_B, 1, tq), F32), pltpu.VMEM((H_B, 1, tq), F32),
                        pltpu.VMEM((H_B, HEAD_DIM, tq), F32)],
        compiler_params=_cp("parallel", "arbitrary"),
    )(q, qi, wit, k, vt, ki, bias)


def _conv3(u, w_ref, cols, prev):
    tm = u.shape[0]
    row = lax.broadcasted_iota(I32, (tm, 1), 0)
    r1 = pltpu.roll(u, 1, 0)
    r2 = pltpu.roll(u, 2, 0)
    if prev[0] == "stream":
        carry = prev[1]
        c6, c7 = carry[6:7, :], carry[7:8, :]
        um1 = jnp.where(row == 0, c7, r1)
        um2 = jnp.where(row == 0, c6, jnp.where(row == 1, c7, r2))
    else:
        _, seq, pm1, pm2 = prev
        t = row % seq
        um1 = jnp.where(t == 0, pm1, r1)
        um2 = jnp.where(t < 2, pm2, r2)
    return w_ref[0:1, cols] * um2 + w_ref[1:2, cols] * um1 + w_ref[2:3, cols] * u


def _col_chunks(n, width):
    out, c = [], 0
    while c < n:
        out.append((c, min(width, n - c)))
        c += width
    return out


def _ffn_kernel(*refs, mode, seq, tpb, dff, cw):
    if mode == "stream":
        x_ref, g_ref, wup_ref, wc_ref, bc_ref, wdn_ref, o_ref, tail_ref, carry_scr = refs
    else:
        x_ref, g_ref, wup_ref, wc_ref, bc_ref, wdn_ref, pm1_ref, pm2_ref, o_ref, up_ref = refs
    x = x_ref[...]
    tm = x.shape[0]
    h = _rms(x, g_ref[...]).astype(BF16)
    if mode == "stream":
        @pl.when(pl.program_id(0) % tpb == 0)
        def _():
            carry_scr[...] = jnp.zeros(carry_scr.shape, F32)
    acc = [x]

    def chunk(c0, w):
        def issue():
            return [_mm(h, wup_ref[:, base + c0:base + c0 + w]) for base in (0, dff)]

        def consume(ups):
            ys = []
            for base, up in zip((0, dff), ups):
                cols = slice(base + c0, base + c0 + w)
                if mode == "stream":
                    y = _conv3(up, wc_ref, cols, ("stream", carry_scr[:, cols]))
                    carry_scr[:, cols] = up[tm - SUBLANES:tm, :]
                    tail_ref[0, :, cols] = up[tm - SUBLANES:tm, :]
                else:
                    y = _conv3(up, wc_ref, cols, ("seq", seq, pm1_ref[:, cols], pm2_ref[:, cols]))
                    up_ref[:, cols] = up
                ys.append(y + bc_ref[:, cols])
            gate, val = ys
            act = (gate / (1.0 + jnp.exp(-gate))) * val
            acc[0] = acc[0] + _mm(act.astype(BF16), wdn_ref[c0:c0 + w, :])

        return issue, consume

    _lookahead([chunk(c0, w) for c0, w in _col_chunks(dff, cw)], 2)
    o_ref[...] = acc[0]


def _ffn_stream(x, g, w_up, w_conv, b_conv, w_down, *, tm, t_len):
    m, d = x.shape
    c2 = w_up.shape[1]
    tpb = t_len // tm
    kern = functools.partial(_ffn_kernel, mode="stream", seq=None, tpb=tpb, dff=c2 // 2, cw=256)
    return pl.pallas_call(
        kern,
        grid=(m // tm,),
        in_specs=[_rows(tm, d)] + [_WHOLE] * 5,
        out_specs=[_rows(tm, d), pl.BlockSpec((1, SUBLANES, c2), lambda i: (i // tpb, 0, 0))],
        out_shape=[jax.ShapeDtypeStruct((m, d), F32),
                   jax.ShapeDtypeStruct((m // t_len, SUBLANES, c2), F32)],
        scratch_shapes=[pltpu.VMEM((SUBLANES, c2), F32)],
        compiler_params=_cp("arbitrary"),
    )(x, g, w_up, w_conv, b_conv, w_down)


def _ffn_seq(x, g, w_up, w_conv, b_conv, w_down, pm1, pm2, *, tm, seq):
    m, d = x.shape
    c2 = w_up.shape[1]
    kern = functools.partial(_ffn_kernel, mode="seq", seq=seq, tpb=None, dff=c2 // 2, cw=512)
    return pl.pallas_call(
        kern,
        grid=(m // tm,),
        in_specs=[_rows(tm, d)] + [_WHOLE] * 5 + [_rows(tm, c2), _rows(tm, c2)],
        out_specs=[_rows(tm, d), _rows(tm, c2)],
        out_shape=[jax.ShapeDtypeStruct((m, d), F32), jax.ShapeDtypeStruct((m, c2), F32)],
        compiler_params=_cp("parallel"),
    )(x, g, w_up, w_conv, b_conv, w_down, pm1, pm2)


def _sconv_kernel(*refs, mode, seq, tpb, cw):
    if mode == "stream":
        x_ref, g_ref, win_ref, wc_ref, wout_ref, o_ref, tail_ref, carry_scr = refs
    else:
        x_ref, g_ref, win_ref, wc_ref, wout_ref, pm1_ref, pm2_ref, o_ref, p_ref = refs
    x = x_ref[...]
    tm, d = x.shape
    h = _rms(x, g_ref[...]).astype(BF16)
    if mode == "stream":
        @pl.when(pl.program_id(0) % tpb == 0)
        def _():
            carry_scr[...] = jnp.zeros(carry_scr.shape, F32)
    acc = x
    for c0, w in _col_chunks(d, cw):
        cols = slice(c0, c0 + w)
        gb = _mm(h, win_ref[:, c0:c0 + w])
        gc = _mm(h, win_ref[:, d + c0:d + c0 + w])
        u = _mm(h, win_ref[:, 2 * d + c0:2 * d + c0 + w])
        p = gc * u
        if mode == "stream":
            y = _conv3(p, wc_ref, cols, ("stream", carry_scr[:, cols]))
            carry_scr[:, cols] = p[tm - SUBLANES:tm, :]
            tail_ref[0, :, cols] = p[tm - SUBLANES:tm, :]
        else:
            y = _conv3(p, wc_ref, cols, ("seq", seq, pm1_ref[:, cols], pm2_ref[:, cols]))
            p_ref[:, cols] = p
        acc = acc + _mm((gb * y).astype(BF16), wout_ref[c0:c0 + w, :])
    o_ref[...] = acc


def _sconv_stream(x, g, w_in, w_conv, w_out, *, tm, t_len):
    m, d = x.shape
    tpb = t_len // tm
    kern = functools.partial(_sconv_kernel, mode="stream", seq=None, tpb=tpb, cw=512)
    return pl.pallas_call(
        kern,
        grid=(m // tm,),
        in_specs=[_rows(tm, d)] + [_WHOLE] * 4,
        out_specs=[_rows(tm, d), pl.BlockSpec((1, SUBLANES, d), lambda i: (i // tpb, 0, 0))],
        out_shape=[jax.ShapeDtypeStruct((m, d), F32),
                   jax.ShapeDtypeStruct((m // t_len, SUBLANES, d), F32)],
        scratch_shapes=[pltpu.VMEM((SUBLANES, d), F32)],
        compiler_params=_cp("arbitrary"),
    )(x, g, w_in, w_conv, w_out)


def _sconv_seq(x, g, w_in, w_conv, w_out, pm1, pm2, *, tm, seq):
    m, d = x.shape
    kern = functools.partial(_sconv_kernel, mode="seq", seq=seq, tpb=None, cw=512)
    return pl.pallas_call(
        kern,
        grid=(m // tm,),
        in_specs=[_rows(tm, d)] + [_WHOLE] * 4 + [_rows(tm, d), _rows(tm, d)],
        out_specs=[_rows(tm, d), _rows(tm, d)],
        out_shape=[jax.ShapeDtypeStruct((m, d), F32), jax.ShapeDtypeStruct((m, d), F32)],
        compiler_params=_cp("parallel"),
    )(x, g, w_in, w_conv, w_out, pm1, pm2)


def _post_kernel(*refs, nb, has_mix):
    if has_mix:
        x_ref, a_ref, wmix_ref, g_ref, wq_ref, mk_ref, mv_ref, wo_ref, o_ref, oc_scr = refs
        x = x_ref[...] + _mm(a_ref[...], wmix_ref[...])
    else:
        x_ref, g_ref, wq_ref, mk_ref, mv_ref, wo_ref, o_ref, oc_scr = refs
        x = x_ref[...]
    tm, d = x.shape
    dh = d // XA_HEADS
    rpb = tm // nb
    h = _rms(x, g_ref[...]).astype(BF16)
    q = (_mm(h, wq_ref[...]) * (dh ** -0.5)).astype(BF16)
    for b in range(nb):
        rows = slice(b * rpb, (b + 1) * rpb)
        for hh in range(XA_HEADS):
            cols = slice(hh * dh, (hh + 1) * dh)
            s = _mm_nt(q[rows, cols], mk_ref[b, :, cols].astype(BF16))
            e = jnp.exp(s - jnp.max(s, axis=1, keepdims=True))
            p = e / jnp.sum(e, axis=1, keepdims=True)
            oc_scr[rows, cols] = _mm(p.astype(BF16), mv_ref[b, :, cols].astype(BF16)).astype(BF16)
    o_ref[...] = x + _mm(oc_scr[...], wo_ref[...])


def _post(x, a, w_mix, g, w_q, mk, mv, w_o, *, tm, nb, tiles_per_mem):
    m, d = x.shape
    n_mem = mk.shape[1]
    has_mix = a is not None
    mem_spec = pl.BlockSpec((nb, n_mem, d), lambda i: (i // tiles_per_mem, 0, 0))
    ins = [x] + ([a, w_mix] if has_mix else []) + [g, w_q, mk, mv, w_o]
    specs = ([_rows(tm, d)] + ([_rows(tm, d), _WHOLE] if has_mix else [])
             + [_WHOLE, _WHOLE, mem_spec, mem_spec, _WHOLE])
    return pl.pallas_call(
        functools.partial(_post_kernel, nb=nb, has_mix=has_mix),
        grid=(m // tm,),
        in_specs=specs,
        out_specs=_rows(tm, d),
        out_shape=jax.ShapeDtypeStruct((m, d), F32),
        scratch_shapes=[pltpu.VMEM((tm, d), BF16)],
        compiler_params=_cp("parallel"),
    )(*ins)


def _norm_kernel(x_ref, g_ref, o_ref):
    o_ref[...] = _rms(x_ref[...], g_ref[...])


def _final_norm(x, g, tm):
    m, d = x.shape
    return pl.pallas_call(
        _norm_kernel, grid=(m // tm,), in_specs=[_rows(tm, d), _WHOLE], out_specs=_rows(tm, d),
        out_shape=jax.ShapeDtypeStruct((m, d), F32), compiler_params=_cp("parallel"),
    )(x, g)


def _t5_bucket(rel):
    half = NUM_BUCKETS // 2
    ret = jnp.where(rel > 0, half, 0)
    n = jnp.abs(rel)
    max_exact = half // 2
    nf = jnp.maximum(n, 1).astype(jnp.float32)
    large = max_exact + (jnp.log(nf / max_exact) / math.log(MAX_DISTANCE / max_exact)
                         * (half - max_exact)).astype(jnp.int32)
    large = jnp.minimum(large, half - 1)
    return ret + jnp.where(n < max_exact, n, large)


def _near_bias(rel_bias):
    sl = jnp.arange(KEY_TILE, dtype=jnp.int32)[None, :, None]
    ql = jnp.arange(KEY_TILE, dtype=jnp.int32)[None, None, :]
    off = jnp.array([-KEY_TILE, 0], jnp.int32)[:, None, None]
    rel = sl + off - ql
    far = _t5_bucket(jnp.array(-2 * KEY_TILE, jnp.int32))
    tab = (rel_bias[_t5_bucket(rel)] - rel_bias[far]) * LOG2E
    return jnp.transpose(tab, (3, 0, 1, 2)).astype(F32)


def _seq_context(prev, seq):
    b, _, c = prev.shape
    z = jnp.zeros((b, seq, c), prev.dtype)
    pm1 = z.at[:, 0].set(prev[:, 1])
    pm2 = z.at[:, 0].set(prev[:, 0]).at[:, 1].set(prev[:, 1])
    return pm1.reshape(b * seq, c), pm2.reshape(b * seq, c)


def _pad_keys(cache, new, t_pad):
    b, p, c = cache.shape
    t = new.shape[1]
    pad = jnp.zeros((b, t_pad - p - t, c), BF16)
    return jnp.concatenate([cache.astype(BF16), new.astype(BF16), pad], axis=1)


def _forget_slot_maps(d):
    s_q = np.zeros((3 * LANES, 2 * d), np.float32)
    s_k = np.zeros((3 * LANES, 2 * d), np.float32)
    c_q = np.zeros((1, 2 * d), np.float32)
    c_k = np.zeros((1, 2 * d), np.float32)
    for h in range(H_A):
        base = h * 2 * HEAD_DIM + HEAD_DIM
        for p in range(3):
            s_q[p * LANES + h, base + 3 + p] = 1.0
            s_k[p * LANES + h, base + p] = -1.0
            c_q[0, base + p] = 1.0
            c_k[0, base + 3 + p] = 1.0
    return jnp.asarray(s_q, BF16), jnp.asarray(s_k, BF16), jnp.asarray(c_q), jnp.asarray(c_k)


def _split3(x):
    def top(v):
        bits = lax.bitcast_convert_type(v, jnp.uint32) & jnp.uint32(0xFFFF0000)
        return lax.bitcast_convert_type(bits, F32)

    hi = top(x)
    mid = top(x - hi)
    lo = x - hi - mid
    return [hi.astype(BF16), mid.astype(BF16), lo.astype(BF16)]


def _feature_major(a, t_pad):
    a = jnp.swapaxes(a, 1, 2)
    return jnp.pad(a, ((0, 0), (0, 0), (0, t_pad - a.shape[2])))


def _key_tiles_feature_major(a, tk):
    b, t, c = a.shape
    return jnp.swapaxes(a.reshape(b, t // tk, tk, c), 2, 3)


def kernel(x_prompt, x_sample, mem_prompt, cache_k_A, cache_v_A, cache_logf_A, cache_k_B, cache_v_B,
           cache_kidx_B, state_conv_C, state_ffconv, cache_mem_k, cache_mem_v, g_mix, g_xa, g_ffn,
           g_mem, g_final, w_in_A, b_f_A, w_out_A, w_in_B, w_out_B, rel_bias, w_in_C, w_conv_C,
           w_out_C, w_q_xa, w_k_xa, w_v_xa, w_o_xa, w_up, w_conv_ff, b_conv_ff, w_down):
    bp, tp, d = x_prompt.shape
    bs, ts, _ = x_sample.shape
    depth = g_mix.shape[0]
    past = cache_k_A.shape[2]
    n_mem = mem_prompt.shape[1]
    assert d == H_A * HEAD_DIM == H_B * HEAD_DIM and ts % SUBLANES == 0 and ts <= KEY_TILE
    assert past % KEY_TILE == 0 and tp % KEY_TILE == 0
    mixers = tuple("ABC"[l % 3] for l in range(depth))
    slot = tuple(mixers[:l].count(mixers[l]) for l in range(depth))
    mp, ms = bp * tp, bs * ts
    tm_p = min(256, tp)
    tm_s = min(ms, 8 * ts)
    tq_a = min(512, tp)
    tk_a = min(256, tp)
    t_all = past + ts
    t_pad = -(-t_all // KEY_TILE) * KEY_TILE
    k_top_p = min(TOPK_MAX, tp // 4)
    k_top_s = min(TOPK_MAX, t_all // 4)
    nb_s = min(4, bs)

    bf = lambda a: a.astype(BF16)
    row = lambda a: a.reshape(1, -1)
    xp = x_prompt.reshape(mp, d)
    xs = x_sample.reshape(ms, d)
    mem = mem_prompt.reshape(bp * n_mem, d)
    bias_near = _near_bias(rel_bias)

    outs = {n: [] for n in ("kA_p", "vA_p", "fA_p", "kB_p", "vB_p", "iB_p", "cC_p", "ff_p", "mk_p",
                            "mv_p", "kA_s", "vA_s", "fA_s", "kB_s", "vB_s", "iB_s", "cC_s", "ff_s")}

    slot_q, slot_k, ones_q, ones_k = _forget_slot_maps(d)

    def fox_prompt(x, g, j):
        w = w_in_A[j]
        widen = lambda a: bf(jnp.pad(a.reshape(d, H_A, HEAD_DIM),
                                     ((0, 0), (0, 0), (0, HEAD_DIM))).reshape(d, 2 * d))
        w_f = jnp.pad(w[:, 3 * d:], ((0, 0), (0, LANES - H_A)))
        b_f = jnp.pad(b_f_A[j], (0, LANES - H_A)).reshape(1, LANES)
        k, v, lf, qa, ka, vt = _a_in_seq(
            x, g, widen(w[:, :d]), widen(w[:, d:2 * d]), bf(w[:, 2 * d:3 * d]),
            bf(w[:, 2 * d:3 * d].T), bf(w_f), b_f, slot_q, slot_k, ones_q, ones_k,
            tm=tk_a, t_len=tp)
        o = _fox_attention(qa, ka, vt, b=bp, tq=tq_a, tk=tk_a, q_off=0)
        return o, k, v, lf

    def fox_sample(x, g, j, ck, cv, clf):
        b, t, nl = bs, ts, H_A * ts
        assert nl % LANES == 0 and 3 * H_A + 3 <= LANES
        q, k, v, kb, vb, lf = _a_in(x, g, bf(w_in_A[j][:, :3 * d]), bf(w_in_A[j][:, 3 * d:]),
                                    row(b_f_A[j]), min(256, ms))
        t_f = past + LANES
        lf_t = jnp.swapaxes(lf.reshape(b, t, H_A), 1, 2)
        lf_all = jnp.concatenate([jnp.swapaxes(clf, 1, 2).astype(F32), lf_t,
                                  jnp.zeros((b, H_A, LANES - t), F32)], axis=2)
        f_all = _cumsum_lanes(lf_all.reshape(b * H_A, t_f), min(64, b * H_A)).reshape(b, H_A, t_f)
        f3 = _split3(f_all * LOG2E)
        spare = LANES - 3 * H_A - 3
        kslot = jnp.concatenate([jnp.swapaxes(p, 1, 2) for p in f3]
                                + [jnp.ones((b, t_f, 3), BF16), jnp.zeros((b, t_f, spare), BF16)],
                                axis=-1)
        eye = jnp.eye(H_A, dtype=BF16)
        own = jnp.broadcast_to(-jnp.repeat(eye, t, axis=1), (b, H_A, nl))
        fq3 = [p[:, :, past:past + t].reshape(b, 1, nl) for p in f3]
        q_slot = jnp.concatenate([own] * 3 + fq3 + [jnp.zeros((b, spare, nl), BF16)], axis=1)
        q_feat = jnp.einsum("bthd,hg->bhdgt", q.reshape(b, t, H_A, HEAD_DIM), eye)
        qbd = jnp.concatenate([q_feat.reshape(b, d, nl), q_slot], axis=1)
        rows128 = lambda a: jnp.pad(a.reshape(b, t, d), ((0, 0), (0, LANES - t), (0, 0)))
        o = _fox_cache_attention(ck.reshape(b, past, d), cv.reshape(b, past, d), kslot,
                                 rows128(kb), rows128(vb), qbd, ts=t)
        return o.reshape(ms, d), k, v, lf

    def dsa(x, g, j, b, t, cache):
        m = b * t
        w = w_in_B[j]
        dkv = HKV_B * HEAD_DIM
        c0, c2, c3 = d, d + 2 * dkv, d + 2 * dkv + H_IDX * D_IDX
        q, k, v, kb, vb, qi, ki, kib, wi, vt = _b_in(
            x, g, bf(w[:, :c0]), bf(w[:, c0:c2]), bf(w[:, c2:c3]),
            bf(jnp.concatenate([w[:, c3:c3 + D_IDX]] * 2, axis=1)), bf(w[:, c3 + D_IDX:]),
            bf(w[:, c0 + dkv:c2].T), min(256, m))
        if cache is None:
            q_off, k_top, tk_valid, tq_pad = 0, k_top_p, t, t
            k_all, ki_all = kb.reshape(b, t, dkv), kib.reshape(b, t, -1)
        else:
            ck, cv, cki = cache
            q_off, k_top, tk_valid, tq_pad = past, k_top_s, t_all, KEY_TILE
            k_all = _pad_keys(ck.reshape(b, past, dkv), kb.reshape(b, t, dkv), t_pad)
            v_all = _pad_keys(cv.reshape(b, past, dkv), vb.reshape(b, t, dkv), t_pad)
            vt = _key_tiles_feature_major(v_all, KEY_TILE).reshape(-1, dkv, KEY_TILE)
            cki2 = jnp.concatenate([cki, cki], axis=-1)
            ki_all = _pad_keys(cki2, kib.reshape(b, t, -1), t_pad)
        rows = lambda a: jnp.pad(a.reshape(b, t, -1), ((0, 0), (0, tq_pad - t), (0, 0)))
        o = _dsa_attention(rows(q), rows(qi), _feature_major(wi.reshape(b, t, H_IDX), tq_pad),
                           k_all, vt, ki_all, bias_near, q_off=q_off, k_top=k_top,
                           tk_valid=tk_valid)
        return o[:, :t].reshape(m, d), k, v, ki

    for l in range(depth):
        mix, j = mixers[l], slot[l]
        gm, gx, gf = row(g_mix[l]), row(g_xa[l]), row(g_ffn[l])
        mk, mv = _proj(mem, row(g_mem[l]), [bf(w_k_xa[l]), bf(w_v_xa[l])], min(256, bp * n_mem))
        outs["mk_p"].append(mk.reshape(bp, n_mem, XA_HEADS, d // XA_HEADS))
        outs["mv_p"].append(mv.reshape(bp, n_mem, XA_HEADS, d // XA_HEADS))
        mks, mvs = cache_mem_k[l].reshape(bs, n_mem, d), cache_mem_v[l].reshape(bs, n_mem, d)
        post = dict(g=gx, w_q=bf(w_q_xa[l]), w_o=bf(w_o_xa[l]))

        if mix == "A":
            ap, k, v, f = fox_prompt(xp, gm, j)
            outs["kA_p"].append(k.reshape(bp, tp, H_A, HEAD_DIM))
            outs["vA_p"].append(v.reshape(bp, tp, H_A, HEAD_DIM))
            outs["fA_p"].append(f.reshape(bp, tp, H_A))
            a_s, k, v, f = fox_sample(xs, gm, j, cache_k_A[j], cache_v_A[j], cache_logf_A[j])
            outs["kA_s"].append(k.reshape(bs, ts, H_A, HEAD_DIM))
            outs["vA_s"].append(v.reshape(bs, ts, H_A, HEAD_DIM))
            outs["fA_s"].append(f.reshape(bs, ts, H_A))
            w_mix = bf(w_out_A[j])
        elif mix == "B":
            ap, k, v, ki = dsa(xp, gm, j, bp, tp, None)
            outs["kB_p"].append(k.reshape(bp, tp, HKV_B, HEAD_DIM))
            outs["vB_p"].append(v.reshape(bp, tp, HKV_B, HEAD_DIM))
            outs["iB_p"].append(ki.reshape(bp, tp, D_IDX))
            a_s, k, v, ki = dsa(xs, gm, j, bs, ts, (cache_k_B[j], cache_v_B[j], cache_kidx_B[j]))
            outs["kB_s"].append(k.reshape(bs, ts, HKV_B, HEAD_DIM))
            outs["vB_s"].append(v.reshape(bs, ts, HKV_B, HEAD_DIM))
            outs["iB_s"].append(ki.reshape(bs, ts, D_IDX))
            w_mix = bf(w_out_B[j])
        else:
            xp, tail = _sconv_stream(xp, gm, bf(w_in_C[j]), w_conv_C[j], bf(w_out_C[j]),
                                     tm=tm_p, t_len=tp)
            outs["cC_p"].append(tail[:, SUBLANES - 2:, :])
            pm1, pm2 = _seq_context(state_conv_C[j], ts)
            xs, p_all = _sconv_seq(xs, gm, bf(w_in_C[j]), w_conv_C[j], bf(w_out_C[j]), pm1, pm2,
                                   tm=tm_s, seq=ts)
            outs["cC_s"].append(p_all.reshape(bs, ts, d)[:, ts - 2:, :])
            ap = a_s = w_mix = None

        xp = _post(xp, ap, w_mix, mk=mk.reshape(bp, n_mem, d), mv=mv.reshape(bp, n_mem, d),
                   tm=tm_p, nb=1, tiles_per_mem=tp // tm_p, **post)
        xs = _post(xs, a_s, w_mix, mk=mks, mv=mvs, tm=nb_s * ts, nb=nb_s, tiles_per_mem=1, **post)

        wu, wc, bc, wd = bf(w_up[l]), w_conv_ff[l], row(b_conv_ff[l]), bf(w_down[l])
        xp, tail = _ffn_stream(xp, gf, wu, wc, bc, wd, tm=tm_p, t_len=tp)
        outs["ff_p"].append(tail[:, SUBLANES - 2:, :])
        pm1, pm2 = _seq_context(state_ffconv[l], ts)
        xs, up_all = _ffn_seq(xs, gf, wu, wc, bc, wd, pm1, pm2, tm=tm_s, seq=ts)
        outs["ff_s"].append(up_all.reshape(bs, ts, -1)[:, ts - 2:, :])

    y_p = _final_norm(xp, row(g_final), tm_p).reshape(bp, tp, d)
    y_s = _final_norm(xs, row(g_final), min(256, ms)).reshape(bs, ts, d)
    st = lambda n: jnp.stack(outs[n])
    return (y_p, y_s, st("kA_p"), st("vA_p"), st("fA_p"), st("kB_p"), st("vB_p"), st("iB_p"),
            st("cC_p"), st("ff_p"), st("mk_p"), st("mv_p"), st("kA_s"), st("vA_s"), st("fA_s"),
            st("kB_s"), st("vB_s"), st("iB_s"), st("cC_s"), st("ff_s"))
```

```python
import functools
import math

import jax
import jax.numpy as jnp
import numpy as np
from jax import lax
from jax.experimental import pallas as pl
from jax.experimental.pallas import tpu as pltpu

F32, BF16, I32 = jnp.float32, jnp.bfloat16, jnp.int32

CHUNK = 64
H_A = 16
H_B = 16
HKV_B = 4
G_B = H_B // HKV_B
H_IDX = 8
D_IDX = 64
TOPK_MAX = 256
NUM_BUCKETS = 32
MAX_DISTANCE = 128
XA_HEADS = 4
EPS = 1e-6

LANES = 128
SUBLANES = 8
HEAD_DIM = 64
KEY_TILE = 128
VMEM_LIMIT_BYTES = 56 * 1024 * 1024

INT_MIN = -2 ** 31
NEG_INF = float("-inf")
LOG2E = 1.0 / math.log(2.0)
Q_SCALE = HEAD_DIM ** -0.5 * LOG2E

_WHOLE = pl.BlockSpec(memory_space=pltpu.VMEM)


def _cp(*sem):
    return pltpu.CompilerParams(dimension_semantics=sem, vmem_limit_bytes=VMEM_LIMIT_BYTES)


def _rows(tm, n):
    return pl.BlockSpec((tm, n), lambda i: (i, 0))


def _rms(x, g):
    return x * lax.rsqrt(jnp.mean(x * x, axis=-1, keepdims=True) + EPS) * g


def _mm(a, b):
    return jnp.dot(a, b, preferred_element_type=F32)


def _mm_nt(a, b):
    return lax.dot_general(a, b, (((1,), (1,)), ((), ())), preferred_element_type=F32)


def _proj_kernel(x_ref, g_ref, *refs, n_w):
    w_refs, o_refs = refs[:n_w], refs[n_w:]
    h = _rms(x_ref[...], g_ref[...]).astype(BF16)
    for w_ref, o_ref in zip(w_refs, o_refs):
        o_ref[...] = _mm(h, w_ref[...])


def _proj(x, g, ws, tm):
    m, d = x.shape
    return pl.pallas_call(
        functools.partial(_proj_kernel, n_w=len(ws)),
        grid=(m // tm,),
        in_specs=[_rows(tm, d), _WHOLE] + [_WHOLE] * len(ws),
        out_specs=[_rows(tm, w.shape[1]) for w in ws],
        out_shape=[jax.ShapeDtypeStruct((m, w.shape[1]), F32) for w in ws],
        compiler_params=_cp("parallel"),
    )(x, g, *ws)


def _a_in_kernel(x_ref, g_ref, w_ref, wf_ref, bf_ref, q_ref, k_ref, v_ref, kb_ref, vb_ref, lf_ref):
    d = q_ref.shape[-1]
    h = _rms(x_ref[...], g_ref[...]).astype(BF16)
    q_ref[...] = (_mm(h, w_ref[:, 0:d]) * Q_SCALE).astype(BF16)
    k = _mm(h, w_ref[:, d:2 * d])
    k_ref[...] = k
    kb_ref[...] = k.astype(BF16)
    v = _mm(h, w_ref[:, 2 * d:3 * d])
    v_ref[...] = v
    vb_ref[...] = v.astype(BF16)
    fl = _mm(h, wf_ref[...]) + bf_ref[...]
    lf_ref[...] = jnp.minimum(fl, 0.0) - jnp.log1p(jnp.exp(-jnp.abs(fl)))


def _a_in(x, g, w_qkv, w_f, b_f, tm):
    m, d = x.shape
    sds = jax.ShapeDtypeStruct
    return pl.pallas_call(
        _a_in_kernel,
        grid=(m // tm,),
        in_specs=[_rows(tm, d), _WHOLE, _WHOLE, _WHOLE, _WHOLE],
        out_specs=[_rows(tm, d)] * 5 + [_rows(tm, H_A)],
        out_shape=[sds((m, d), BF16), sds((m, d), F32), sds((m, d), F32),
                   sds((m, d), BF16), sds((m, d), BF16), sds((m, H_A), F32)],
        compiler_params=_cp("parallel"),
    )(x, g, w_qkv, w_f, b_f)


def _top_bits(v):
    bits = lax.bitcast_convert_type(v, I32) & jnp.int32(-65536)
    return lax.bitcast_convert_type(bits, F32)


def _a_in_seq_kernel(x_ref, g_ref, wq_ref, wk_ref, wv_ref, wvt_ref, wf_ref, bf_ref, sq_ref, sk_ref,
                     cq_ref, ck_ref, k_ref, v_ref, lf_ref, qa_ref, ka_ref, vt_ref, carry_scr, *, tpb):
    tm = x_ref.shape[0]
    h = _rms(x_ref[...], g_ref[...]).astype(BF16)
    lane = lax.broadcasted_iota(I32, (tm, LANES), 1)

    @pl.when(pl.program_id(0) % tpb == 0)
    def _():
        carry_scr[...] = jnp.zeros(carry_scr.shape, F32)

    fl = _mm(h, wf_ref[...]) + bf_ref[...]
    lf = jnp.minimum(fl, 0.0) - jnp.log1p(jnp.exp(-jnp.abs(fl)))
    lf = jnp.where(lane < H_A, lf, 0.0)
    lf_ref[...] = lf[:, 0:H_A]
    ra = lax.broadcasted_iota(I32, (tm, tm), 0)
    rb = lax.broadcasted_iota(I32, (tm, tm), 1)
    tri = (rb <= ra).astype(F32)
    f_cum = jnp.dot(tri, lf, precision=lax.Precision.HIGHEST,
                    preferred_element_type=F32) + carry_scr[0:1, :]
    carry_scr[...] = jnp.broadcast_to(f_cum[tm - 1:tm, :], carry_scr.shape)
    f2 = f_cum * LOG2E
    hi = _top_bits(f2)
    mid = _top_bits(f2 - hi)
    lo = f2 - hi - mid
    pieces = jnp.concatenate([hi, mid, lo], axis=1).astype(BF16)

    q = _mm(h, wq_ref[...]) * Q_SCALE
    k = _mm(h, wk_ref[...])
    k_ref[...] = k
    slots_q = _mm(pieces, sq_ref[...]) + cq_ref[...]
    slots_k = _mm(pieces, sk_ref[...]) + ck_ref[...]
    for feat, slots, out_ref in ((q, slots_q, qa_ref), (k, slots_k, ka_ref)):
        for hd in range(H_A):
            pair = slice((hd // 2) * LANES, (hd // 2 + 1) * LANES)
            a, b = feat[:, pair], slots[:, pair]
            if hd % 2 == 0:
                b = pltpu.roll(b, HEAD_DIM, 1)
            else:
                a = pltpu.roll(a, HEAD_DIM, 1)
            out_ref[:, hd * LANES:(hd + 1) * LANES] = jnp.where(lane < HEAD_DIM, a, b).astype(BF16)
    v_ref[...] = _mm(h, wv_ref[...])
    vt_ref[0] = _mm_nt(wvt_ref[...], h).astype(BF16)


def _a_in_seq(x, g, w_q, w_k, w_v, w_vt, w_f, b_f, s_q, s_k, c_q, c_k, *, tm, t_len):
    m, d = x.shape
    sds = jax.ShapeDtypeStruct
    kern = functools.partial(_a_in_seq_kernel, tpb=t_len // tm)
    return pl.pallas_call(
        kern,
        grid=(m // tm,),
        in_specs=[_rows(tm, d)] + [_WHOLE] * 11,
        out_specs=[_rows(tm, d), _rows(tm, d), _rows(tm, H_A), _rows(tm, 2 * d), _rows(tm, 2 * d),
                   pl.BlockSpec((1, d, tm), lambda i: (i, 0, 0))],
        out_shape=[sds((m, d), F32), sds((m, d), F32), sds((m, H_A), F32),
                   sds((m, 2 * d), BF16), sds((m, 2 * d), BF16), sds((m // tm, d, tm), BF16)],
        scratch_shapes=[pltpu.VMEM((SUBLANES, LANES), F32)],
        compiler_params=_cp("arbitrary"),
    )(x, g, w_q, w_k, w_v, w_vt, w_f, b_f, s_q, s_k, c_q, c_k)


def _b_in_kernel(x_ref, g_ref, wq_ref, wkv_ref, wqi_ref, wki_ref, wwi_ref, wvt_ref,
                 q_ref, k_ref, v_ref, kb_ref, vb_ref, qi_ref, ki_ref, kib_ref, wi_ref, vt_ref):
    dkv = k_ref.shape[-1]
    h = _rms(x_ref[...], g_ref[...]).astype(BF16)
    n_vt = x_ref.shape[0] // KEY_TILE
    if n_vt == 0:
        vt_ref[...] = jnp.zeros(vt_ref.shape, BF16)
    for c in range(n_vt):
        vt_ref[c] = _mm_nt(wvt_ref[...], h[c * KEY_TILE:(c + 1) * KEY_TILE, :]).astype(BF16)
    q_ref[...] = (_mm(h, wq_ref[...]) * Q_SCALE).astype(BF16)
    kv = _mm(h, wkv_ref[...])
    k_ref[...] = kv[:, 0:dkv]
    kb_ref[...] = kv[:, 0:dkv].astype(BF16)
    v_ref[...] = kv[:, dkv:2 * dkv]
    vb_ref[...] = kv[:, dkv:2 * dkv].astype(BF16)
    qi_ref[...] = (_mm(h, wqi_ref[...]) * (D_IDX ** -0.5)).astype(BF16)
    ki2 = _mm(h, wki_ref[...])
    ki_ref[...] = ki2[:, 0:D_IDX]
    kib_ref[...] = ki2.astype(BF16)
    wi_ref[...] = _mm(h, wwi_ref[...]) * (H_IDX ** -0.5)


def _b_in(x, g, w_q, w_kv, w_qi, w_ki2, w_wi, w_vt, tm):
    m, d = x.shape
    dkv = w_kv.shape[1] // 2
    dqi = w_qi.shape[1]
    n_vt = tm // KEY_TILE
    sds = jax.ShapeDtypeStruct
    return pl.pallas_call(
        _b_in_kernel,
        grid=(m // tm,),
        in_specs=[_rows(tm, d)] + [_WHOLE] * 7,
        out_specs=[_rows(tm, d), _rows(tm, dkv), _rows(tm, dkv), _rows(tm, dkv), _rows(tm, dkv),
                   _rows(tm, dqi), _rows(tm, D_IDX), _rows(tm, 2 * D_IDX), _rows(tm, H_IDX),
                   pl.BlockSpec((max(n_vt, 1), dkv, KEY_TILE), lambda i: (i, 0, 0))],
        out_shape=[sds((m, d), BF16), sds((m, dkv), F32), sds((m, dkv), F32),
                   sds((m, dkv), BF16), sds((m, dkv), BF16), sds((m, dqi), BF16),
                   sds((m, D_IDX), F32), sds((m, 2 * D_IDX), BF16), sds((m, H_IDX), F32),
                   sds((max(n_vt, 1) * (m // tm), dkv, KEY_TILE), BF16)],
        compiler_params=_cp("parallel"),
    )(x, g, w_q, w_kv, w_qi, w_ki2, w_wi, w_vt)


def _cumsum_kernel(x_ref, o_ref):
    n = x_ref.shape[-1]
    a = lax.broadcasted_iota(I32, (LANES, LANES), 0)
    b = lax.broadcasted_iota(I32, (LANES, LANES), 1)
    tri = (a <= b).astype(F32)
    carry = jnp.zeros((x_ref.shape[0], 1), F32)
    for c in range(n // LANES):
        sl = slice(c * LANES, (c + 1) * LANES)
        y = jnp.dot(x_ref[:, sl], tri, precision=lax.Precision.HIGHEST,
                    preferred_element_type=F32) + carry
        o_ref[:, sl] = y
        carry = y[:, LANES - 1:LANES]


def _cumsum_lanes(x, rb):
    r, n = x.shape
    return pl.pallas_call(
        _cumsum_kernel,
        grid=(r // rb,),
        in_specs=[_rows(rb, n)],
        out_specs=_rows(rb, n),
        out_shape=jax.ShapeDtypeStruct((r, n), F32),
        compiler_params=_cp("parallel"),
    )(x)


def _lookahead(units, depth):
    pending = []
    for idx in range(len(units) + depth):
        if idx < len(units):
            pending.append(units[idx][0]())
        if idx >= depth:
            units[idx - depth][1](pending[idx - depth])


def _fox_kernel(qa_ref, ka_ref, vt_ref, o_ref, qt_scr, m_scr, l_scr, acc_scr,
                *, tq, tk, tu, tpi, q_off):
    i = pl.program_id(2)
    for e in range(2):
        qt_scr[e] = qa_ref[:, e * LANES:(e + 1) * LANES].astype(F32).T.astype(BF16)
    m_scr[...] = jnp.full(m_scr.shape, NEG_INF, F32)
    l_scr[...] = jnp.zeros(l_scr.shape, F32)
    acc_scr[...] = jnp.zeros(acc_scr.shape, F32)
    q0 = q_off + i * tq

    def unit(kk, e, c, masked):
        cols = slice(c * tu, (c + 1) * tu)
        slab = slice(e * LANES, (e + 1) * LANES)

        def issue():
            return _mm(ka_ref[pl.ds(kk * tk, tk), slab], qt_scr[e, :, cols])

        def consume(s):
            if masked:
                kpos = kk * tk + lax.broadcasted_iota(I32, (tk, 1), 0)
                qpos = q0 + c * tu + lax.broadcasted_iota(I32, (1, tu), 1)
                s = jnp.where(kpos <= qpos, s, NEG_INF)
            m_old = m_scr[e, :, cols]
            m_new = jnp.maximum(m_old, jnp.max(s, axis=0, keepdims=True))
            alpha = jnp.exp2(m_old - m_new)
            p = jnp.exp2(s - m_new)
            l_scr[e, :, cols] = alpha * l_scr[e, :, cols] + jnp.sum(p, axis=0, keepdims=True)
            m_scr[e, :, cols] = m_new
            ve = vt_ref[kk, e * HEAD_DIM:(e + 1) * HEAD_DIM, :]
            acc_scr[e, :, cols] = acc_scr[e, :, cols] * alpha + _mm(ve, p.astype(BF16))

        return issue, consume

    def full_body(kp, carry):
        _lookahead([unit(tpi * kp + t, e, c, False)
                    for t in range(tpi) for c in range(tq // tu) for e in range(2)], 2)
        return carry

    assert tpi % 2 == 0 and q_off % (2 * tk) == 0 and tq % tk == 0 and tq % tu == 0
    assert tq // tk == 1 or tq % (2 * tk) == 0
    n_full = q0 // tk
    trips = n_full // tpi
    lax.fori_loop(0, trips, full_body, 0)
    for r in range(0, tpi, 2):
        @pl.when(n_full - trips * tpi == r)
        def _():
            tail = [unit(trips * tpi + t, e, c, False)
                    for t in range(r) for c in range(tq // tu) for e in range(2)]
            for t in range(tq // tk):
                for c in range(tq // tu):
                    if t * tk > (c + 1) * tu - 1:
                        continue
                    masked = (t + 1) * tk - 1 > c * tu
                    tail += [unit(n_full + t, e, c, masked) for e in range(2)]
            _lookahead(tail, 2)
    ot = jnp.concatenate([acc_scr[e] / l_scr[e] for e in range(2)], axis=0)
    o_ref[...] = ot.T.astype(BF16)


def _fox_attention(qa, ka, vt, *, b, tq, tk, q_off):
    d = qa.shape[1] // 2
    t_q, t_k = qa.shape[0] // b, ka.shape[0] // b
    nq, nkt = t_q // tq, t_k // tk
    tu = tq
    tpi = 4
    kern = functools.partial(_fox_kernel, tq=tq, tk=tk, tu=tu, tpi=tpi, q_off=q_off)
    return pl.pallas_call(
        kern,
        grid=(b, d // LANES, nq),
        in_specs=[pl.BlockSpec((tq, 2 * LANES), lambda bb, j, i: (bb * nq + i, j)),
                  pl.BlockSpec((t_k, 2 * LANES), lambda bb, j, i: (bb, j)),
                  pl.BlockSpec((nkt, LANES, tk), lambda bb, j, i: (bb, j, 0))],
        out_specs=pl.BlockSpec((tq, LANES), lambda bb, j, i: (bb * nq + i, j)),
        out_shape=jax.ShapeDtypeStruct((b * t_q, d), BF16),
        scratch_shapes=[pltpu.VMEM((2, LANES, tq), BF16),
                        pltpu.VMEM((2, 1, tq), F32), pltpu.VMEM((2, 1, tq), F32),
                        pltpu.VMEM((2, HEAD_DIM, tq), F32)],
        compiler_params=_cp("parallel", "parallel", "arbitrary"),
    )(qa, ka, vt)


def _fox_cache_kernel(ck_ref, cv_ref, kslot_ref, knew_ref, vnew_ref, qbd_ref, o_ref, s_scr, of_scr,
                      *, past, ts, kc):
    d = ck_ref.shape[2]
    nl = qbd_ref.shape[2]
    q_feat, q_slot = qbd_ref[0, 0:d, :], qbd_ref[0, d:d + LANES, :]
    for c in range(past // kc):
        rows = slice(c * kc, (c + 1) * kc)
        s_scr[rows, :] = (_mm(ck_ref[0, rows, :].astype(BF16), q_feat)
                          + _mm(kslot_ref[0, rows, :], q_slot))
    s_new = _mm(knew_ref[0], q_feat) + _mm(kslot_ref[0, past:past + LANES, :], q_slot)
    key_j = lax.broadcasted_iota(I32, (LANES, nl), 0)
    query = lax.broadcasted_iota(I32, (LANES, nl), 1) % ts
    s_scr[past:past + LANES, :] = jnp.where(key_j <= query, s_new, NEG_INF)
    s = s_scr[...]
    p = jnp.exp2(s - jnp.max(s, axis=0, keepdims=True))
    p = p / jnp.sum(p, axis=0, keepdims=True)
    pt = p.T.astype(BF16)
    of_scr[...] = _mm(pt[:, past:past + LANES], vnew_ref[0])
    for c in range(past // kc):
        rows = slice(c * kc, (c + 1) * kc)
        of_scr[...] += _mm(pt[:, rows], cv_ref[0, rows, :].astype(BF16))
    for h in range(d // HEAD_DIM):
        cols = slice(h * HEAD_DIM, (h + 1) * HEAD_DIM)
        o_ref[0, :, cols] = of_scr[h * ts:(h + 1) * ts, cols].astype(BF16)


def _fox_cache_attention(ck, cv, kslot, knew, vnew, qbd, *, ts):
    b, past, d = ck.shape
    nl = qbd.shape[2]
    blk = lambda *s: pl.BlockSpec((1,) + s, lambda i: (i, 0, 0))
    return pl.pallas_call(
        functools.partial(_fox_cache_kernel, past=past, ts=ts, kc=min(512, past)),
        grid=(b,),
        in_specs=[blk(past, d), blk(past, d), blk(past + LANES, LANES), blk(LANES, d), blk(LANES, d),
                  blk(d + LANES, nl)],
        out_specs=blk(ts, d),
        out_shape=jax.ShapeDtypeStruct((b, ts, d), BF16),
        scratch_shapes=[pltpu.VMEM((past + LANES, nl), F32), pltpu.VMEM((nl, d), F32)],
        compiler_params=_cp("parallel"),
    )(ck, cv, kslot, knew, vnew, qbd)


def _dsa_kernel(q_ref, qi_ref, wit_ref, k_ref, vt_ref, ki_ref, bias_ref, o_ref,
                keys_scr, mb_scr, qim_scr, qg_scr, m_scr, l_scr, acc_scr,
                *, q_off, k_top, tk_valid, nkt):
    tq = KEY_TILE
    i = pl.program_id(1)
    q0 = q_off + i * tq
    home = q0 // KEY_TILE
    qpos = q0 + lax.broadcasted_iota(I32, (1, tq), 1)
    chunk_shift = int(math.log2(CHUNK))
    qchunk = lax.shift_right_logical(qpos, chunk_shift)
    lane = lax.broadcasted_iota(I32, (tq, LANES), 1)
    kf = float(k_top)

    for h in range(H_IDX):
        pair = qi_ref[0, :, (h // 2) * LANES:(h // 2 + 1) * LANES].astype(F32)
        pair = jnp.where((lane >= HEAD_DIM) == (h % 2 == 1), pair, 0.0)
        qim_scr[h] = pair.T.astype(BF16)
    for g in range(HKV_B):
        for r in range(G_B):
            hq = g * G_B + r
            slab = q_ref[0, :, (hq // 2) * LANES:(hq // 2 + 1) * LANES].astype(F32)
            if hq % 2 != g % 2:
                slab = pltpu.roll(slab, HEAD_DIM, 1)
            slab = jnp.where((lane >= HEAD_DIM) == (g % 2 == 1), slab, 0.0)
            qg_scr[g, :, r * tq:(r + 1) * tq] = slab.T.astype(BF16)

    def score_unit(kt):
        def issue():
            kit = ki_ref[0, pl.ds(jnp.minimum(kt, nkt - 1) * KEY_TILE, KEY_TILE), :]
            return [_mm(kit, qim_scr[h]) for h in range(H_IDX)]

        def consume(dots):
            acc = jnp.zeros((KEY_TILE, tq), F32)
            for h in range(H_IDX):
                acc = acc + jnp.maximum(dots[h], 0.0) * wit_ref[0, h:h + 1, :]
            acc = jnp.where(acc == 0.0, 0.0, acc)
            bits = lax.bitcast_convert_type(acc, I32)
            key = bits ^ (lax.shift_right_arithmetic(bits, 31) & 0x7FFFFFFF)
            kpos = kt * KEY_TILE + lax.broadcasted_iota(I32, (KEY_TILE, 1), 0)
            adm = (lax.shift_right_logical(kpos, chunk_shift) <= qchunk) & (kpos < tk_valid)
            keys_scr[kt] = jnp.where(adm, key, INT_MIN)

        return issue, consume

    def score_body(kp, carry):
        _lookahead([score_unit(2 * kp), score_unit(2 * kp + 1)], 1)
        return carry

    lax.fori_loop(0, (home + 2) // 2, score_body, 0)
    keys_scr[home + 1] = jnp.full((KEY_TILE, tq), INT_MIN, I32)

    def count(cand, strict):
        def body(kp, a):
            for t in range(2):
                key = keys_scr[2 * kp + t]
                hit = (key > cand) if strict else (key >= cand)
                a = jnp.where(hit, a + 1.0, a)
            return a

        a = lax.fori_loop(0, (home + 2) // 2, body, jnp.zeros((KEY_TILE, tq), F32))
        return jnp.sum(a, axis=0, keepdims=True)

    zero = jnp.zeros((1, tq), I32)
    thr = jnp.where(count(zero, False) >= kf, zero, jnp.full((1, tq), INT_MIN, I32))

    def bit_body(b, t):
        cand = t + lax.shift_left(jnp.int32(1), 30 - b)
        return jnp.where(count(cand, False) >= kf, cand, t)

    thr = lax.fori_loop(0, 31, bit_body, thr)

    no_ties = jnp.max(jnp.abs(count(thr, False) - kf)) == 0.0

    @pl.when(no_ties)
    def _():
        def body(kt, carry):
            key = keys_scr[kt]
            mb_scr[kt] = jnp.where((key >= thr) & (key != INT_MIN), 0.0, NEG_INF)
            return carry

        lax.fori_loop(0, home + 1, body, 0)

    @pl.when(jnp.logical_not(no_ties))
    def _():
        need = kf - count(thr, True)
        ra = lax.broadcasted_iota(I32, (KEY_TILE, KEY_TILE), 0)
        rb = lax.broadcasted_iota(I32, (KEY_TILE, KEY_TILE), 1)
        earlier = (rb < ra).astype(BF16)

        def tie_body(kt, seen):
            key = keys_scr[kt]
            eq = key == thr
            eqf = jnp.where(eq, 1.0, 0.0)
            rank = _mm(earlier, eqf.astype(BF16)) + seen
            sel = ((key > thr) | (eq & (rank < need))) & (key != INT_MIN)
            mb_scr[kt] = jnp.where(sel, 0.0, NEG_INF)
            return seen + jnp.sum(eqf, axis=0, keepdims=True)

        lax.fori_loop(0, home + 1, tie_body, jnp.zeros((1, tq), F32))

    m_scr[...] = jnp.full(m_scr.shape, NEG_INF, F32)
    l_scr[...] = jnp.zeros(l_scr.shape, F32)
    acc_scr[...] = jnp.zeros(acc_scr.shape, F32)

    def unit(kt, g, near):
        def issue():
            ks = k_ref[0, pl.ds(kt * KEY_TILE, KEY_TILE), (g // 2) * LANES:(g // 2 + 1) * LANES]
            return _mm(ks, qg_scr[g])

        def consume(st):
            mb = mb_scr[kt]
            vt = vt_ref[kt, g * HEAD_DIM:(g + 1) * HEAD_DIM, :]
            for r in range(G_B):
                hq = g * G_B + r
                s = st[:, r * tq:(r + 1) * tq] + mb
                if near is not None:
                    s = s + bias_ref[hq, near]
                m_old = m_scr[hq]
                m_new = jnp.maximum(m_old, jnp.max(s, axis=0, keepdims=True))
                m_use = jnp.where(m_new == NEG_INF, 0.0, m_new)
                alpha = jnp.exp2(m_old - m_use)
                p = jnp.exp2(s - m_use)
                l_scr[hq] = alpha * l_scr[hq] + jnp.sum(p, axis=0, keepdims=True)
                m_scr[hq] = m_new
                acc_scr[hq] = acc_scr[hq] * alpha + _mm(vt, p.astype(BF16))

        return issue, consume

    def attend(tiles):
        _lookahead([unit(kt, g, near) for kt, near in tiles for g in range(HKV_B)], 2)

    far_tpi = 4

    def far_body(kp, carry):
        attend([(far_tpi * kp + t, None) for t in range(far_tpi)])
        return carry

    n_far = jnp.maximum(home - 1, 0)
    trips = n_far // far_tpi
    lax.fori_loop(0, trips, far_body, 0)
    for r in range(1, far_tpi):
        @pl.when(n_far - trips * far_tpi == r)
        def _():
            attend([(trips * far_tpi + t, None) for t in range(r)])

    @pl.when(home >= 1)
    def _():
        attend([(home - 1, 0), (home, 1)])

    @pl.when(home == 0)
    def _():
        attend([(home, 1)])

    for s in range(H_B // 2):
        ot = jnp.concatenate([acc_scr[2 * s + e] / l_scr[2 * s + e] for e in range(2)], axis=0)
        o_ref[0, :, s * LANES:(s + 1) * LANES] = ot.T.astype(BF16)


def _dsa_attention(q, qi, wit, k, vt, ki, bias, *, q_off, k_top, tk_valid):
    b, t_q, d = q.shape
    t_k = k.shape[1]
    nkt = t_k // KEY_TILE
    tq = KEY_TILE
    kern = functools.partial(_dsa_kernel, q_off=q_off, k_top=k_top, tk_valid=tk_valid, nkt=nkt)
    qblk = lambda n: pl.BlockSpec((1, tq, n), lambda bb, i: (bb, i, 0))
    kblk = lambda n: pl.BlockSpec((1, t_k, n), lambda bb, i: (bb, 0, 0))
    return pl.pallas_call(
        kern,
        grid=(b, t_q // tq),
        in_specs=[qblk(d), qblk(qi.shape[2]),
                  pl.BlockSpec((1, H_IDX, tq), lambda bb, i: (bb, 0, i)), kblk(k.shape[2]),
                  pl.BlockSpec((nkt, vt.shape[1], KEY_TILE), lambda bb, i: (bb, 0, 0)),
                  kblk(ki.shape[2]), _WHOLE],
        out_specs=qblk(d),
        out_shape=jax.ShapeDtypeStruct((b, t_q, d), BF16),
        scratch_shapes=[pltpu.VMEM((nkt + 1, KEY_TILE, tq), I32), pltpu.VMEM((nkt, KEY_TILE, tq), F32),
                        pltpu.VMEM((H_IDX, LANES, tq), BF16),
                        pltpu.VMEM((HKV_B, LANES, G_B * tq), BF16),
                        pltpu.VMEM((H_B, 1, tq), F32), pltpu.VMEM((H_B, 1, tq), F32),
                        pltpu.VMEM((H_B, HEAD_DIM, tq), F32)],
        compiler_params=_cp("parallel", "arbitrary"),
    )(q, qi, wit, k, vt, ki, bias)


def _conv3(u, w_ref, cols, prev):
    tm = u.shape[0]
    row = lax.broadcasted_iota(I32, (tm, 1), 0)
    r1 = pltpu.roll(u, 1, 0)
    r2 = pltpu.roll(u, 2, 0)
    if prev[0] == "stream":
        carry = prev[1]
        c6, c7 = carry[6:7, :], carry[7:8, :]
        um1 = jnp.where(row == 0, c7, r1)
        um2 = jnp.where(row == 0, c6, jnp.where(row == 1, c7, r2))
    else:
        _, seq, pm1, pm2 = prev
        t = row % seq
        um1 = jnp.where(t == 0, pm1, r1)
        um2 = jnp.where(t < 2, pm2, r2)
    return w_ref[0:1, cols] * um2 + w_ref[1:2, cols] * um1 + w_ref[2:3, cols] * u


def _col_chunks(n, width):
    out, c = [], 0
    while c < n:
        out.append((c, min(width, n - c)))
        c += width
    return out


def _ffn_kernel(*refs, mode, seq, tpb, dff, cw):
    if mode == "stream":
        x_ref, g_ref, wup_ref, wc_ref, bc_ref, wdn_ref, o_ref, tail_ref, carry_scr = refs
    else:
        x_ref, g_ref, wup_ref, wc_ref, bc_ref, wdn_ref, pm1_ref, pm2_ref, o_ref, up_ref = refs
    x = x_ref[...]
    tm = x.shape[0]
    h = _rms(x, g_ref[...]).astype(BF16)
    if mode == "stream":
        @pl.when(pl.program_id(0) % tpb == 0)
        def _():
            carry_scr[...] = jnp.zeros(carry_scr.shape, F32)
    acc = [x]

    def chunk(c0, w):
        def issue():
            return [_mm(h, wup_ref[:, base + c0:base + c0 + w]) for base in (0, dff)]

        def consume(ups):
            ys = []
            for base, up in zip((0, dff), ups):
                cols = slice(base + c0, base + c0 + w)
                if mode == "stream":
                    y = _conv3(up, wc_ref, cols, ("stream", carry_scr[:, cols]))
                    carry_scr[:, cols] = up[tm - SUBLANES:tm, :]
                    tail_ref[0, :, cols] = up[tm - SUBLANES:tm, :]
                else:
                    y = _conv3(up, wc_ref, cols, ("seq", seq, pm1_ref[:, cols], pm2_ref[:, cols]))
                    up_ref[:, cols] = up
                ys.append(y + bc_ref[:, cols])
            gate, val = ys
            act = (gate / (1.0 + jnp.exp(-gate))) * val
            acc[0] = acc[0] + _mm(act.astype(BF16), wdn_ref[c0:c0 + w, :])

        return issue, consume

    _lookahead([chunk(c0, w) for c0, w in _col_chunks(dff, cw)], 2)
    o_ref[...] = acc[0]


def _ffn_stream(x, g, w_up, w_conv, b_conv, w_down, *, tm, t_len):
    m, d = x.shape
    c2 = w_up.shape[1]
    tpb = t_len // tm
    kern = functools.partial(_ffn_kernel, mode="stream", seq=None, tpb=tpb, dff=c2 // 2, cw=256)
    return pl.pallas_call(
        kern,
        grid=(m // tm,),
        in_specs=[_rows(tm, d)] + [_WHOLE] * 5,
        out_specs=[_rows(tm, d), pl.BlockSpec((1, SUBLANES, c2), lambda i: (i // tpb, 0, 0))],
        out_shape=[jax.ShapeDtypeStruct((m, d), F32),
                   jax.ShapeDtypeStruct((m // t_len, SUBLANES, c2), F32)],
        scratch_shapes=[pltpu.VMEM((SUBLANES, c2), F32)],
        compiler_params=_cp("arbitrary"),
    )(x, g, w_up, w_conv, b_conv, w_down)


def _ffn_seq(x, g, w_up, w_conv, b_conv, w_down, pm1, pm2, *, tm, seq):
    m, d = x.shape
    c2 = w_up.shape[1]
    kern = functools.partial(_ffn_kernel, mode="seq", seq=seq, tpb=None, dff=c2 // 2, cw=512)
    return pl.pallas_call(
        kern,
        grid=(m // tm,),
        in_specs=[_rows(tm, d)] + [_WHOLE] * 5 + [_rows(tm, c2), _rows(tm, c2)],
        out_specs=[_rows(tm, d), _rows(tm, c2)],
        out_shape=[jax.ShapeDtypeStruct((m, d), F32), jax.ShapeDtypeStruct((m, c2), F32)],
        compiler_params=_cp("parallel"),
    )(x, g, w_up, w_conv, b_conv, w_down, pm1, pm2)


def _sconv_kernel(*refs, mode, seq, tpb, cw):
    if mode == "stream":
        x_ref, g_ref, win_ref, wc_ref, wout_ref, o_ref, tail_ref, carry_scr = refs
    else:
        x_ref, g_ref, win_ref, wc_ref, wout_ref, pm1_ref, pm2_ref, o_ref, p_ref = refs
    x = x_ref[...]
    tm, d = x.shape
    h = _rms(x, g_ref[...]).astype(BF16)
    if mode == "stream":
        @pl.when(pl.program_id(0) % tpb == 0)
        def _():
            carry_scr[...] = jnp.zeros(carry_scr.shape, F32)
    acc = [x]

    def chunk(c0, w):
        cols = slice(c0, c0 + w)

        def issue():
            return [_mm(h, win_ref[:, base + c0:base + c0 + w]) for base in (0, d, 2 * d)]

        def consume(zs):
            gb, gc, u = zs
            p = gc * u
            if mode == "stream":
                y = _conv3(p, wc_ref, cols, ("stream", carry_scr[:, cols]))
                carry_scr[:, cols] = p[tm - SUBLANES:tm, :]
                tail_ref[0, :, cols] = p[tm - SUBLANES:tm, :]
            else:
                y = _conv3(p, wc_ref, cols, ("seq", seq, pm1_ref[:, cols], pm2_ref[:, cols]))
                p_ref[:, cols] = p
            acc[0] = acc[0] + _mm((gb * y).astype(BF16), wout_ref[c0:c0 + w, :])

        return issue, consume

    _lookahead([chunk(c0, w) for c0, w in _col_chunks(d, cw)], 1)
    o_ref[...] = acc[0]


def _sconv_stream(x, g, w_in, w_conv, w_out, *, tm, t_len):
    m, d = x.shape
    tpb = t_len // tm
    kern = functools.partial(_sconv_kernel, mode="stream", seq=None, tpb=tpb, cw=256)
    return pl.pallas_call(
        kern,
        grid=(m // tm,),
        in_specs=[_rows(tm, d)] + [_WHOLE] * 4,
        out_specs=[_rows(tm, d), pl.BlockSpec((1, SUBLANES, d), lambda i: (i // tpb, 0, 0))],
        out_shape=[jax.ShapeDtypeStruct((m, d), F32),
                   jax.ShapeDtypeStruct((m // t_len, SUBLANES, d), F32)],
        scratch_shapes=[pltpu.VMEM((SUBLANES, d), F32)],
        compiler_params=_cp("arbitrary"),
    )(x, g, w_in, w_conv, w_out)


def _sconv_seq(x, g, w_in, w_conv, w_out, pm1, pm2, *, tm, seq):
    m, d = x.shape
    kern = functools.partial(_sconv_kernel, mode="seq", seq=seq, tpb=None, cw=512)
    return pl.pallas_call(
        kern,
        grid=(m // tm,),
        in_specs=[_rows(tm, d)] + [_WHOLE] * 4 + [_rows(tm, d), _rows(tm, d)],
        out_specs=[_rows(tm, d), _rows(tm, d)],
        out_shape=[jax.ShapeDtypeStruct((m, d), F32), jax.ShapeDtypeStruct((m, d), F32)],
        compiler_params=_cp("parallel"),
    )(x, g, w_in, w_conv, w_out, pm1, pm2)


def _post_kernel(*refs, nb, has_mix):
    if has_mix:
        x_ref, a_ref, wmix_ref, g_ref, wq_ref, mk_ref, mv_ref, wo_ref, o_ref, oc_scr = refs
        x = x_ref[...] + _mm(a_ref[...], wmix_ref[...])
    else:
        x_ref, g_ref, wq_ref, mk_ref, mv_ref, wo_ref, o_ref, oc_scr = refs
        x = x_ref[...]
    tm, d = x.shape
    dh = d // XA_HEADS
    rpb = tm // nb
    h = _rms(x, g_ref[...]).astype(BF16)
    q = (_mm(h, wq_ref[...]) * (dh ** -0.5)).astype(BF16)
    for b in range(nb):
        rows = slice(b * rpb, (b + 1) * rpb)
        for hh in range(XA_HEADS):
            cols = slice(hh * dh, (hh + 1) * dh)
            s = _mm_nt(q[rows, cols], mk_ref[b, :, cols].astype(BF16))
            e = jnp.exp(s - jnp.max(s, axis=1, keepdims=True))
            p = e / jnp.sum(e, axis=1, keepdims=True)
            oc_scr[rows, cols] = _mm(p.astype(BF16), mv_ref[b, :, cols].astype(BF16)).astype(BF16)
    o_ref[...] = x + _mm(oc_scr[...], wo_ref[...])


def _post(x, a, w_mix, g, w_q, mk, mv, w_o, *, tm, nb, tiles_per_mem):
    m, d = x.shape
    n_mem = mk.shape[1]
    has_mix = a is not None
    mem_spec = pl.BlockSpec((nb, n_mem, d), lambda i: (i // tiles_per_mem, 0, 0))
    ins = [x] + ([a, w_mix] if has_mix else []) + [g, w_q, mk, mv, w_o]
    specs = ([_rows(tm, d)] + ([_rows(tm, d), _WHOLE] if has_mix else [])
             + [_WHOLE, _WHOLE, mem_spec, mem_spec, _WHOLE])
    return pl.pallas_call(
        functools.partial(_post_kernel, nb=nb, has_mix=has_mix),
        grid=(m // tm,),
        in_specs=specs,
        out_specs=_rows(tm, d),
        out_shape=jax.ShapeDtypeStruct((m, d), F32),
        scratch_shapes=[pltpu.VMEM((tm, d), BF16)],
        compiler_params=_cp("parallel"),
    )(*ins)


def _norm_kernel(x_ref, g_ref, o_ref):
    o_ref[...] = _rms(x_ref[...], g_ref[...])


def _final_norm(x, g, tm):
    m, d = x.shape
    return pl.pallas_call(
        _norm_kernel, grid=(m // tm,), in_specs=[_rows(tm, d), _WHOLE], out_specs=_rows(tm, d),
        out_shape=jax.ShapeDtypeStruct((m, d), F32), compiler_params=_cp("parallel"),
    )(x, g)


def _t5_bucket(rel):
    half = NUM_BUCKETS // 2
    ret = jnp.where(rel > 0, half, 0)
    n = jnp.abs(rel)
    max_exact = half // 2
    nf = jnp.maximum(n, 1).astype(jnp.float32)
    large = max_exact + (jnp.log(nf / max_exact) / math.log(MAX_DISTANCE / max_exact)
                         * (half - max_exact)).astype(jnp.int32)
    large = jnp.minimum(large, half - 1)
    return ret + jnp.where(n < max_exact, n, large)


def _near_bias(rel_bias):
    sl = jnp.arange(KEY_TILE, dtype=jnp.int32)[None, :, None]
    ql = jnp.arange(KEY_TILE, dtype=jnp.int32)[None, None, :]
    off = jnp.array([-KEY_TILE, 0], jnp.int32)[:, None, None]
    rel = sl + off - ql
    far = _t5_bucket(jnp.array(-2 * KEY_TILE, jnp.int32))
    tab = (rel_bias[_t5_bucket(rel)] - rel_bias[far]) * LOG2E
    return jnp.transpose(tab, (3, 0, 1, 2)).astype(F32)


def _seq_context(prev, seq):
    b, _, c = prev.shape
    z = jnp.zeros((b, seq, c), prev.dtype)
    pm1 = z.at[:, 0].set(prev[:, 1])
    pm2 = z.at[:, 0].set(prev[:, 0]).at[:, 1].set(prev[:, 1])
    return pm1.reshape(b * seq, c), pm2.reshape(b * seq, c)


def _pad_keys(cache, new, t_pad):
    b, p, c = cache.shape
    t = new.shape[1]
    pad = jnp.zeros((b, t_pad - p - t, c), BF16)
    return jnp.concatenate([cache.astype(BF16), new.astype(BF16), pad], axis=1)


def _forget_slot_maps(d):
    s_q = np.zeros((3 * LANES, d), np.float32)
    s_k = np.zeros((3 * LANES, d), np.float32)
    c_q = np.zeros((1, d), np.float32)
    c_k = np.zeros((1, d), np.float32)
    for h in range(H_A):
        base = h * HEAD_DIM
        for p in range(3):
            s_q[p * LANES + h, base + 3 + p] = 1.0
            s_k[p * LANES + h, base + p] = -1.0
            c_q[0, base + p] = 1.0
            c_k[0, base + 3 + p] = 1.0
    return jnp.asarray(s_q, BF16), jnp.asarray(s_k, BF16), jnp.asarray(c_q), jnp.asarray(c_k)


def _split3(x):
    def top(v):
        bits = lax.bitcast_convert_type(v, jnp.uint32) & jnp.uint32(0xFFFF0000)
        return lax.bitcast_convert_type(bits, F32)

    hi = top(x)
    mid = top(x - hi)
    lo = x - hi - mid
    return [hi.astype(BF16), mid.astype(BF16), lo.astype(BF16)]


def _feature_major(a, t_pad):
    a = jnp.swapaxes(a, 1, 2)
    return jnp.pad(a, ((0, 0), (0, 0), (0, t_pad - a.shape[2])))


def _key_tiles_feature_major(a, tk):
    b, t, c = a.shape
    return jnp.swapaxes(a.reshape(b, t // tk, tk, c), 2, 3)


def kernel(x_prompt, x_sample, mem_prompt, cache_k_A, cache_v_A, cache_logf_A, cache_k_B, cache_v_B,
           cache_kidx_B, state_conv_C, state_ffconv, cache_mem_k, cache_mem_v, g_mix, g_xa, g_ffn,
           g_mem, g_final, w_in_A, b_f_A, w_out_A, w_in_B, w_out_B, rel_bias, w_in_C, w_conv_C,
           w_out_C, w_q_xa, w_k_xa, w_v_xa, w_o_xa, w_up, w_conv_ff, b_conv_ff, w_down):
    bp, tp, d = x_prompt.shape
    bs, ts, _ = x_sample.shape
    depth = g_mix.shape[0]
    past = cache_k_A.shape[2]
    n_mem = mem_prompt.shape[1]
    assert d == H_A * HEAD_DIM == H_B * HEAD_DIM and ts % SUBLANES == 0 and ts <= KEY_TILE
    assert past % KEY_TILE == 0 and tp % KEY_TILE == 0
    mixers = tuple("ABC"[l % 3] for l in range(depth))
    slot = tuple(mixers[:l].count(mixers[l]) for l in range(depth))
    mp, ms = bp * tp, bs * ts
    tm_p = min(256, tp)
    tm_s = min(ms, 8 * ts)
    tq_a = min(512, tp)
    tk_a = min(256, tp)
    t_all = past + ts
    t_pad = -(-t_all // KEY_TILE) * KEY_TILE
    k_top_p = min(TOPK_MAX, tp // 4)
    k_top_s = min(TOPK_MAX, t_all // 4)
    nb_s = min(4, bs)

    bf = lambda a: a.astype(BF16)
    row = lambda a: a.reshape(1, -1)
    xp = x_prompt.reshape(mp, d)
    xs = x_sample.reshape(ms, d)
    mem = mem_prompt.reshape(bp * n_mem, d)
    bias_near = _near_bias(rel_bias)

    outs = {n: [] for n in ("kA_p", "vA_p", "fA_p", "kB_p", "vB_p", "iB_p", "cC_p", "ff_p", "mk_p",
                            "mv_p", "kA_s", "vA_s", "fA_s", "kB_s", "vB_s", "iB_s", "cC_s", "ff_s")}

    slot_q, slot_k, ones_q, ones_k = _forget_slot_maps(d)

    def fox_prompt(x, g, j):
        w = w_in_A[j]
        w_f = jnp.pad(w[:, 3 * d:], ((0, 0), (0, LANES - H_A)))
        b_f = jnp.pad(b_f_A[j], (0, LANES - H_A)).reshape(1, LANES)
        k, v, lf, qa, ka, vt = _a_in_seq(
            x, g, bf(w[:, :d]), bf(w[:, d:2 * d]), bf(w[:, 2 * d:3 * d]),
            bf(w[:, 2 * d:3 * d].T), bf(w_f), b_f, slot_q, slot_k, ones_q, ones_k,
            tm=tk_a, t_len=tp)
        o = _fox_attention(qa, ka, vt, b=bp, tq=tq_a, tk=tk_a, q_off=0)
        return o, k, v, lf

    def fox_sample(x, g, j, ck, cv, clf):
        b, t, nl = bs, ts, H_A * ts
        assert nl % LANES == 0 and 3 * H_A + 3 <= LANES
        q, k, v, kb, vb, lf = _a_in(x, g, bf(w_in_A[j][:, :3 * d]), bf(w_in_A[j][:, 3 * d:]),
                                    row(b_f_A[j]), min(256, ms))
        t_f = past + LANES
        lf_t = jnp.swapaxes(lf.reshape(b, t, H_A), 1, 2)
        lf_all = jnp.concatenate([jnp.swapaxes(clf, 1, 2).astype(F32), lf_t,
                                  jnp.zeros((b, H_A, LANES - t), F32)], axis=2)
        f_all = _cumsum_lanes(lf_all.reshape(b * H_A, t_f), min(64, b * H_A)).reshape(b, H_A, t_f)
        f3 = _split3(f_all * LOG2E)
        spare = LANES - 3 * H_A - 3
        kslot = jnp.concatenate([jnp.swapaxes(p, 1, 2) for p in f3]
                                + [jnp.ones((b, t_f, 3), BF16), jnp.zeros((b, t_f, spare), BF16)],
                                axis=-1)
        eye = jnp.eye(H_A, dtype=BF16)
        own = jnp.broadcast_to(-jnp.repeat(eye, t, axis=1), (b, H_A, nl))
        fq3 = [p[:, :, past:past + t].reshape(b, 1, nl) for p in f3]
        q_slot = jnp.concatenate([own] * 3 + fq3 + [jnp.zeros((b, spare, nl), BF16)], axis=1)
        q_feat = jnp.einsum("bthd,hg->bhdgt", q.reshape(b, t, H_A, HEAD_DIM), eye)
        qbd = jnp.concatenate([q_feat.reshape(b, d, nl), q_slot], axis=1)
        rows128 = lambda a: jnp.pad(a.reshape(b, t, d), ((0, 0), (0, LANES - t), (0, 0)))
        o = _fox_cache_attention(ck.reshape(b, past, d), cv.reshape(b, past, d), kslot,
                                 rows128(kb), rows128(vb), qbd, ts=t)
        return o.reshape(ms, d), k, v, lf

    def dsa(x, g, j, b, t, cache):
        m = b * t
        w = w_in_B[j]
        dkv = HKV_B * HEAD_DIM
        c0, c2, c3 = d, d + 2 * dkv, d + 2 * dkv + H_IDX * D_IDX
        q, k, v, kb, vb, qi, ki, kib, wi, vt = _b_in(
            x, g, bf(w[:, :c0]), bf(w[:, c0:c2]), bf(w[:, c2:c3]),
            bf(jnp.concatenate([w[:, c3:c3 + D_IDX]] * 2, axis=1)), bf(w[:, c3 + D_IDX:]),
            bf(w[:, c0 + dkv:c2].T), min(256, m))
        if cache is None:
            q_off, k_top, tk_valid, tq_pad = 0, k_top_p, t, t
            k_all, ki_all = kb.reshape(b, t, dkv), kib.reshape(b, t, -1)
        else:
            ck, cv, cki = cache
            q_off, k_top, tk_valid, tq_pad = past, k_top_s, t_all, KEY_TILE
            k_all = _pad_keys(ck.reshape(b, past, dkv), kb.reshape(b, t, dkv), t_pad)
            v_all = _pad_keys(cv.reshape(b, past, dkv), vb.reshape(b, t, dkv), t_pad)
            vt = _key_tiles_feature_major(v_all, KEY_TILE).reshape(-1, dkv, KEY_TILE)
            cki2 = jnp.concatenate([cki, cki], axis=-1)
            ki_all = _pad_keys(cki2, kib.reshape(b, t, -1), t_pad)
        rows = lambda a: jnp.pad(a.reshape(b, t, -1), ((0, 0), (0, tq_pad - t), (0, 0)))
        o = _dsa_attention(rows(q), rows(qi), _feature_major(wi.reshape(b, t, H_IDX), tq_pad),
                           k_all, vt, ki_all, bias_near, q_off=q_off, k_top=k_top,
                           tk_valid=tk_valid)
        return o[:, :t].reshape(m, d), k, v, ki

    for l in range(depth):
        mix, j = mixers[l], slot[l]
        gm, gx, gf = row(g_mix[l]), row(g_xa[l]), row(g_ffn[l])
        mk, mv = _proj(mem, row(g_mem[l]), [bf(w_k_xa[l]), bf(w_v_xa[l])], min(256, bp * n_mem))
        outs["mk_p"].append(mk.reshape(bp, n_mem, XA_HEADS, d // XA_HEADS))
        outs["mv_p"].append(mv.reshape(bp, n_mem, XA_HEADS, d // XA_HEADS))
        mks, mvs = cache_mem_k[l].reshape(bs, n_mem, d), cache_mem_v[l].reshape(bs, n_mem, d)
        post = dict(g=gx, w_q=bf(w_q_xa[l]), w_o=bf(w_o_xa[l]))

        if mix == "A":
            ap, k, v, f = fox_prompt(xp, gm, j)
            outs["kA_p"].append(k.reshape(bp, tp, H_A, HEAD_DIM))
            outs["vA_p"].append(v.reshape(bp, tp, H_A, HEAD_DIM))
            outs["fA_p"].append(f.reshape(bp, tp, H_A))
            a_s, k, v, f = fox_sample(xs, gm, j, cache_k_A[j], cache_v_A[j], cache_logf_A[j])
            outs["kA_s"].append(k.reshape(bs, ts, H_A, HEAD_DIM))
            outs["vA_s"].append(v.reshape(bs, ts, H_A, HEAD_DIM))
            outs["fA_s"].append(f.reshape(bs, ts, H_A))
            w_mix = bf(w_out_A[j])
        elif mix == "B":
            ap, k, v, ki = dsa(xp, gm, j, bp, tp, None)
            outs["kB_p"].append(k.reshape(bp, tp, HKV_B, HEAD_DIM))
            outs["vB_p"].append(v.reshape(bp, tp, HKV_B, HEAD_DIM))
            outs["iB_p"].append(ki.reshape(bp, tp, D_IDX))
            a_s, k, v, ki = dsa(xs, gm, j, bs, ts, (cache_k_B[j], cache_v_B[j], cache_kidx_B[j]))
            outs["kB_s"].append(k.reshape(bs, ts, HKV_B, HEAD_DIM))
            outs["vB_s"].append(v.reshape(bs, ts, HKV_B, HEAD_DIM))
            outs["iB_s"].append(ki.reshape(bs, ts, D_IDX))
            w_mix = bf(w_out_B[j])
        else:
            xp, tail = _sconv_stream(xp, gm, bf(w_in_C[j]), w_conv_C[j], bf(w_out_C[j]),
                                     tm=tm_p, t_len=tp)
            outs["cC_p"].append(tail[:, SUBLANES - 2:, :])
            pm1, pm2 = _seq_context(state_conv_C[j], ts)
            xs, p_all = _sconv_seq(xs, gm, bf(w_in_C[j]), w_conv_C[j], bf(w_out_C[j]), pm1, pm2,
                                   tm=tm_s, seq=ts)
            outs["cC_s"].append(p_all.reshape(bs, ts, d)[:, ts - 2:, :])
            ap = a_s = w_mix = None

        xp = _post(xp, ap, w_mix, mk=mk.reshape(bp, n_mem, d), mv=mv.reshape(bp, n_mem, d),
                   tm=tm_p, nb=1, tiles_per_mem=tp // tm_p, **post)
        xs = _post(xs, a_s, w_mix, mk=mks, mv=mvs, tm=nb_s * ts, nb=nb_s, tiles_per_mem=1, **post)

        wu, wc, bc, wd = bf(w_up[l]), w_conv_ff[l], row(b_conv_ff[l]), bf(w_down[l])
        xp, tail = _ffn_stream(xp, gf, wu, wc, bc, wd, tm=tm_p, t_len=tp)
        outs["ff_p"].append(tail[:, SUBLANES - 2:, :])
        pm1, pm2 = _seq_context(state_ffconv[l], ts)
        xs, up_all = _ffn_seq(xs, gf, wu, wc, bc, wd, pm1, pm2, tm=tm_s, seq=ts)
        outs["ff_s"].append(up_all.reshape(bs, ts, -1)[:, ts - 2:, :])

    y_p = _final_norm(xp, row(g_final), tm_p).reshape(bp, tp, d)
    y_s = _final_norm(xs, row(g_final), min(256, ms)).reshape(bs, ts, d)
    st = lambda n: jnp.stack(outs[n])
    return (y_p, y_s, st("kA_p"), st("vA_p"), st("fA_p"), st("kB_p"), st("vB_p"), st("iB_p"),
            st("cC_p"), st("ff_p"), st("mk_p"), st("mv_p"), st("kA_s"), st("vA_s"), st("fA_s"),
            st("kB_s"), st("vB_s"), st("iB_s"), st("cC_s"), st("ff_s"))
```

```python
import functools
import math

import jax
import jax.numpy as jnp
import numpy as np
from jax import lax
from jax.experimental import pallas as pl
from jax.experimental.pallas import tpu as pltpu

F32, BF16, I32 = jnp.float32, jnp.bfloat16, jnp.int32

CHUNK = 64
H_A = 16
H_B = 16
HKV_B = 4
G_B = H_B // HKV_B
H_IDX = 8
D_IDX = 64
TOPK_MAX = 256
NUM_BUCKETS = 32
MAX_DISTANCE = 128
XA_HEADS = 4
EPS = 1e-6

LANES = 128
SUBLANES = 8
HEAD_DIM = 64
KEY_TILE = 128
VMEM_LIMIT_BYTES = 56 * 1024 * 1024

INT_MIN = -2 ** 31
NEG_INF = float("-inf")
LOG2E = 1.0 / math.log(2.0)
Q_SCALE = HEAD_DIM ** -0.5 * LOG2E

_WHOLE = pl.BlockSpec(memory_space=pltpu.VMEM)


def _cp(*sem):
    return pltpu.CompilerParams(dimension_semantics=sem, vmem_limit_bytes=VMEM_LIMIT_BYTES)


def _rows(tm, n):
    return pl.BlockSpec((tm, n), lambda i: (i, 0))


def _rms(x, g):
    return x * lax.rsqrt(jnp.mean(x * x, axis=-1, keepdims=True) + EPS) * g


def _mm(a, b):
    return jnp.dot(a, b, preferred_element_type=F32)


def _mm_nt(a, b):
    return lax.dot_general(a, b, (((1,), (1,)), ((), ())), preferred_element_type=F32)


def _proj_kernel(x_ref, g_ref, *refs, n_w):
    w_refs, o_refs = refs[:n_w], refs[n_w:]
    h = _rms(x_ref[...], g_ref[...]).astype(BF16)
    for w_ref, o_ref in zip(w_refs, o_refs):
        o_ref[...] = _mm(h, w_ref[...])


def _proj(x, g, ws, tm):
    m, d = x.shape
    return pl.pallas_call(
        functools.partial(_proj_kernel, n_w=len(ws)),
        grid=(m // tm,),
        in_specs=[_rows(tm, d), _WHOLE] + [_WHOLE] * len(ws),
        out_specs=[_rows(tm, w.shape[1]) for w in ws],
        out_shape=[jax.ShapeDtypeStruct((m, w.shape[1]), F32) for w in ws],
        compiler_params=_cp("parallel"),
    )(x, g, *ws)


def _a_in_kernel(x_ref, g_ref, w_ref, wf_ref, bf_ref, q_ref, k_ref, v_ref, kb_ref, vb_ref, lf_ref):
    d = q_ref.shape[-1]
    h = _rms(x_ref[...], g_ref[...]).astype(BF16)
    q_ref[...] = (_mm(h, w_ref[:, 0:d]) * Q_SCALE).astype(BF16)
    k = _mm(h, w_ref[:, d:2 * d])
    k_ref[...] = k
    kb_ref[...] = k.astype(BF16)
    v = _mm(h, w_ref[:, 2 * d:3 * d])
    v_ref[...] = v
    vb_ref[...] = v.astype(BF16)
    fl = _mm(h, wf_ref[...]) + bf_ref[...]
    lf_ref[...] = jnp.minimum(fl, 0.0) - jnp.log1p(jnp.exp(-jnp.abs(fl)))


def _a_in(x, g, w_qkv, w_f, b_f, tm):
    m, d = x.shape
    sds = jax.ShapeDtypeStruct
    return pl.pallas_call(
        _a_in_kernel,
        grid=(m // tm,),
        in_specs=[_rows(tm, d), _WHOLE, _WHOLE, _WHOLE, _WHOLE],
        out_specs=[_rows(tm, d)] * 5 + [_rows(tm, H_A)],
        out_shape=[sds((m, d), BF16), sds((m, d), F32), sds((m, d), F32),
                   sds((m, d), BF16), sds((m, d), BF16), sds((m, H_A), F32)],
        compiler_params=_cp("parallel"),
    )(x, g, w_qkv, w_f, b_f)


def _top_bits(v):
    bits = lax.bitcast_convert_type(v, I32) & jnp.int32(-65536)
    return lax.bitcast_convert_type(bits, F32)


def _a_in_seq_kernel(x_ref, g_ref, wq_ref, wk_ref, wv_ref, wvt_ref, wf_ref, bf_ref, sq_ref, sk_ref,
                     cq_ref, ck_ref, k_ref, v_ref, lf_ref, qa_ref, ka_ref, vt_ref, carry_scr, *, tpb):
    tm = x_ref.shape[0]
    h = _rms(x_ref[...], g_ref[...]).astype(BF16)
    lane = lax.broadcasted_iota(I32, (tm, LANES), 1)

    @pl.when(pl.program_id(0) % tpb == 0)
    def _():
        carry_scr[...] = jnp.zeros(carry_scr.shape, F32)

    fl = _mm(h, wf_ref[...]) + bf_ref[...]
    lf = jnp.minimum(fl, 0.0) - jnp.log1p(jnp.exp(-jnp.abs(fl)))
    lf = jnp.where(lane < H_A, lf, 0.0)
    lf_ref[...] = lf[:, 0:H_A]
    ra = lax.broadcasted_iota(I32, (tm, tm), 0)
    rb = lax.broadcasted_iota(I32, (tm, tm), 1)
    tri = (rb <= ra).astype(F32)
    f_cum = jnp.dot(tri, lf, precision=lax.Precision.HIGHEST,
                    preferred_element_type=F32) + carry_scr[0:1, :]
    carry_scr[...] = jnp.broadcast_to(f_cum[tm - 1:tm, :], carry_scr.shape)
    f2 = f_cum * LOG2E
    hi = _top_bits(f2)
    mid = _top_bits(f2 - hi)
    lo = f2 - hi - mid
    pieces = jnp.concatenate([hi, mid, lo], axis=1).astype(BF16)

    q = _mm(h, wq_ref[...]) * Q_SCALE
    k = _mm(h, wk_ref[...])
    k_ref[...] = k
    slots_q = _mm(pieces, sq_ref[...]) + cq_ref[...]
    slots_k = _mm(pieces, sk_ref[...]) + ck_ref[...]
    for feat, slots, out_ref in ((q, slots_q, qa_ref), (k, slots_k, ka_ref)):
        for hd in range(H_A):
            pair = slice((hd // 2) * LANES, (hd // 2 + 1) * LANES)
            a, b = feat[:, pair], slots[:, pair]
            if hd % 2 == 0:
                b = pltpu.roll(b, HEAD_DIM, 1)
            else:
                a = pltpu.roll(a, HEAD_DIM, 1)
            out_ref[:, hd * LANES:(hd + 1) * LANES] = jnp.where(lane < HEAD_DIM, a, b).astype(BF16)
    v_ref[...] = _mm(h, wv_ref[...])
    vt_ref[0] = _mm_nt(wvt_ref[...], h).astype(BF16)


def _a_in_seq(x, g, w_q, w_k, w_v, w_vt, w_f, b_f, s_q, s_k, c_q, c_k, *, tm, t_len):
    m, d = x.shape
    sds = jax.ShapeDtypeStruct
    kern = functools.partial(_a_in_seq_kernel, tpb=t_len // tm)
    return pl.pallas_call(
        kern,
        grid=(m // tm,),
        in_specs=[_rows(tm, d)] + [_WHOLE] * 11,
        out_specs=[_rows(tm, d), _rows(tm, d), _rows(tm, H_A), _rows(tm, 2 * d), _rows(tm, 2 * d),
                   pl.BlockSpec((1, d, tm), lambda i: (i, 0, 0))],
        out_shape=[sds((m, d), F32), sds((m, d), F32), sds((m, H_A), F32),
                   sds((m, 2 * d), BF16), sds((m, 2 * d), BF16), sds((m // tm, d, tm), BF16)],
        scratch_shapes=[pltpu.VMEM((SUBLANES, LANES), F32)],
        compiler_params=_cp("arbitrary"),
    )(x, g, w_q, w_k, w_v, w_vt, w_f, b_f, s_q, s_k, c_q, c_k)


def _b_in_kernel(x_ref, g_ref, wq_ref, wkv_ref, wqi_ref, wki_ref, wwi_ref, wvt_ref,
                 q_ref, k_ref, v_ref, kb_ref, vb_ref, qi_ref, ki_ref, kib_ref, wi_ref, vt_ref):
    dkv = k_ref.shape[-1]
    h = _rms(x_ref[...], g_ref[...]).astype(BF16)
    n_vt = x_ref.shape[0] // KEY_TILE
    if n_vt == 0:
        vt_ref[...] = jnp.zeros(vt_ref.shape, BF16)
    for c in range(n_vt):
        vt_ref[c] = _mm_nt(wvt_ref[...], h[c * KEY_TILE:(c + 1) * KEY_TILE, :]).astype(BF16)
    q_ref[...] = (_mm(h, wq_ref[...]) * Q_SCALE).astype(BF16)
    kv = _mm(h, wkv_ref[...])
    k_ref[...] = kv[:, 0:dkv]
    kb_ref[...] = kv[:, 0:dkv].astype(BF16)
    v_ref[...] = kv[:, dkv:2 * dkv]
    vb_ref[...] = kv[:, dkv:2 * dkv].astype(BF16)
    qi_ref[...] = (_mm(h, wqi_ref[...]) * (D_IDX ** -0.5)).astype(BF16)
    ki2 = _mm(h, wki_ref[...])
    ki_ref[...] = ki2[:, 0:D_IDX]
    kib_ref[...] = ki2.astype(BF16)
    wi_ref[...] = _mm(h, wwi_ref[...]) * (H_IDX ** -0.5)


def _b_in(x, g, w_q, w_kv, w_qi, w_ki2, w_wi, w_vt, tm):
    m, d = x.shape
    dkv = w_kv.shape[1] // 2
    dqi = w_qi.shape[1]
    n_vt = tm // KEY_TILE
    sds = jax.ShapeDtypeStruct
    return pl.pallas_call(
        _b_in_kernel,
        grid=(m // tm,),
        in_specs=[_rows(tm, d)] + [_WHOLE] * 7,
        out_specs=[_rows(tm, d), _rows(tm, dkv), _rows(tm, dkv), _rows(tm, dkv), _rows(tm, dkv),
                   _rows(tm, dqi), _rows(tm, D_IDX), _rows(tm, 2 * D_IDX), _rows(tm, H_IDX),
                   pl.BlockSpec((max(n_vt, 1), dkv, KEY_TILE), lambda i: (i, 0, 0))],
        out_shape=[sds((m, d), BF16), sds((m, dkv), F32), sds((m, dkv), F32),
                   sds((m, dkv), BF16), sds((m, dkv), BF16), sds((m, dqi), BF16),
                   sds((m, D_IDX), F32), sds((m, 2 * D_IDX), BF16), sds((m, H_IDX), F32),
                   sds((max(n_vt, 1) * (m // tm), dkv, KEY_TILE), BF16)],
        compiler_params=_cp("parallel"),
    )(x, g, w_q, w_kv, w_qi, w_ki2, w_wi, w_vt)


def _cumsum_kernel(x_ref, o_ref):
    n = x_ref.shape[-1]
    a = lax.broadcasted_iota(I32, (LANES, LANES), 0)
    b = lax.broadcasted_iota(I32, (LANES, LANES), 1)
    tri = (a <= b).astype(F32)
    carry = jnp.zeros((x_ref.shape[0], 1), F32)
    for c in range(n // LANES):
        sl = slice(c * LANES, (c + 1) * LANES)
        y = jnp.dot(x_ref[:, sl], tri, precision=lax.Precision.HIGHEST,
                    preferred_element_type=F32) + carry
        o_ref[:, sl] = y
        carry = y[:, LANES - 1:LANES]


def _cumsum_lanes(x, rb):
    r, n = x.shape
    return pl.pallas_call(
        _cumsum_kernel,
        grid=(r // rb,),
        in_specs=[_rows(rb, n)],
        out_specs=_rows(rb, n),
        out_shape=jax.ShapeDtypeStruct((r, n), F32),
        compiler_params=_cp("parallel"),
    )(x)


def _lookahead(units, depth):
    pending = []
    for idx in range(len(units) + depth):
        if idx < len(units):
            pending.append(units[idx][0]())
        if idx >= depth:
            units[idx - depth][1](pending[idx - depth])


def _fox_kernel(qa_ref, ka_ref, vt_ref, o_ref, qt_scr, m_scr, l_scr, acc_scr,
                *, tq, tk, tu, tpi, q_off):
    i = pl.program_id(2)
    for e in range(2):
        qt_scr[e] = qa_ref[:, e * LANES:(e + 1) * LANES].astype(F32).T.astype(BF16)
    m_scr[...] = jnp.full(m_scr.shape, NEG_INF, F32)
    l_scr[...] = jnp.zeros(l_scr.shape, F32)
    acc_scr[...] = jnp.zeros(acc_scr.shape, F32)
    q0 = q_off + i * tq

    def unit(kk, e, c, masked):
        cols = slice(c * tu, (c + 1) * tu)
        slab = slice(e * LANES, (e + 1) * LANES)

        def issue():
            return _mm(ka_ref[pl.ds(kk * tk, tk), slab], qt_scr[e, :, cols])

        def consume(s):
            if masked:
                kpos = kk * tk + lax.broadcasted_iota(I32, (tk, 1), 0)
                qpos = q0 + c * tu + lax.broadcasted_iota(I32, (1, tu), 1)
                s = jnp.where(kpos <= qpos, s, NEG_INF)
            m_old = m_scr[e, :, cols]
            m_new = jnp.maximum(m_old, jnp.max(s, axis=0, keepdims=True))
            alpha = jnp.exp2(m_old - m_new)
            p = jnp.exp2(s - m_new)
            l_scr[e, :, cols] = alpha * l_scr[e, :, cols] + jnp.sum(p, axis=0, keepdims=True)
            m_scr[e, :, cols] = m_new
            ve = vt_ref[kk, e * HEAD_DIM:(e + 1) * HEAD_DIM, :]
            acc_scr[e, :, cols] = acc_scr[e, :, cols] * alpha + _mm(ve, p.astype(BF16))

        return issue, consume

    def full_body(kp, carry):
        _lookahead([unit(tpi * kp + t, e, c, False)
                    for t in range(tpi) for c in range(tq // tu) for e in range(2)], 2)
        return carry

    assert tpi % 2 == 0 and q_off % (2 * tk) == 0 and tq % tk == 0 and tq % tu == 0
    assert tq // tk == 1 or tq % (2 * tk) == 0
    n_full = q0 // tk
    trips = n_full // tpi
    lax.fori_loop(0, trips, full_body, 0)
    for r in range(0, tpi, 2):
        @pl.when(n_full - trips * tpi == r)
        def _():
            tail = [unit(trips * tpi + t, e, c, False)
                    for t in range(r) for c in range(tq // tu) for e in range(2)]
            for t in range(tq // tk):
                for c in range(tq // tu):
                    if t * tk > (c + 1) * tu - 1:
                        continue
                    masked = (t + 1) * tk - 1 > c * tu
                    tail += [unit(n_full + t, e, c, masked) for e in range(2)]
            _lookahead(tail, 2)
    ot = jnp.concatenate([acc_scr[e] / l_scr[e] for e in range(2)], axis=0)
    o_ref[...] = ot.T.astype(BF16)


def _fox_attention(qa, ka, vt, *, b, tq, tk, q_off):
    d = qa.shape[1] // 2
    t_q, t_k = qa.shape[0] // b, ka.shape[0] // b
    nq, nkt = t_q // tq, t_k // tk
    tu = tq
    tpi = 8
    kern = functools.partial(_fox_kernel, tq=tq, tk=tk, tu=tu, tpi=tpi, q_off=q_off)
    return pl.pallas_call(
        kern,
        grid=(b, d // LANES, nq),
        in_specs=[pl.BlockSpec((tq, 2 * LANES), lambda bb, j, i: (bb * nq + i, j)),
                  pl.BlockSpec((t_k, 2 * LANES), lambda bb, j, i: (bb, j)),
                  pl.BlockSpec((nkt, LANES, tk), lambda bb, j, i: (bb, j, 0))],
        out_specs=pl.BlockSpec((tq, LANES), lambda bb, j, i: (bb * nq + i, j)),
        out_shape=jax.ShapeDtypeStruct((b * t_q, d), BF16),
        scratch_shapes=[pltpu.VMEM((2, LANES, tq), BF16),
                        pltpu.VMEM((2, 1, tq), F32), pltpu.VMEM((2, 1, tq), F32),
                        pltpu.VMEM((2, HEAD_DIM, tq), F32)],
        compiler_params=_cp("parallel", "parallel", "arbitrary"),
    )(qa, ka, vt)


def _fox_cache_kernel(ck_ref, cv_ref, kslot_ref, knew_ref, vnew_ref, qbd_ref, o_ref, s_scr, of_scr,
                      *, past, ts, kc):
    d = ck_ref.shape[2]
    nl = qbd_ref.shape[2]
    q_feat, q_slot = qbd_ref[0, 0:d, :], qbd_ref[0, d:d + LANES, :]
    for c in range(past // kc):
        rows = slice(c * kc, (c + 1) * kc)
        s_scr[rows, :] = (_mm(ck_ref[0, rows, :].astype(BF16), q_feat)
                          + _mm(kslot_ref[0, rows, :], q_slot))
    s_new = _mm(knew_ref[0], q_feat) + _mm(kslot_ref[0, past:past + LANES, :], q_slot)
    key_j = lax.broadcasted_iota(I32, (LANES, nl), 0)
    query = lax.broadcasted_iota(I32, (LANES, nl), 1) % ts
    s_scr[past:past + LANES, :] = jnp.where(key_j <= query, s_new, NEG_INF)
    s = s_scr[...]
    p = jnp.exp2(s - jnp.max(s, axis=0, keepdims=True))
    p = p / jnp.sum(p, axis=0, keepdims=True)
    pt = p.T.astype(BF16)
    of_scr[...] = _mm(pt[:, past:past + LANES], vnew_ref[0])
    for c in range(past // kc):
        rows = slice(c * kc, (c + 1) * kc)
        of_scr[...] += _mm(pt[:, rows], cv_ref[0, rows, :].astype(BF16))
    for h in range(d // HEAD_DIM):
        cols = slice(h * HEAD_DIM, (h + 1) * HEAD_DIM)
        o_ref[0, :, cols] = of_scr[h * ts:(h + 1) * ts, cols].astype(BF16)


def _fox_cache_attention(ck, cv, kslot, knew, vnew, qbd, *, ts):
    b, past, d = ck.shape
    nl = qbd.shape[2]
    blk = lambda *s: pl.BlockSpec((1,) + s, lambda i: (i, 0, 0))
    return pl.pallas_call(
        functools.partial(_fox_cache_kernel, past=past, ts=ts, kc=min(512, past)),
        grid=(b,),
        in_specs=[blk(past, d), blk(past, d), blk(past + LANES, LANES), blk(LANES, d), blk(LANES, d),
                  blk(d + LANES, nl)],
        out_specs=blk(ts, d),
        out_shape=jax.ShapeDtypeStruct((b, ts, d), BF16),
        scratch_shapes=[pltpu.VMEM((past + LANES, nl), F32), pltpu.VMEM((nl, d), F32)],
        compiler_params=_cp("parallel"),
    )(ck, cv, kslot, knew, vnew, qbd)


def _dsa_kernel(q_ref, qi_ref, wit_ref, k_ref, vt_ref, ki_ref, bias_ref, o_ref,
                keys_scr, mb_scr, qim_scr, qg_scr, m_scr, l_scr, acc_scr,
                *, q_off, k_top, tk_valid, nkt):
    tq = KEY_TILE
    i = pl.program_id(1)
    q0 = q_off + i * tq
    home = q0 // KEY_TILE
    qpos = q0 + lax.broadcasted_iota(I32, (1, tq), 1)
    chunk_shift = int(math.log2(CHUNK))
    qchunk = lax.shift_right_logical(qpos, chunk_shift)
    lane = lax.broadcasted_iota(I32, (tq, LANES), 1)
    kf = float(k_top)

    for h in range(H_IDX):
        pair = qi_ref[0, :, (h // 2) * LANES:(h // 2 + 1) * LANES].astype(F32)
        pair = jnp.where((lane >= HEAD_DIM) == (h % 2 == 1), pair, 0.0)
        qim_scr[h] = pair.T.astype(BF16)
    for g in range(HKV_B):
        for r in range(G_B):
            hq = g * G_B + r
            slab = q_ref[0, :, (hq // 2) * LANES:(hq // 2 + 1) * LANES].astype(F32)
            if hq % 2 != g % 2:
                slab = pltpu.roll(slab, HEAD_DIM, 1)
            slab = jnp.where((lane >= HEAD_DIM) == (g % 2 == 1), slab, 0.0)
            qg_scr[g, :, r * tq:(r + 1) * tq] = slab.T.astype(BF16)

    def score_unit(kt):
        def issue():
            kit = ki_ref[0, pl.ds(jnp.minimum(kt, nkt - 1) * KEY_TILE, KEY_TILE), :]
            return [_mm(kit, qim_scr[h]) for h in range(H_IDX)]

        def consume(dots):
            acc = jnp.zeros((KEY_TILE, tq), F32)
            for h in range(H_IDX):
                acc = acc + jnp.maximum(dots[h], 0.0) * wit_ref[0, h:h + 1, :]
            acc = jnp.where(acc == 0.0, 0.0, acc)
            bits = lax.bitcast_convert_type(acc, I32)
            key = bits ^ (lax.shift_right_arithmetic(bits, 31) & 0x7FFFFFFF)
            kpos = kt * KEY_TILE + lax.broadcasted_iota(I32, (KEY_TILE, 1), 0)
            adm = (lax.shift_right_logical(kpos, chunk_shift) <= qchunk) & (kpos < tk_valid)
            keys_scr[kt] = jnp.where(adm, key, INT_MIN)

        return issue, consume

    def score_body(kp, carry):
        _lookahead([score_unit(2 * kp), score_unit(2 * kp + 1)], 1)
        return carry

    lax.fori_loop(0, (home + 2) // 2, score_body, 0)
    keys_scr[home + 1] = jnp.full((KEY_TILE, tq), INT_MIN, I32)

    def count(cand, strict):
        def body(kp, a):
            for t in range(2):
                key = keys_scr[2 * kp + t]
                hit = (key > cand) if strict else (key >= cand)
                a = jnp.where(hit, a + 1.0, a)
            return a

        a = lax.fori_loop(0, (home + 2) // 2, body, jnp.zeros((KEY_TILE, tq), F32))
        return jnp.sum(a, axis=0, keepdims=True)

    zero = jnp.zeros((1, tq), I32)
    thr = jnp.where(count(zero, False) >= kf, zero, jnp.full((1, tq), INT_MIN, I32))

    def bit_body(b, t):
        cand = t + lax.shift_left(jnp.int32(1), 30 - b)
        return jnp.where(count(cand, False) >= kf, cand, t)

    thr = lax.fori_loop(0, 31, bit_body, thr)

    no_ties = jnp.max(jnp.abs(count(thr, False) - kf)) == 0.0

    @pl.when(no_ties)
    def _():
        def body(kt, carry):
            key = keys_scr[kt]
            mb_scr[kt] = jnp.where((key >= thr) & (key != INT_MIN), 0.0, NEG_INF)
            return carry

        lax.fori_loop(0, home + 1, body, 0)

    @pl.when(jnp.logical_not(no_ties))
    def _():
        need = kf - count(thr, True)
        ra = lax.broadcasted_iota(I32, (KEY_TILE, KEY_TILE), 0)
        rb = lax.broadcasted_iota(I32, (KEY_TILE, KEY_TILE), 1)
        earlier = (rb < ra).astype(BF16)

        def tie_body(kt, seen):
            key = keys_scr[kt]
            eq = key == thr
            eqf = jnp.where(eq, 1.0, 0.0)
            rank = _mm(earlier, eqf.astype(BF16)) + seen
            sel = ((key > thr) | (eq & (rank < need))) & (key != INT_MIN)
            mb_scr[kt] = jnp.where(sel, 0.0, NEG_INF)
            return seen + jnp.sum(eqf, axis=0, keepdims=True)

        lax.fori_loop(0, home + 1, tie_body, jnp.zeros((1, tq), F32))

    m_scr[...] = jnp.full(m_scr.shape, NEG_INF, F32)
    l_scr[...] = jnp.zeros(l_scr.shape, F32)
    acc_scr[...] = jnp.zeros(acc_scr.shape, F32)

    def unit(kt, g, near):
        def issue():
            ks = k_ref[0, pl.ds(kt * KEY_TILE, KEY_TILE), (g // 2) * LANES:(g // 2 + 1) * LANES]
            return _mm(ks, qg_scr[g])

        def consume(st):
            mb = mb_scr[kt]
            vt = vt_ref[kt, g * HEAD_DIM:(g + 1) * HEAD_DIM, :]
            for r in range(G_B):
                hq = g * G_B + r
                s = st[:, r * tq:(r + 1) * tq] + mb
                if near is not None:
                    s = s + bias_ref[hq, near]
                m_old = m_scr[hq]
                m_new = jnp.maximum(m_old, jnp.max(s, axis=0, keepdims=True))
                m_use = jnp.where(m_new == NEG_INF, 0.0, m_new)
                alpha = jnp.exp2(m_old - m_use)
                p = jnp.exp2(s - m_use)
                l_scr[hq] = alpha * l_scr[hq] + jnp.sum(p, axis=0, keepdims=True)
                m_scr[hq] = m_new
                acc_scr[hq] = acc_scr[hq] * alpha + _mm(vt, p.astype(BF16))

        return issue, consume

    def attend(tiles):
        _lookahead([unit(kt, g, near) for kt, near in tiles for g in range(HKV_B)], 2)

    far_tpi = 8

    def far_body(kp, carry):
        attend([(far_tpi * kp + t, None) for t in range(far_tpi)])
        return carry

    n_far = jnp.maximum(home - 1, 0)
    trips = n_far // far_tpi
    lax.fori_loop(0, trips, far_body, 0)
    for r in range(1, far_tpi):
        @pl.when(n_far - trips * far_tpi == r)
        def _():
            attend([(trips * far_tpi + t, None) for t in range(r)])

    @pl.when(home >= 1)
    def _():
        attend([(home - 1, 0), (home, 1)])

    @pl.when(home == 0)
    def _():
        attend([(home, 1)])

    for s in range(H_B // 2):
        ot = jnp.concatenate([acc_scr[2 * s + e] / l_scr[2 * s + e] for e in range(2)], axis=0)
        o_ref[0, :, s * LANES:(s + 1) * LANES] = ot.T.astype(BF16)


def _dsa_attention(q, qi, wit, k, vt, ki, bias, *, q_off, k_top, tk_valid):
    b, t_q, d = q.shape
    t_k = k.shape[1]
    nkt = t_k // KEY_TILE
    tq = KEY_TILE
    kern = functools.partial(_dsa_kernel, q_off=q_off, k_top=k_top, tk_valid=tk_valid, nkt=nkt)
    qblk = lambda n: pl.BlockSpec((1, tq, n), lambda bb, i: (bb, i, 0))
    kblk = lambda n: pl.BlockSpec((1, t_k, n), lambda bb, i: (bb, 0, 0))
    return pl.pallas_call(
        kern,
        grid=(b, t_q // tq),
        in_specs=[qblk(d), qblk(qi.shape[2]),
                  pl.BlockSpec((1, H_IDX, tq), lambda bb, i: (bb, 0, i)), kblk(k.shape[2]),
                  pl.BlockSpec((nkt, vt.shape[1], KEY_TILE), lambda bb, i: (bb, 0, 0)),
                  kblk(ki.shape[2]), _WHOLE],
        out_specs=qblk(d),
        out_shape=jax.ShapeDtypeStruct((b, t_q, d), BF16),
        scratch_shapes=[pltpu.VMEM((nkt + 1, KEY_TILE, tq), I32), pltpu.VMEM((nkt, KEY_TILE, tq), F32),
                        pltpu.VMEM((H_IDX, LANES, tq), BF16),
                        pltpu.VMEM((HKV_B, LANES, G_B * tq), BF16),
                        pltpu.VMEM((H_B, 1, tq), F32), pltpu.VMEM((H_B, 1, tq), F32),
                        pltpu.VMEM((H_B, HEAD_DIM, tq), F32)],
        compiler_params=_cp("parallel", "arbitrary"),
    )(q, qi, wit, k, vt, ki, bias)


def _conv3(u, w_ref, cols, prev):
    tm = u.shape[0]
    row = lax.broadcasted_iota(I32, (tm, 1), 0)
    r1 = pltpu.roll(u, 1, 0)
    r2 = pltpu.roll(u, 2, 0)
    if prev[0] == "stream":
        carry = prev[1]
        c6, c7 = carry[6:7, :], carry[7:8, :]
        um1 = jnp.where(row == 0, c7, r1)
        um2 = jnp.where(row == 0, c6, jnp.where(row == 1, c7, r2))
    else:
        _, seq, pm1, pm2 = prev
        t = row % seq
        um1 = jnp.where(t == 0, pm1, r1)
        um2 = jnp.where(t < 2, pm2, r2)
    return w_ref[0:1, cols] * um2 + w_ref[1:2, cols] * um1 + w_ref[2:3, cols] * u


def _col_chunks(n, width):
    out, c = [], 0
    while c < n:
        out.append((c, min(width, n - c)))
        c += width
    return out


def _ffn_kernel(*refs, mode, seq, tpb, dff, cw):
    if mode == "stream":
        x_ref, g_ref, wup_ref, wc_ref, bc_ref, wdn_ref, o_ref, tail_ref, carry_scr = refs
    else:
        x_ref, g_ref, wup_ref, wc_ref, bc_ref, wdn_ref, pm1_ref, pm2_ref, o_ref, up_ref = refs
    x = x_ref[...]
    tm = x.shape[0]
    h = _rms(x, g_ref[...]).astype(BF16)
    if mode == "stream":
        @pl.when(pl.program_id(0) % tpb == 0)
        def _():
            carry_scr[...] = jnp.zeros(carry_scr.shape, F32)
    acc = [x]

    def chunk(c0, w):
        def issue():
            return [_mm(h, wup_ref[:, base + c0:base + c0 + w]) for base in (0, dff)]

        def consume(ups):
            ys = []
            for base, up in zip((0, dff), ups):
                cols = slice(base + c0, base + c0 + w)
                if mode == "stream":
                    y = _conv3(up, wc_ref, cols, ("stream", carry_scr[:, cols]))
                    carry_scr[:, cols] = up[tm - SUBLANES:tm, :]
                    tail_ref[0, :, cols] = up[tm - SUBLANES:tm, :]
                else:
                    y = _conv3(up, wc_ref, cols, ("seq", seq, pm1_ref[:, cols], pm2_ref[:, cols]))
                    up_ref[:, cols] = up
                ys.append(y + bc_ref[:, cols])
            gate, val = ys
            act = (gate / (1.0 + jnp.exp(-gate))) * val
            acc[0] = acc[0] + _mm(act.astype(BF16), wdn_ref[c0:c0 + w, :])

        return issue, consume

    _lookahead([chunk(c0, w) for c0, w in _col_chunks(dff, cw)], 2)
    o_ref[...] = acc[0]


def _ffn_stream(x, g, w_up, w_conv, b_conv, w_down, *, tm, t_len):
    m, d = x.shape
    c2 = w_up.shape[1]
    tpb = t_len // tm
    kern = functools.partial(_ffn_kernel, mode="stream", seq=None, tpb=tpb, dff=c2 // 2, cw=256)
    return pl.pallas_call(
        kern,
        grid=(m // tm,),
        in_specs=[_rows(tm, d)] + [_WHOLE] * 5,
        out_specs=[_rows(tm, d), pl.BlockSpec((1, SUBLANES, c2), lambda i: (i // tpb, 0, 0))],
        out_shape=[jax.ShapeDtypeStruct((m, d), F32),
                   jax.ShapeDtypeStruct((m // t_len, SUBLANES, c2), F32)],
        scratch_shapes=[pltpu.VMEM((SUBLANES, c2), F32)],
        compiler_params=_cp("arbitrary"),
    )(x, g, w_up, w_conv, b_conv, w_down)


def _ffn_seq(x, g, w_up, w_conv, b_conv, w_down, pm1, pm2, *, tm, seq):
    m, d = x.shape
    c2 = w_up.shape[1]
    kern = functools.partial(_ffn_kernel, mode="seq", seq=seq, tpb=None, dff=c2 // 2, cw=512)
    return pl.pallas_call(
        kern,
        grid=(m // tm,),
        in_specs=[_rows(tm, d)] + [_WHOLE] * 5 + [_rows(tm, c2), _rows(tm, c2)],
        out_specs=[_rows(tm, d), _rows(tm, c2)],
        out_shape=[jax.ShapeDtypeStruct((m, d), F32), jax.ShapeDtypeStruct((m, c2), F32)],
        compiler_params=_cp("parallel"),
    )(x, g, w_up, w_conv, b_conv, w_down, pm1, pm2)


def _sconv_kernel(*refs, mode, seq, tpb, cw):
    if mode == "stream":
        x_ref, g_ref, win_ref, wc_ref, wout_ref, o_ref, tail_ref, carry_scr = refs
    else:
        x_ref, g_ref, win_ref, wc_ref, wout_ref, pm1_ref, pm2_ref, o_ref, p_ref = refs
    x = x_ref[...]
    tm, d = x.shape
    h = _rms(x, g_ref[...]).astype(BF16)
    if mode == "stream":
        @pl.when(pl.program_id(0) % tpb == 0)
        def _():
            carry_scr[...] = jnp.zeros(carry_scr.shape, F32)
    acc = [x]

    def chunk(c0, w):
        cols = slice(c0, c0 + w)

        def issue():
            return [_mm(h, win_ref[:, base + c0:base + c0 + w]) for base in (0, d, 2 * d)]

        def consume(zs):
            gb, gc, u = zs
            p = gc * u
            if mode == "stream":
                y = _conv3(p, wc_ref, cols, ("stream", carry_scr[:, cols]))
                carry_scr[:, cols] = p[tm - SUBLANES:tm, :]
                tail_ref[0, :, cols] = p[tm - SUBLANES:tm, :]
            else:
                y = _conv3(p, wc_ref, cols, ("seq", seq, pm1_ref[:, cols], pm2_ref[:, cols]))
                p_ref[:, cols] = p
            acc[0] = acc[0] + _mm((gb * y).astype(BF16), wout_ref[c0:c0 + w, :])

        return issue, consume

    _lookahead([chunk(c0, w) for c0, w in _col_chunks(d, cw)], 1)
    o_ref[...] = acc[0]


def _sconv_stream(x, g, w_in, w_conv, w_out, *, tm, t_len):
    m, d = x.shape
    tpb = t_len // tm
    kern = functools.partial(_sconv_kernel, mode="stream", seq=None, tpb=tpb, cw=256)
    return pl.pallas_call(
        kern,
        grid=(m // tm,),
        in_specs=[_rows(tm, d)] + [_WHOLE] * 4,
        out_specs=[_rows(tm, d), pl.BlockSpec((1, SUBLANES, d), lambda i: (i // tpb, 0, 0))],
        out_shape=[jax.ShapeDtypeStruct((m, d), F32),
                   jax.ShapeDtypeStruct((m // t_len, SUBLANES, d), F32)],
        scratch_shapes=[pltpu.VMEM((SUBLANES, d), F32)],
        compiler_params=_cp("arbitrary"),
    )(x, g, w_in, w_conv, w_out)


def _sconv_seq(x, g, w_in, w_conv, w_out, pm1, pm2, *, tm, seq):
    m, d = x.shape
    kern = functools.partial(_sconv_kernel, mode="seq", seq=seq, tpb=None, cw=512)
    return pl.pallas_call(
        kern,
        grid=(m // tm,),
        in_specs=[_rows(tm, d)] + [_WHOLE] * 4 + [_rows(tm, d), _rows(tm, d)],
        out_specs=[_rows(tm, d), _rows(tm, d)],
        out_shape=[jax.ShapeDtypeStruct((m, d), F32), jax.ShapeDtypeStruct((m, d), F32)],
        compiler_params=_cp("parallel"),
    )(x, g, w_in, w_conv, w_out, pm1, pm2)


def _post_kernel(*refs, nb, has_mix):
    if has_mix:
        x_ref, a_ref, wmix_ref, g_ref, wq_ref, mk_ref, mv_ref, wo_ref, o_ref, oc_scr = refs
        x = x_ref[...] + _mm(a_ref[...], wmix_ref[...])
    else:
        x_ref, g_ref, wq_ref, mk_ref, mv_ref, wo_ref, o_ref, oc_scr = refs
        x = x_ref[...]
    tm, d = x.shape
    dh = d // XA_HEADS
    rpb = tm // nb
    h = _rms(x, g_ref[...]).astype(BF16)
    q = (_mm(h, wq_ref[...]) * (dh ** -0.5)).astype(BF16)
    for b in range(nb):
        rows = slice(b * rpb, (b + 1) * rpb)
        for hh in range(XA_HEADS):
            cols = slice(hh * dh, (hh + 1) * dh)
            s = _mm_nt(q[rows, cols], mk_ref[b, :, cols].astype(BF16))
            e = jnp.exp(s - jnp.max(s, axis=1, keepdims=True))
            p = e / jnp.sum(e, axis=1, keepdims=True)
            oc_scr[rows, cols] = _mm(p.astype(BF16), mv_ref[b, :, cols].astype(BF16)).astype(BF16)
    o_ref[...] = x + _mm(oc_scr[...], wo_ref[...])


def _post(x, a, w_mix, g, w_q, mk, mv, w_o, *, tm, nb, tiles_per_mem):
    m, d = x.shape
    n_mem = mk.shape[1]
    has_mix = a is not None
    mem_spec = pl.BlockSpec((nb, n_mem, d), lambda i: (i // tiles_per_mem, 0, 0))
    ins = [x] + ([a, w_mix] if has_mix else []) + [g, w_q, mk, mv, w_o]
    specs = ([_rows(tm, d)] + ([_rows(tm, d), _WHOLE] if has_mix else [])
             + [_WHOLE, _WHOLE, mem_spec, mem_spec, _WHOLE])
    return pl.pallas_call(
        functools.partial(_post_kernel, nb=nb, has_mix=has_mix),
        grid=(m // tm,),
        in_specs=specs,
        out_specs=_rows(tm, d),
        out_shape=jax.ShapeDtypeStruct((m, d), F32),
        scratch_shapes=[pltpu.VMEM((tm, d), BF16)],
        compiler_params=_cp("parallel"),
    )(*ins)


def _norm_kernel(x_ref, g_ref, o_ref):
    o_ref[...] = _rms(x_ref[...], g_ref[...])


def _final_norm(x, g, tm):
    m, d = x.shape
    return pl.pallas_call(
        _norm_kernel, grid=(m // tm,), in_specs=[_rows(tm, d), _WHOLE], out_specs=_rows(tm, d),
        out_shape=jax.ShapeDtypeStruct((m, d), F32), compiler_params=_cp("parallel"),
    )(x, g)


def _t5_bucket(rel):
    half = NUM_BUCKETS // 2
    ret = jnp.where(rel > 0, half, 0)
    n = jnp.abs(rel)
    max_exact = half // 2
    nf = jnp.maximum(n, 1).astype(jnp.float32)
    large = max_exact + (jnp.log(nf / max_exact) / math.log(MAX_DISTANCE / max_exact)
                         * (half - max_exact)).astype(jnp.int32)
    large = jnp.minimum(large, half - 1)
    return ret + jnp.where(n < max_exact, n, large)


def _near_bias(rel_bias):
    sl = jnp.arange(KEY_TILE, dtype=jnp.int32)[None, :, None]
    ql = jnp.arange(KEY_TILE, dtype=jnp.int32)[None, None, :]
    off = jnp.array([-KEY_TILE, 0], jnp.int32)[:, None, None]
    rel = sl + off - ql
    far = _t5_bucket(jnp.array(-2 * KEY_TILE, jnp.int32))
    tab = (rel_bias[_t5_bucket(rel)] - rel_bias[far]) * LOG2E
    return jnp.transpose(tab, (3, 0, 1, 2)).astype(F32)


def _seq_context(prev, seq):
    b, _, c = prev.shape
    z = jnp.zeros((b, seq, c), prev.dtype)
    pm1 = z.at[:, 0].set(prev[:, 1])
    pm2 = z.at[:, 0].set(prev[:, 0]).at[:, 1].set(prev[:, 1])
    return pm1.reshape(b * seq, c), pm2.reshape(b * seq, c)


def _pad_keys(cache, new, t_pad):
    b, p, c = cache.shape
    t = new.shape[1]
    pad = jnp.zeros((b, t_pad - p - t, c), BF16)
    return jnp.concatenate([cache.astype(BF16), new.astype(BF16), pad], axis=1)


def _forget_slot_maps(d):
    s_q = np.zeros((3 * LANES, d), np.float32)
    s_k = np.zeros((3 * LANES, d), np.float32)
    c_q = np.zeros((1, d), np.float32)
    c_k = np.zeros((1, d), np.float32)
    for h in range(H_A):
        base = h * HEAD_DIM
        for p in range(3):
            s_q[p * LANES + h, base + 3 + p] = 1.0
            s_k[p * LANES + h, base + p] = -1.0
            c_q[0, base + p] = 1.0
            c_k[0, base + 3 + p] = 1.0
    return jnp.asarray(s_q, BF16), jnp.asarray(s_k, BF16), jnp.asarray(c_q), jnp.asarray(c_k)


def _split3(x):
    def top(v):
        bits = lax.bitcast_convert_type(v, jnp.uint32) & jnp.uint32(0xFFFF0000)
        return lax.bitcast_convert_type(bits, F32)

    hi = top(x)
    mid = top(x - hi)
    lo = x - hi - mid
    return [hi.astype(BF16), mid.astype(BF16), lo.astype(BF16)]


def _feature_major(a, t_pad):
    a = jnp.swapaxes(a, 1, 2)
    return jnp.pad(a, ((0, 0), (0, 0), (0, t_pad - a.shape[2])))


def _key_tiles_feature_major(a, tk):
    b, t, c = a.shape
    return jnp.swapaxes(a.reshape(b, t // tk, tk, c), 2, 3)


def kernel(x_prompt, x_sample, mem_prompt, cache_k_A, cache_v_A, cache_logf_A, cache_k_B, cache_v_B,
           cache_kidx_B, state_conv_C, state_ffconv, cache_mem_k, cache_mem_v, g_mix, g_xa, g_ffn,
           g_mem, g_final, w_in_A, b_f_A, w_out_A, w_in_B, w_out_B, rel_bias, w_in_C, w_conv_C,
           w_out_C, w_q_xa, w_k_xa, w_v_xa, w_o_xa, w_up, w_conv_ff, b_conv_ff, w_down):
    bp, tp, d = x_prompt.shape
    bs, ts, _ = x_sample.shape
    depth = g_mix.shape[0]
    past = cache_k_A.shape[2]
    n_mem = mem_prompt.shape[1]
    assert d == H_A * HEAD_DIM == H_B * HEAD_DIM and ts % SUBLANES == 0 and ts <= KEY_TILE
    assert past % KEY_TILE == 0 and tp % KEY_TILE == 0
    mixers = tuple("ABC"[l % 3] for l in range(depth))
    slot = tuple(mixers[:l].count(mixers[l]) for l in range(depth))
    mp, ms = bp * tp, bs * ts
    tm_p = min(256, tp)
    tm_s = min(ms, 8 * ts)
    tq_a = min(512, tp)
    tk_a = min(256, tp)
    t_all = past + ts
    t_pad = -(-t_all // KEY_TILE) * KEY_TILE
    k_top_p = min(TOPK_MAX, tp // 4)
    k_top_s = min(TOPK_MAX, t_all // 4)
    nb_s = min(4, bs)

    bf = lambda a: a.astype(BF16)
    row = lambda a: a.reshape(1, -1)
    xp = x_prompt.reshape(mp, d)
    xs = x_sample.reshape(ms, d)
    mem = mem_prompt.reshape(bp * n_mem, d)
    bias_near = _near_bias(rel_bias)

    outs = {n: [] for n in ("kA_p", "vA_p", "fA_p", "kB_p", "vB_p", "iB_p", "cC_p", "ff_p", "mk_p",
                            "mv_p", "kA_s", "vA_s", "fA_s", "kB_s", "vB_s", "iB_s", "cC_s", "ff_s")}

    slot_q, slot_k, ones_q, ones_k = _forget_slot_maps(d)

    def fox_prompt(x, g, j):
        w = w_in_A[j]
        w_f = jnp.pad(w[:, 3 * d:], ((0, 0), (0, LANES - H_A)))
        b_f = jnp.pad(b_f_A[j], (0, LANES - H_A)).reshape(1, LANES)
        k, v, lf, qa, ka, vt = _a_in_seq(
            x, g, bf(w[:, :d]), bf(w[:, d:2 * d]), bf(w[:, 2 * d:3 * d]),
            bf(w[:, 2 * d:3 * d].T), bf(w_f), b_f, slot_q, slot_k, ones_q, ones_k,
            tm=tk_a, t_len=tp)
        o = _fox_attention(qa, ka, vt, b=bp, tq=tq_a, tk=tk_a, q_off=0)
        return o, k, v, lf

    def fox_sample(x, g, j, ck, cv, clf):
        b, t, nl = bs, ts, H_A * ts
        assert nl % LANES == 0 and 3 * H_A + 3 <= LANES
        q, k, v, kb, vb, lf = _a_in(x, g, bf(w_in_A[j][:, :3 * d]), bf(w_in_A[j][:, 3 * d:]),
                                    row(b_f_A[j]), min(256, ms))
        t_f = past + LANES
        lf_t = jnp.swapaxes(lf.reshape(b, t, H_A), 1, 2)
        lf_all = jnp.concatenate([jnp.swapaxes(clf, 1, 2).astype(F32), lf_t,
                                  jnp.zeros((b, H_A, LANES - t), F32)], axis=2)
        f_all = _cumsum_lanes(lf_all.reshape(b * H_A, t_f), min(64, b * H_A)).reshape(b, H_A, t_f)
        f3 = _split3(f_all * LOG2E)
        spare = LANES - 3 * H_A - 3
        kslot = jnp.concatenate([jnp.swapaxes(p, 1, 2) for p in f3]
                                + [jnp.ones((b, t_f, 3), BF16), jnp.zeros((b, t_f, spare), BF16)],
                                axis=-1)
        eye = jnp.eye(H_A, dtype=BF16)
        own = jnp.broadcast_to(-jnp.repeat(eye, t, axis=1), (b, H_A, nl))
        fq3 = [p[:, :, past:past + t].reshape(b, 1, nl) for p in f3]
        q_slot = jnp.concatenate([own] * 3 + fq3 + [jnp.zeros((b, spare, nl), BF16)], axis=1)
        q_feat = jnp.einsum("bthd,hg->bhdgt", q.reshape(b, t, H_A, HEAD_DIM), eye)
        qbd = jnp.concatenate([q_feat.reshape(b, d, nl), q_slot], axis=1)
        rows128 = lambda a: jnp.pad(a.reshape(b, t, d), ((0, 0), (0, LANES - t), (0, 0)))
        o = _fox_cache_attention(ck.reshape(b, past, d), cv.reshape(b, past, d), kslot,
                                 rows128(kb), rows128(vb), qbd, ts=t)
        return o.reshape(ms, d), k, v, lf

    def dsa(x, g, j, b, t, cache):
        m = b * t
        w = w_in_B[j]
        dkv = HKV_B * HEAD_DIM
        c0, c2, c3 = d, d + 2 * dkv, d + 2 * dkv + H_IDX * D_IDX
        q, k, v, kb, vb, qi, ki, kib, wi, vt = _b_in(
            x, g, bf(w[:, :c0]), bf(w[:, c0:c2]), bf(w[:, c2:c3]),
            bf(jnp.concatenate([w[:, c3:c3 + D_IDX]] * 2, axis=1)), bf(w[:, c3 + D_IDX:]),
            bf(w[:, c0 + dkv:c2].T), min(256, m))
        if cache is None:
            q_off, k_top, tk_valid, tq_pad = 0, k_top_p, t, t
            k_all, ki_all = kb.reshape(b, t, dkv), kib.reshape(b, t, -1)
        else:
            ck, cv, cki = cache
            q_off, k_top, tk_valid, tq_pad = past, k_top_s, t_all, KEY_TILE
            k_all = _pad_keys(ck.reshape(b, past, dkv), kb.reshape(b, t, dkv), t_pad)
            v_all = _pad_keys(cv.reshape(b, past, dkv), vb.reshape(b, t, dkv), t_pad)
            vt = _key_tiles_feature_major(v_all, KEY_TILE).reshape(-1, dkv, KEY_TILE)
            cki2 = jnp.concatenate([cki, cki], axis=-1)
            ki_all = _pad_keys(cki2, kib.reshape(b, t, -1), t_pad)
        rows = lambda a: jnp.pad(a.reshape(b, t, -1), ((0, 0), (0, tq_pad - t), (0, 0)))
        o = _dsa_attention(rows(q), rows(qi), _feature_major(wi.reshape(b, t, H_IDX), tq_pad),
                           k_all, vt, ki_all, bias_near, q_off=q_off, k_top=k_top,
                           tk_valid=tk_valid)
        return o[:, :t].reshape(m, d), k, v, ki

    for l in range(depth):
        mix, j = mixers[l], slot[l]
        gm, gx, gf = row(g_mix[l]), row(g_xa[l]), row(g_ffn[l])
        mk, mv = _proj(mem, row(g_mem[l]), [bf(w_k_xa[l]), bf(w_v_xa[l])], min(256, bp * n_mem))
        outs["mk_p"].append(mk.reshape(bp, n_mem, XA_HEADS, d // XA_HEADS))
        outs["mv_p"].append(mv.reshape(bp, n_mem, XA_HEADS, d // XA_HEADS))
        mks, mvs = cache_mem_k[l].reshape(bs, n_mem, d), cache_mem_v[l].reshape(bs, n_mem, d)
        post = dict(g=gx, w_q=bf(w_q_xa[l]), w_o=bf(w_o_xa[l]))

        if mix == "A":
            ap, k, v, f = fox_prompt(xp, gm, j)
            outs["kA_p"].append(k.reshape(bp, tp, H_A, HEAD_DIM))
            outs["vA_p"].append(v.reshape(bp, tp, H_A, HEAD_DIM))
            outs["fA_p"].append(f.reshape(bp, tp, H_A))
            a_s, k, v, f = fox_sample(xs, gm, j, cache_k_A[j], cache_v_A[j], cache_logf_A[j])
            outs["kA_s"].append(k.reshape(bs, ts, H_A, HEAD_DIM))
            outs["vA_s"].append(v.reshape(bs, ts, H_A, HEAD_DIM))
            outs["fA_s"].append(f.reshape(bs, ts, H_A))
            w_mix = bf(w_out_A[j])
        elif mix == "B":
            ap, k, v, ki = dsa(xp, gm, j, bp, tp, None)
            outs["kB_p"].append(k.reshape(bp, tp, HKV_B, HEAD_DIM))
            outs["vB_p"].append(v.reshape(bp, tp, HKV_B, HEAD_DIM))
            outs["iB_p"].append(ki.reshape(bp, tp, D_IDX))
            a_s, k, v, ki = dsa(xs, gm, j, bs, ts, (cache_k_B[j], cache_v_B[j], cache_kidx_B[j]))
            outs["kB_s"].append(k.reshape(bs, ts, HKV_B, HEAD_DIM))
            outs["vB_s"].append(v.reshape(bs, ts, HKV_B, HEAD_DIM))
            outs["iB_s"].append(ki.reshape(bs, ts, D_IDX))
            w_mix = bf(w_out_B[j])
        else:
            xp, tail = _sconv_stream(xp, gm, bf(w_in_C[j]), w_conv_C[j], bf(w_out_C[j]),
                                     tm=tm_p, t_len=tp)
            outs["cC_p"].append(tail[:, SUBLANES - 2:, :])
            pm1, pm2 = _seq_context(state_conv_C[j], ts)
            xs, p_all = _sconv_seq(xs, gm, bf(w_in_C[j]), w_conv_C[j], bf(w_out_C[j]), pm1, pm2,
                                   tm=tm_s, seq=ts)
            outs["cC_s"].append(p_all.reshape(bs, ts, d)[:, ts - 2:, :])
            ap = a_s = w_mix = None

        xp = _post(xp, ap, w_mix, mk=mk.reshape(bp, n_mem, d), mv=mv.reshape(bp, n_mem, d),
                   tm=tm_p, nb=1, tiles_per_mem=tp // tm_p, **post)
        xs = _post(xs, a_s, w_mix, mk=mks, mv=mvs, tm=nb_s * ts, nb=nb_s, tiles_per_mem=1, **post)

        wu, wc, bc, wd = bf(w_up[l]), w_conv_ff[l], row(b_conv_ff[l]), bf(w_down[l])
        xp, tail = _ffn_stream(xp, gf, wu, wc, bc, wd, tm=tm_p, t_len=tp)
        outs["ff_p"].append(tail[:, SUBLANES - 2:, :])
        pm1, pm2 = _seq_context(state_ffconv[l], ts)
        xs, up_all = _ffn_seq(xs, gf, wu, wc, bc, wd, pm1, pm2, tm=tm_s, seq=ts)
        outs["ff_s"].append(up_all.reshape(bs, ts, -1)[:, ts - 2:, :])

    y_p = _final_norm(xp, row(g_final), tm_p).reshape(bp, tp, d)
    y_s = _final_norm(xs, row(g_final), min(256, ms)).reshape(bs, ts, d)
    st = lambda n: jnp.stack(outs[n])
    return (y_p, y_s, st("kA_p"), st("vA_p"), st("fA_p"), st("kB_p"), st("vB_p"), st("iB_p"),
            st("cC_p"), st("ff_p"), st("mk_p"), st("mv_p"), st("kA_s"), st("vA_s"), st("fA_s"),
            st("kB_s"), st("vB_s"), st("iB_s"), st("cC_s"), st("ff_s"))
```

```python
import functools
import math

import jax
import jax.numpy as jnp
import numpy as np
from jax import lax
from jax.experimental import pallas as pl
from jax.experimental.pallas import tpu as pltpu

F32, BF16, I32 = jnp.float32, jnp.bfloat16, jnp.int32

CHUNK = 64
H_A = 16
H_B = 16
HKV_B = 4
G_B = H_B // HKV_B
H_IDX = 8
D_IDX = 64
TOPK_MAX = 256
NUM_BUCKETS = 32
MAX_DISTANCE = 128
XA_HEADS = 4
EPS = 1e-6

LANES = 128
SUBLANES = 8
HEAD_DIM = 64
KEY_TILE = 128
VMEM_LIMIT_BYTES = 56 * 1024 * 1024

INT_MIN = -2 ** 31
NEG_INF = float("-inf")
LOG2E = 1.0 / math.log(2.0)
Q_SCALE = HEAD_DIM ** -0.5 * LOG2E

_WHOLE = pl.BlockSpec(memory_space=pltpu.VMEM)


def _cp(*sem):
    return pltpu.CompilerParams(dimension_semantics=sem, vmem_limit_bytes=VMEM_LIMIT_BYTES)


def _rows(tm, n):
    return pl.BlockSpec((tm, n), lambda i: (i, 0))


def _rms(x, g):
    return x * lax.rsqrt(jnp.mean(x * x, axis=-1, keepdims=True) + EPS) * g


def _mm(a, b):
    return jnp.dot(a, b, preferred_element_type=F32)


def _mm_nt(a, b):
    return lax.dot_general(a, b, (((1,), (1,)), ((), ())), preferred_element_type=F32)


def _proj_kernel(x_ref, g_ref, *refs, n_w):
    w_refs, o_refs = refs[:n_w], refs[n_w:]
    h = _rms(x_ref[...], g_ref[...]).astype(BF16)
    for w_ref, o_ref in zip(w_refs, o_refs):
        o_ref[...] = _mm(h, w_ref[...])


def _proj(x, g, ws, tm):
    m, d = x.shape
    return pl.pallas_call(
        functools.partial(_proj_kernel, n_w=len(ws)),
        grid=(m // tm,),
        in_specs=[_rows(tm, d), _WHOLE] + [_WHOLE] * len(ws),
        out_specs=[_rows(tm, w.shape[1]) for w in ws],
        out_shape=[jax.ShapeDtypeStruct((m, w.shape[1]), F32) for w in ws],
        compiler_params=_cp("parallel"),
    )(x, g, *ws)


def _a_in_kernel(x_ref, g_ref, w_ref, wf_ref, bf_ref, q_ref, k_ref, v_ref, kb_ref, vb_ref, lf_ref):
    d = q_ref.shape[-1]
    h = _rms(x_ref[...], g_ref[...]).astype(BF16)
    q_ref[...] = (_mm(h, w_ref[:, 0:d]) * Q_SCALE).astype(BF16)
    k = _mm(h, w_ref[:, d:2 * d])
    k_ref[...] = k
    kb_ref[...] = k.astype(BF16)
    v = _mm(h, w_ref[:, 2 * d:3 * d])
    v_ref[...] = v
    vb_ref[...] = v.astype(BF16)
    fl = _mm(h, wf_ref[...]) + bf_ref[...]
    lf_ref[...] = jnp.minimum(fl, 0.0) - jnp.log1p(jnp.exp(-jnp.abs(fl)))


def _a_in(x, g, w_qkv, w_f, b_f, tm):
    m, d = x.shape
    sds = jax.ShapeDtypeStruct
    return pl.pallas_call(
        _a_in_kernel,
        grid=(m // tm,),
        in_specs=[_rows(tm, d), _WHOLE, _WHOLE, _WHOLE, _WHOLE],
        out_specs=[_rows(tm, d)] * 5 + [_rows(tm, H_A)],
        out_shape=[sds((m, d), BF16), sds((m, d), F32), sds((m, d), F32),
                   sds((m, d), BF16), sds((m, d), BF16), sds((m, H_A), F32)],
        compiler_params=_cp("parallel"),
    )(x, g, w_qkv, w_f, b_f)


def _top_bits(v):
    bits = lax.bitcast_convert_type(v, I32) & jnp.int32(-65536)
    return lax.bitcast_convert_type(bits, F32)


def _a_in_seq_kernel(x_ref, g_ref, wq_ref, wk_ref, wv_ref, wvt_ref, wf_ref, bf_ref, sq_ref, sk_ref,
                     cq_ref, ck_ref, k_ref, v_ref, lf_ref, qa_ref, ka_ref, vt_ref, carry_scr, *, tpb):
    tm = x_ref.shape[0]
    h = _rms(x_ref[...], g_ref[...]).astype(BF16)
    lane = lax.broadcasted_iota(I32, (tm, LANES), 1)

    @pl.when(pl.program_id(0) % tpb == 0)
    def _():
        carry_scr[...] = jnp.zeros(carry_scr.shape, F32)

    fl = _mm(h, wf_ref[...]) + bf_ref[...]
    lf = jnp.minimum(fl, 0.0) - jnp.log1p(jnp.exp(-jnp.abs(fl)))
    lf = jnp.where(lane < H_A, lf, 0.0)
    lf_ref[...] = lf[:, 0:H_A]
    ra = lax.broadcasted_iota(I32, (tm, tm), 0)
    rb = lax.broadcasted_iota(I32, (tm, tm), 1)
    tri = (rb <= ra).astype(F32)
    f_cum = jnp.dot(tri, lf, precision=lax.Precision.HIGHEST,
                    preferred_element_type=F32) + carry_scr[0:1, :]
    carry_scr[...] = jnp.broadcast_to(f_cum[tm - 1:tm, :], carry_scr.shape)
    f2 = f_cum * LOG2E
    hi = _top_bits(f2)
    mid = _top_bits(f2 - hi)
    lo = f2 - hi - mid
    pieces = jnp.concatenate([hi, mid, lo], axis=1).astype(BF16)

    q = _mm(h, wq_ref[...]) * Q_SCALE
    k = _mm(h, wk_ref[...])
    k_ref[...] = k
    slots_q = _mm(pieces, sq_ref[...]) + cq_ref[...]
    slots_k = _mm(pieces, sk_ref[...]) + ck_ref[...]
    for feat, slots, out_ref in ((q, slots_q, qa_ref), (k, slots_k, ka_ref)):
        for hd in range(H_A):
            pair = slice((hd // 2) * LANES, (hd // 2 + 1) * LANES)
            a, b = feat[:, pair], slots[:, pair]
            if hd % 2 == 0:
                b = pltpu.roll(b, HEAD_DIM, 1)
            else:
                a = pltpu.roll(a, HEAD_DIM, 1)
            out_ref[:, hd * LANES:(hd + 1) * LANES] = jnp.where(lane < HEAD_DIM, a, b).astype(BF16)
    v_ref[...] = _mm(h, wv_ref[...])
    vt_ref[0] = _mm_nt(wvt_ref[...], h).astype(BF16)


def _a_in_seq(x, g, w_q, w_k, w_v, w_vt, w_f, b_f, s_q, s_k, c_q, c_k, *, tm, t_len):
    m, d = x.shape
    sds = jax.ShapeDtypeStruct
    kern = functools.partial(_a_in_seq_kernel, tpb=t_len // tm)
    return pl.pallas_call(
        kern,
        grid=(m // tm,),
        in_specs=[_rows(tm, d)] + [_WHOLE] * 11,
        out_specs=[_rows(tm, d), _rows(tm, d), _rows(tm, H_A), _rows(tm, 2 * d), _rows(tm, 2 * d),
                   pl.BlockSpec((1, d, tm), lambda i: (i, 0, 0))],
        out_shape=[sds((m, d), F32), sds((m, d), F32), sds((m, H_A), F32),
                   sds((m, 2 * d), BF16), sds((m, 2 * d), BF16), sds((m // tm, d, tm), BF16)],
        scratch_shapes=[pltpu.VMEM((SUBLANES, LANES), F32)],
        compiler_params=_cp("arbitrary"),
    )(x, g, w_q, w_k, w_v, w_vt, w_f, b_f, s_q, s_k, c_q, c_k)


def _b_in_kernel(x_ref, g_ref, wq_ref, wkv_ref, wqi_ref, wki_ref, wwi_ref, wvt_ref,
                 q_ref, k_ref, v_ref, kb_ref, vb_ref, qi_ref, ki_ref, kib_ref, wi_ref, vt_ref):
    dkv = k_ref.shape[-1]
    h = _rms(x_ref[...], g_ref[...]).astype(BF16)
    n_vt = x_ref.shape[0] // KEY_TILE
    if n_vt == 0:
        vt_ref[...] = jnp.zeros(vt_ref.shape, BF16)
    for c in range(n_vt):
        vt_ref[c] = _mm_nt(wvt_ref[...], h[c * KEY_TILE:(c + 1) * KEY_TILE, :]).astype(BF16)
    q_ref[...] = (_mm(h, wq_ref[...]) * Q_SCALE).astype(BF16)
    kv = _mm(h, wkv_ref[...])
    k_ref[...] = kv[:, 0:dkv]
    kb_ref[...] = kv[:, 0:dkv].astype(BF16)
    v_ref[...] = kv[:, dkv:2 * dkv]
    vb_ref[...] = kv[:, dkv:2 * dkv].astype(BF16)
    qi_ref[...] = (_mm(h, wqi_ref[...]) * (D_IDX ** -0.5)).astype(BF16)
    ki2 = _mm(h, wki_ref[...])
    ki_ref[...] = ki2[:, 0:D_IDX]
    kib_ref[...] = ki2.astype(BF16)
    wi_ref[...] = _mm(h, wwi_ref[...]) * (H_IDX ** -0.5)


def _b_in(x, g, w_q, w_kv, w_qi, w_ki2, w_wi, w_vt, tm):
    m, d = x.shape
    dkv = w_kv.shape[1] // 2
    dqi = w_qi.shape[1]
    n_vt = tm // KEY_TILE
    sds = jax.ShapeDtypeStruct
    return pl.pallas_call(
        _b_in_kernel,
        grid=(m // tm,),
        in_specs=[_rows(tm, d)] + [_WHOLE] * 7,
        out_specs=[_rows(tm, d), _rows(tm, dkv), _rows(tm, dkv), _rows(tm, dkv), _rows(tm, dkv),
                   _rows(tm, dqi), _rows(tm, D_IDX), _rows(tm, 2 * D_IDX), _rows(tm, H_IDX),
                   pl.BlockSpec((max(n_vt, 1), dkv, KEY_TILE), lambda i: (i, 0, 0))],
        out_shape=[sds((m, d), BF16), sds((m, dkv), F32), sds((m, dkv), F32),
                   sds((m, dkv), BF16), sds((m, dkv), BF16), sds((m, dqi), BF16),
                   sds((m, D_IDX), F32), sds((m, 2 * D_IDX), BF16), sds((m, H_IDX), F32),
                   sds((max(n_vt, 1) * (m // tm), dkv, KEY_TILE), BF16)],
        compiler_params=_cp("parallel"),
    )(x, g, w_q, w_kv, w_qi, w_ki2, w_wi, w_vt)


def _cumsum_kernel(x_ref, o_ref):
    n = x_ref.shape[-1]
    a = lax.broadcasted_iota(I32, (LANES, LANES), 0)
    b = lax.broadcasted_iota(I32, (LANES, LANES), 1)
    tri = (a <= b).astype(F32)
    carry = jnp.zeros((x_ref.shape[0], 1), F32)
    for c in range(n // LANES):
        sl = slice(c * LANES, (c + 1) * LANES)
        y = jnp.dot(x_ref[:, sl], tri, precision=lax.Precision.HIGHEST,
                    preferred_element_type=F32) + carry
        o_ref[:, sl] = y
        carry = y[:, LANES - 1:LANES]


def _cumsum_lanes(x, rb):
    r, n = x.shape
    return pl.pallas_call(
        _cumsum_kernel,
        grid=(r // rb,),
        in_specs=[_rows(rb, n)],
        out_specs=_rows(rb, n),
        out_shape=jax.ShapeDtypeStruct((r, n), F32),
        compiler_params=_cp("parallel"),
    )(x)


def _lookahead(units, depth):
    pending = []
    for idx in range(len(units) + depth):
        if idx < len(units):
            pending.append(units[idx][0]())
        if idx >= depth:
            units[idx - depth][1](pending[idx - depth])


def _fox_kernel(qa_ref, ka_ref, vt_ref, o_ref, qt_scr, m_scr, l_scr, acc_scr,
                *, tq, tk, tu, tpi, q_off):
    i = pl.program_id(2)
    for e in range(2):
        qt_scr[e] = qa_ref[:, e * LANES:(e + 1) * LANES].astype(F32).T.astype(BF16)
    m_scr[...] = jnp.full(m_scr.shape, NEG_INF, F32)
    l_scr[...] = jnp.zeros(l_scr.shape, F32)
    acc_scr[...] = jnp.zeros(acc_scr.shape, F32)
    q0 = q_off + i * tq

    def unit(kk, e, c, masked):
        cols = slice(c * tu, (c + 1) * tu)
        slab = slice(e * LANES, (e + 1) * LANES)

        def issue():
            return _mm(ka_ref[pl.ds(kk * tk, tk), slab], qt_scr[e, :, cols])

        def consume(s):
            if masked:
                kpos = kk * tk + lax.broadcasted_iota(I32, (tk, 1), 0)
                qpos = q0 + c * tu + lax.broadcasted_iota(I32, (1, tu), 1)
                s = jnp.where(kpos <= qpos, s, NEG_INF)
            m_old = m_scr[e, :, cols]
            m_new = jnp.maximum(m_old, jnp.max(s, axis=0, keepdims=True))
            alpha = jnp.exp2(m_old - m_new)
            p = jnp.exp2(s - m_new)
            l_scr[e, :, cols] = alpha * l_scr[e, :, cols] + jnp.sum(p, axis=0, keepdims=True)
            m_scr[e, :, cols] = m_new
            ve = vt_ref[kk, e * HEAD_DIM:(e + 1) * HEAD_DIM, :]
            acc_scr[e, :, cols] = acc_scr[e, :, cols] * alpha + _mm(ve, p.astype(BF16))

        return issue, consume

    def full_body(kp, carry):
        _lookahead([unit(tpi * kp + t, e, c, False)
                    for t in range(tpi) for c in range(tq // tu) for e in range(2)], 2)
        return carry

    assert tpi % 2 == 0 and q_off % (2 * tk) == 0 and tq % tk == 0 and tq % tu == 0
    assert tq // tk == 1 or tq % (2 * tk) == 0
    n_full = q0 // tk
    trips = n_full // tpi
    lax.fori_loop(0, trips, full_body, 0)
    for r in range(0, tpi, 2):
        @pl.when(n_full - trips * tpi == r)
        def _():
            tail = [unit(trips * tpi + t, e, c, False)
                    for t in range(r) for c in range(tq // tu) for e in range(2)]
            for t in range(tq // tk):
                for c in range(tq // tu):
                    if t * tk > (c + 1) * tu - 1:
                        continue
                    masked = (t + 1) * tk - 1 > c * tu
                    tail += [unit(n_full + t, e, c, masked) for e in range(2)]
            _lookahead(tail, 2)
    ot = jnp.concatenate([acc_scr[e] / l_scr[e] for e in range(2)], axis=0)
    o_ref[...] = ot.T.astype(BF16)


def _fox_attention(qa, ka, vt, *, b, tq, tk, q_off):
    d = qa.shape[1] // 2
    t_q, t_k = qa.shape[0] // b, ka.shape[0] // b
    nq, nkt = t_q // tq, t_k // tk
    tu = tq
    tpi = 8
    kern = functools.partial(_fox_kernel, tq=tq, tk=tk, tu=tu, tpi=tpi, q_off=q_off)
    return pl.pallas_call(
        kern,
        grid=(b, d // LANES, nq),
        in_specs=[pl.BlockSpec((tq, 2 * LANES), lambda bb, j, i: (bb * nq + i, j)),
                  pl.BlockSpec((t_k, 2 * LANES), lambda bb, j, i: (bb, j)),
                  pl.BlockSpec((nkt, LANES, tk), lambda bb, j, i: (bb, j, 0))],
        out_specs=pl.BlockSpec((tq, LANES), lambda bb, j, i: (bb * nq + i, j)),
        out_shape=jax.ShapeDtypeStruct((b * t_q, d), BF16),
        scratch_shapes=[pltpu.VMEM((2, LANES, tq), BF16),
                        pltpu.VMEM((2, 1, tq), F32), pltpu.VMEM((2, 1, tq), F32),
                        pltpu.VMEM((2, HEAD_DIM, tq), F32)],
        compiler_params=_cp("parallel", "parallel", "arbitrary"),
    )(qa, ka, vt)


def _fox_cache_kernel(ck_ref, cv_ref, kslot_ref, knew_ref, vnew_ref, qbd_ref, o_ref, s_scr, of_scr,
                      *, past, ts, kc):
    d = ck_ref.shape[2]
    nl = qbd_ref.shape[2]
    q_feat, q_slot = qbd_ref[0, 0:d, :], qbd_ref[0, d:d + LANES, :]
    for c in range(past // kc):
        rows = slice(c * kc, (c + 1) * kc)
        s_scr[rows, :] = (_mm(ck_ref[0, rows, :].astype(BF16), q_feat)
                          + _mm(kslot_ref[0, rows, :], q_slot))
    s_new = _mm(knew_ref[0], q_feat) + _mm(kslot_ref[0, past:past + LANES, :], q_slot)
    key_j = lax.broadcasted_iota(I32, (LANES, nl), 0)
    query = lax.broadcasted_iota(I32, (LANES, nl), 1) % ts
    s_scr[past:past + LANES, :] = jnp.where(key_j <= query, s_new, NEG_INF)
    s = s_scr[...]
    p = jnp.exp2(s - jnp.max(s, axis=0, keepdims=True))
    p = p / jnp.sum(p, axis=0, keepdims=True)
    pt = p.T.astype(BF16)
    of_scr[...] = _mm(pt[:, past:past + LANES], vnew_ref[0])
    for c in range(past // kc):
        rows = slice(c * kc, (c + 1) * kc)
        of_scr[...] += _mm(pt[:, rows], cv_ref[0, rows, :].astype(BF16))
    for h in range(d // HEAD_DIM):
        cols = slice(h * HEAD_DIM, (h + 1) * HEAD_DIM)
        o_ref[0, :, cols] = of_scr[h * ts:(h + 1) * ts, cols].astype(BF16)


def _fox_cache_attention(ck, cv, kslot, knew, vnew, qbd, *, ts):
    b, past, d = ck.shape
    nl = qbd.shape[2]
    blk = lambda *s: pl.BlockSpec((1,) + s, lambda i: (i, 0, 0))
    return pl.pallas_call(
        functools.partial(_fox_cache_kernel, past=past, ts=ts, kc=min(512, past)),
        grid=(b,),
        in_specs=[blk(past, d), blk(past, d), blk(past + LANES, LANES), blk(LANES, d), blk(LANES, d),
                  blk(d + LANES, nl)],
        out_specs=blk(ts, d),
        out_shape=jax.ShapeDtypeStruct((b, ts, d), BF16),
        scratch_shapes=[pltpu.VMEM((past + LANES, nl), F32), pltpu.VMEM((nl, d), F32)],
        compiler_params=_cp("parallel"),
    )(ck, cv, kslot, knew, vnew, qbd)


def _dsa_kernel(q_ref, qi_ref, wit_ref, k_ref, vt_ref, ki_ref, bias_ref, o_ref,
                keys_scr, mb_scr, qim_scr, qg_scr, m_scr, l_scr, acc_scr,
                *, q_off, k_top, tk_valid, nkt):
    tq = KEY_TILE
    i = pl.program_id(1)
    q0 = q_off + i * tq
    home = q0 // KEY_TILE
    qpos = q0 + lax.broadcasted_iota(I32, (1, tq), 1)
    chunk_shift = int(math.log2(CHUNK))
    qchunk = lax.shift_right_logical(qpos, chunk_shift)
    lane = lax.broadcasted_iota(I32, (tq, LANES), 1)
    kf = float(k_top)

    for h in range(H_IDX):
        pair = qi_ref[0, :, (h // 2) * LANES:(h // 2 + 1) * LANES].astype(F32)
        pair = jnp.where((lane >= HEAD_DIM) == (h % 2 == 1), pair, 0.0)
        qim_scr[h] = pair.T.astype(BF16)
    for g in range(HKV_B):
        for r in range(G_B):
            hq = g * G_B + r
            slab = q_ref[0, :, (hq // 2) * LANES:(hq // 2 + 1) * LANES].astype(F32)
            if hq % 2 != g % 2:
                slab = pltpu.roll(slab, HEAD_DIM, 1)
            slab = jnp.where((lane >= HEAD_DIM) == (g % 2 == 1), slab, 0.0)
            qg_scr[g, :, r * tq:(r + 1) * tq] = slab.T.astype(BF16)

    def score_unit(kt):
        def issue():
            kit = ki_ref[0, pl.ds(jnp.minimum(kt, nkt - 1) * KEY_TILE, KEY_TILE), :]
            return [_mm(kit, qim_scr[h]) for h in range(H_IDX)]

        def consume(dots):
            acc = jnp.zeros((KEY_TILE, tq), F32)
            for h in range(H_IDX):
                acc = acc + jnp.maximum(dots[h], 0.0) * wit_ref[0, h:h + 1, :]
            acc = jnp.where(acc == 0.0, 0.0, acc)
            bits = lax.bitcast_convert_type(acc, I32)
            key = bits ^ (lax.shift_right_arithmetic(bits, 31) & 0x7FFFFFFF)
            kpos = kt * KEY_TILE + lax.broadcasted_iota(I32, (KEY_TILE, 1), 0)
            adm = (lax.shift_right_logical(kpos, chunk_shift) <= qchunk) & (kpos < tk_valid)
            keys_scr[kt] = jnp.where(adm, key, INT_MIN)

        return issue, consume

    def score_body(kp, carry):
        _lookahead([score_unit(2 * kp), score_unit(2 * kp + 1)], 1)
        return carry

    lax.fori_loop(0, (home + 2) // 2, score_body, 0)
    keys_scr[home + 1] = jnp.full((KEY_TILE, tq), INT_MIN, I32)

    def count(cand, strict):
        def body(kp, a):
            for t in range(2):
                key = keys_scr[2 * kp + t]
                hit = (key > cand) if strict else (key >= cand)
                a = jnp.where(hit, a + 1.0, a)
            return a

        a = lax.fori_loop(0, (home + 2) // 2, body, jnp.zeros((KEY_TILE, tq), F32))
        return jnp.sum(a, axis=0, keepdims=True)

    zero = jnp.zeros((1, tq), I32)
    thr = jnp.where(count(zero, False) >= kf, zero, jnp.full((1, tq), INT_MIN, I32))

    def bit_body(b, t):
        cand = t + lax.shift_left(jnp.int32(1), 30 - b)
        return jnp.where(count(cand, False) >= kf, cand, t)

    thr = lax.fori_loop(0, 31, bit_body, thr)

    no_ties = jnp.max(jnp.abs(count(thr, False) - kf)) == 0.0

    @pl.when(no_ties)
    def _():
        def body(kt, carry):
            key = keys_scr[kt]
            mb_scr[kt] = jnp.where((key >= thr) & (key != INT_MIN), 0.0, NEG_INF)
            return carry

        lax.fori_loop(0, home + 1, body, 0)

    @pl.when(jnp.logical_not(no_ties))
    def _():
        need = kf - count(thr, True)
        ra = lax.broadcasted_iota(I32, (KEY_TILE, KEY_TILE), 0)
        rb = lax.broadcasted_iota(I32, (KEY_TILE, KEY_TILE), 1)
        earlier = (rb < ra).astype(BF16)

        def tie_body(kt, seen):
            key = keys_scr[kt]
            eq = key == thr
            eqf = jnp.where(eq, 1.0, 0.0)
            rank = _mm(earlier, eqf.astype(BF16)) + seen
            sel = ((key > thr) | (eq & (rank < need))) & (key != INT_MIN)
            mb_scr[kt] = jnp.where(sel, 0.0, NEG_INF)
            return seen + jnp.sum(eqf, axis=0, keepdims=True)

        lax.fori_loop(0, home + 1, tie_body, jnp.zeros((1, tq), F32))

    m_scr[...] = jnp.full(m_scr.shape, NEG_INF, F32)
    l_scr[...] = jnp.zeros(l_scr.shape, F32)
    acc_scr[...] = jnp.zeros(acc_scr.shape, F32)

    def unit(kt, g, near):
        def issue():
            ks = k_ref[0, pl.ds(kt * KEY_TILE, KEY_TILE), (g // 2) * LANES:(g // 2 + 1) * LANES]
            return _mm(ks, qg_scr[g])

        def consume(st):
            mb = mb_scr[kt]
            vt = vt_ref[kt, g * HEAD_DIM:(g + 1) * HEAD_DIM, :]
            for r in range(G_B):
                hq = g * G_B + r
                s = st[:, r * tq:(r + 1) * tq] + mb
                if near is not None:
                    s = s + bias_ref[hq, near]
                m_old = m_scr[hq]
                m_new = jnp.maximum(m_old, jnp.max(s, axis=0, keepdims=True))
                m_use = jnp.where(m_new == NEG_INF, 0.0, m_new)
                alpha = jnp.exp2(m_old - m_use)
                p = jnp.exp2(s - m_use)
                l_scr[hq] = alpha * l_scr[hq] + jnp.sum(p, axis=0, keepdims=True)
                m_scr[hq] = m_new
                acc_scr[hq] = acc_scr[hq] * alpha + _mm(vt, p.astype(BF16))

        return issue, consume

    def attend(tiles):
        _lookahead([unit(kt, g, near) for kt, near in tiles for g in range(HKV_B)], 2)

    far_tpi = 8

    def far_body(kp, carry):
        attend([(far_tpi * kp + t, None) for t in range(far_tpi)])
        return carry

    n_far = jnp.maximum(home - 1, 0)
    trips = n_far // far_tpi
    lax.fori_loop(0, trips, far_body, 0)
    for r in range(1, far_tpi):
        @pl.when(n_far - trips * far_tpi == r)
        def _():
            attend([(trips * far_tpi + t, None) for t in range(r)])

    @pl.when(home >= 1)
    def _():
        attend([(home - 1, 0), (home, 1)])

    @pl.when(home == 0)
    def _():
        attend([(home, 1)])

    for s in range(H_B // 2):
        ot = jnp.concatenate([acc_scr[2 * s + e] / l_scr[2 * s + e] for e in range(2)], axis=0)
        o_ref[0, :, s * LANES:(s + 1) * LANES] = ot.T.astype(BF16)


def _dsa_attention(q, qi, wit, k, vt, ki, bias, *, q_off, k_top, tk_valid):
    b, t_q, d = q.shape
    t_k = k.shape[1]
    nkt = t_k // KEY_TILE
    tq = KEY_TILE
    kern = functools.partial(_dsa_kernel, q_off=q_off, k_top=k_top, tk_valid=tk_valid, nkt=nkt)
    qblk = lambda n: pl.BlockSpec((1, tq, n), lambda bb, i: (bb, i, 0))
    kblk = lambda n: pl.BlockSpec((1, t_k, n), lambda bb, i: (bb, 0, 0))
    return pl.pallas_call(
        kern,
        grid=(b, t_q // tq),
        in_specs=[qblk(d), qblk(qi.shape[2]),
                  pl.BlockSpec((1, H_IDX, tq), lambda bb, i: (bb, 0, i)), kblk(k.shape[2]),
                  pl.BlockSpec((nkt, vt.shape[1], KEY_TILE), lambda bb, i: (bb, 0, 0)),
                  kblk(ki.shape[2]), _WHOLE],
        out_specs=qblk(d),
        out_shape=jax.ShapeDtypeStruct((b, t_q, d), BF16),
        scratch_shapes=[pltpu.VMEM((nkt + 1, KEY_TILE, tq), I32), pltpu.VMEM((nkt, KEY_TILE, tq), F32),
                        pltpu.VMEM((H_IDX, LANES, tq), BF16),
                        pltpu.VMEM((HKV_B, LANES, G_B * tq), BF16),
                        pltpu.VMEM((H_B, 1, tq), F32), pltpu.VMEM((H_B, 1, tq), F32),
                        pltpu.VMEM((H_B, HEAD_DIM, tq), F32)],
        compiler_params=_cp("parallel", "arbitrary"),
    )(q, qi, wit, k, vt, ki, bias)


def _conv3(u, w_ref, cols, prev):
    tm = u.shape[0]
    row = lax.broadcasted_iota(I32, (tm, 1), 0)
    r1 = pltpu.roll(u, 1, 0)
    r2 = pltpu.roll(u, 2, 0)
    if prev[0] == "stream":
        carry = prev[1]
        c6, c7 = carry[6:7, :], carry[7:8, :]
        um1 = jnp.where(row == 0, c7, r1)
        um2 = jnp.where(row == 0, c6, jnp.where(row == 1, c7, r2))
    else:
        _, seq, pm1, pm2 = prev
        t = row % seq
        um1 = jnp.where(t == 0, pm1, r1)
        um2 = jnp.where(t < 2, pm2, r2)
    return w_ref[0:1, cols] * um2 + w_ref[1:2, cols] * um1 + w_ref[2:3, cols] * u


def _col_chunks(n, width):
    out, c = [], 0
    while c < n:
        out.append((c, min(width, n - c)))
        c += width
    return out


def _ffn_kernel(*refs, mode, seq, tpb, dff, cw):
    if mode == "stream":
        x_ref, g_ref, wup_ref, wc_ref, bc_ref, wdn_ref, o_ref, tail_ref, carry_scr = refs
    else:
        x_ref, g_ref, wup_ref, wc_ref, bc_ref, wdn_ref, pm1_ref, pm2_ref, o_ref, up_ref = refs
    x = x_ref[...]
    tm = x.shape[0]
    h = _rms(x, g_ref[...]).astype(BF16)
    if mode == "stream":
        @pl.when(pl.program_id(0) % tpb == 0)
        def _():
            carry_scr[...] = jnp.zeros(carry_scr.shape, F32)
    acc = [x]

    def chunk(c0, w):
        def issue():
            return [_mm(h, wup_ref[:, base + c0:base + c0 + w]) for base in (0, dff)]

        def consume(ups):
            ys = []
            for base, up in zip((0, dff), ups):
                cols = slice(base + c0, base + c0 + w)
                if mode == "stream":
                    y = _conv3(up, wc_ref, cols, ("stream", carry_scr[:, cols]))
                    carry_scr[:, cols] = up[tm - SUBLANES:tm, :]
                    tail_ref[0, :, cols] = up[tm - SUBLANES:tm, :]
                else:
                    y = _conv3(up, wc_ref, cols, ("seq", seq, pm1_ref[:, cols], pm2_ref[:, cols]))
                    up_ref[:, cols] = up
                ys.append(y + bc_ref[:, cols])
            gate, val = ys
            act = (gate / (1.0 + jnp.exp(-gate))) * val
            acc[0] = acc[0] + _mm(act.astype(BF16), wdn_ref[c0:c0 + w, :])

        return issue, consume

    _lookahead([chunk(c0, w) for c0, w in _col_chunks(dff, cw)], 3)
    o_ref[...] = acc[0]


def _ffn_stream(x, g, w_up, w_conv, b_conv, w_down, *, tm, t_len):
    m, d = x.shape
    c2 = w_up.shape[1]
    tpb = t_len // tm
    kern = functools.partial(_ffn_kernel, mode="stream", seq=None, tpb=tpb, dff=c2 // 2, cw=256)
    return pl.pallas_call(
        kern,
        grid=(m // tm,),
        in_specs=[_rows(tm, d)] + [_WHOLE] * 5,
        out_specs=[_rows(tm, d), pl.BlockSpec((1, SUBLANES, c2), lambda i: (i // tpb, 0, 0))],
        out_shape=[jax.ShapeDtypeStruct((m, d), F32),
                   jax.ShapeDtypeStruct((m // t_len, SUBLANES, c2), F32)],
        scratch_shapes=[pltpu.VMEM((SUBLANES, c2), F32)],
        compiler_params=_cp("arbitrary"),
    )(x, g, w_up, w_conv, b_conv, w_down)


def _ffn_seq(x, g, w_up, w_conv, b_conv, w_down, pm1, pm2, *, tm, seq):
    m, d = x.shape
    c2 = w_up.shape[1]
    kern = functools.partial(_ffn_kernel, mode="seq", seq=seq, tpb=None, dff=c2 // 2, cw=512)
    return pl.pallas_call(
        kern,
        grid=(m // tm,),
        in_specs=[_rows(tm, d)] + [_WHOLE] * 5 + [_rows(tm, c2), _rows(tm, c2)],
        out_specs=[_rows(tm, d), _rows(tm, c2)],
        out_shape=[jax.ShapeDtypeStruct((m, d), F32), jax.ShapeDtypeStruct((m, c2), F32)],
        compiler_params=_cp("parallel"),
    )(x, g, w_up, w_conv, b_conv, w_down, pm1, pm2)


def _sconv_kernel(*refs, mode, seq, tpb, cw):
    if mode == "stream":
        x_ref, g_ref, win_ref, wc_ref, wout_ref, o_ref, tail_ref, carry_scr = refs
    else:
        x_ref, g_ref, win_ref, wc_ref, wout_ref, pm1_ref, pm2_ref, o_ref, p_ref = refs
    x = x_ref[...]
    tm, d = x.shape
    h = _rms(x, g_ref[...]).astype(BF16)
    if mode == "stream":
        @pl.when(pl.program_id(0) % tpb == 0)
        def _():
            carry_scr[...] = jnp.zeros(carry_scr.shape, F32)
    acc = [x]

    def chunk(c0, w):
        cols = slice(c0, c0 + w)

        def issue():
            return [_mm(h, win_ref[:, base + c0:base + c0 + w]) for base in (0, d, 2 * d)]

        def consume(zs):
            gb, gc, u = zs
            p = gc * u
            if mode == "stream":
                y = _conv3(p, wc_ref, cols, ("stream", carry_scr[:, cols]))
                carry_scr[:, cols] = p[tm - SUBLANES:tm, :]
                tail_ref[0, :, cols] = p[tm - SUBLANES:tm, :]
            else:
                y = _conv3(p, wc_ref, cols, ("seq", seq, pm1_ref[:, cols], pm2_ref[:, cols]))
                p_ref[:, cols] = p
            acc[0] = acc[0] + _mm((gb * y).astype(BF16), wout_ref[c0:c0 + w, :])

        return issue, consume

    _lookahead([chunk(c0, w) for c0, w in _col_chunks(d, cw)], 1)
    o_ref[...] = acc[0]


def _sconv_stream(x, g, w_in, w_conv, w_out, *, tm, t_len):
    m, d = x.shape
    tpb = t_len // tm
    kern = functools.partial(_sconv_kernel, mode="stream", seq=None, tpb=tpb, cw=256)
    return pl.pallas_call(
        kern,
        grid=(m // tm,),
        in_specs=[_rows(tm, d)] + [_WHOLE] * 4,
        out_specs=[_rows(tm, d), pl.BlockSpec((1, SUBLANES, d), lambda i: (i // tpb, 0, 0))],
        out_shape=[jax.ShapeDtypeStruct((m, d), F32),
                   jax.ShapeDtypeStruct((m // t_len, SUBLANES, d), F32)],
        scratch_shapes=[pltpu.VMEM((SUBLANES, d), F32)],
        compiler_params=_cp("arbitrary"),
    )(x, g, w_in, w_conv, w_out)


def _sconv_seq(x, g, w_in, w_conv, w_out, pm1, pm2, *, tm, seq):
    m, d = x.shape
    kern = functools.partial(_sconv_kernel, mode="seq", seq=seq, tpb=None, cw=512)
    return pl.pallas_call(
        kern,
        grid=(m // tm,),
        in_specs=[_rows(tm, d)] + [_WHOLE] * 4 + [_rows(tm, d), _rows(tm, d)],
        out_specs=[_rows(tm, d), _rows(tm, d)],
        out_shape=[jax.ShapeDtypeStruct((m, d), F32), jax.ShapeDtypeStruct((m, d), F32)],
        compiler_params=_cp("parallel"),
    )(x, g, w_in, w_conv, w_out, pm1, pm2)


def _post_kernel(*refs, nb, has_mix):
    if has_mix:
        x_ref, a_ref, wmix_ref, g_ref, wq_ref, mk_ref, mv_ref, wo_ref, o_ref, oc_scr = refs
        x = x_ref[...] + _mm(a_ref[...], wmix_ref[...])
    else:
        x_ref, g_ref, wq_ref, mk_ref, mv_ref, wo_ref, o_ref, oc_scr = refs
        x = x_ref[...]
    tm, d = x.shape
    dh = d // XA_HEADS
    rpb = tm // nb
    h = _rms(x, g_ref[...]).astype(BF16)
    q = (_mm(h, wq_ref[...]) * (dh ** -0.5)).astype(BF16)
    for b in range(nb):
        rows = slice(b * rpb, (b + 1) * rpb)
        for hh in range(XA_HEADS):
            cols = slice(hh * dh, (hh + 1) * dh)
            s = _mm_nt(q[rows, cols], mk_ref[b, :, cols].astype(BF16))
            e = jnp.exp(s - jnp.max(s, axis=1, keepdims=True))
            p = e / jnp.sum(e, axis=1, keepdims=True)
            oc_scr[rows, cols] = _mm(p.astype(BF16), mv_ref[b, :, cols].astype(BF16)).astype(BF16)
    o_ref[...] = x + _mm(oc_scr[...], wo_ref[...])


def _post(x, a, w_mix, g, w_q, mk, mv, w_o, *, tm, nb, tiles_per_mem):
    m, d = x.shape
    n_mem = mk.shape[1]
    has_mix = a is not None
    mem_spec = pl.BlockSpec((nb, n_mem, d), lambda i: (i // tiles_per_mem, 0, 0))
    ins = [x] + ([a, w_mix] if has_mix else []) + [g, w_q, mk, mv, w_o]
    specs = ([_rows(tm, d)] + ([_rows(tm, d), _WHOLE] if has_mix else [])
             + [_WHOLE, _WHOLE, mem_spec, mem_spec, _WHOLE])
    return pl.pallas_call(
        functools.partial(_post_kernel, nb=nb, has_mix=has_mix),
        grid=(m // tm,),
        in_specs=specs,
        out_specs=_rows(tm, d),
        out_shape=jax.ShapeDtypeStruct((m, d), F32),
        scratch_shapes=[pltpu.VMEM((tm, d), BF16)],
        compiler_params=_cp("parallel"),
    )(*ins)


def _norm_kernel(x_ref, g_ref, o_ref):
    o_ref[...] = _rms(x_ref[...], g_ref[...])


def _final_norm(x, g, tm):
    m, d = x.shape
    return pl.pallas_call(
        _norm_kernel, grid=(m // tm,), in_specs=[_rows(tm, d), _WHOLE], out_specs=_rows(tm, d),
        out_shape=jax.ShapeDtypeStruct((m, d), F32), compiler_params=_cp("parallel"),
    )(x, g)


def _t5_bucket(rel):
    half = NUM_BUCKETS // 2
    ret = jnp.where(rel > 0, half, 0)
    n = jnp.abs(rel)
    max_exact = half // 2
    nf = jnp.maximum(n, 1).astype(jnp.float32)
    large = max_exact + (jnp.log(nf / max_exact) / math.log(MAX_DISTANCE / max_exact)
                         * (half - max_exact)).astype(jnp.int32)
    large = jnp.minimum(large, half - 1)
    return ret + jnp.where(n < max_exact, n, large)


def _near_bias(rel_bias):
    sl = jnp.arange(KEY_TILE, dtype=jnp.int32)[None, :, None]
    ql = jnp.arange(KEY_TILE, dtype=jnp.int32)[None, None, :]
    off = jnp.array([-KEY_TILE, 0], jnp.int32)[:, None, None]
    rel = sl + off - ql
    far = _t5_bucket(jnp.array(-2 * KEY_TILE, jnp.int32))
    tab = (rel_bias[_t5_bucket(rel)] - rel_bias[far]) * LOG2E
    return jnp.transpose(tab, (3, 0, 1, 2)).astype(F32)


def _seq_context(prev, seq):
    b, _, c = prev.shape
    z = jnp.zeros((b, seq, c), prev.dtype)
    pm1 = z.at[:, 0].set(prev[:, 1])
    pm2 = z.at[:, 0].set(prev[:, 0]).at[:, 1].set(prev[:, 1])
    return pm1.reshape(b * seq, c), pm2.reshape(b * seq, c)


def _pad_keys(cache, new, t_pad):
    b, p, c = cache.shape
    t = new.shape[1]
    pad = jnp.zeros((b, t_pad - p - t, c), BF16)
    return jnp.concatenate([cache.astype(BF16), new.astype(BF16), pad], axis=1)


def _forget_slot_maps(d):
    s_q = np.zeros((3 * LANES, d), np.float32)
    s_k = np.zeros((3 * LANES, d), np.float32)
    c_q = np.zeros((1, d), np.float32)
    c_k = np.zeros((1, d), np.float32)
    for h in range(H_A):
        base = h * HEAD_DIM
        for p in range(3):
            s_q[p * LANES + h, base + 3 + p] = 1.0
            s_k[p * LANES + h, base + p] = -1.0
            c_q[0, base + p] = 1.0
            c_k[0, base + 3 + p] = 1.0
    return jnp.asarray(s_q, BF16), jnp.asarray(s_k, BF16), jnp.asarray(c_q), jnp.asarray(c_k)


def _split3(x):
    def top(v):
        bits = lax.bitcast_convert_type(v, jnp.uint32) & jnp.uint32(0xFFFF0000)
        return lax.bitcast_convert_type(bits, F32)

    hi = top(x)
    mid = top(x - hi)
    lo = x - hi - mid
    return [hi.astype(BF16), mid.astype(BF16), lo.astype(BF16)]


def _feature_major(a, t_pad):
    a = jnp.swapaxes(a, 1, 2)
    return jnp.pad(a, ((0, 0), (0, 0), (0, t_pad - a.shape[2])))


def _key_tiles_feature_major(a, tk):
    b, t, c = a.shape
    return jnp.swapaxes(a.reshape(b, t // tk, tk, c), 2, 3)


def kernel(x_prompt, x_sample, mem_prompt, cache_k_A, cache_v_A, cache_logf_A, cache_k_B, cache_v_B,
           cache_kidx_B, state_conv_C, state_ffconv, cache_mem_k, cache_mem_v, g_mix, g_xa, g_ffn,
           g_mem, g_final, w_in_A, b_f_A, w_out_A, w_in_B, w_out_B, rel_bias, w_in_C, w_conv_C,
           w_out_C, w_q_xa, w_k_xa, w_v_xa, w_o_xa, w_up, w_conv_ff, b_conv_ff, w_down):
    bp, tp, d = x_prompt.shape
    bs, ts, _ = x_sample.shape
    depth = g_mix.shape[0]
    past = cache_k_A.shape[2]
    n_mem = mem_prompt.shape[1]
    assert d == H_A * HEAD_DIM == H_B * HEAD_DIM and ts % SUBLANES == 0 and ts <= KEY_TILE
    assert past % KEY_TILE == 0 and tp % KEY_TILE == 0
    mixers = tuple("ABC"[l % 3] for l in range(depth))
    slot = tuple(mixers[:l].count(mixers[l]) for l in range(depth))
    mp, ms = bp * tp, bs * ts
    tm_p = min(256, tp)
    tm_s = min(ms, 8 * ts)
    tq_a = min(512, tp)
    tk_a = min(256, tp)
    t_all = past + ts
    t_pad = -(-t_all // KEY_TILE) * KEY_TILE
    k_top_p = min(TOPK_MAX, tp // 4)
    k_top_s = min(TOPK_MAX, t_all // 4)
    nb_s = min(4, bs)

    bf = lambda a: a.astype(BF16)
    row = lambda a: a.reshape(1, -1)
    xp = x_prompt.reshape(mp, d)
    xs = x_sample.reshape(ms, d)
    mem = mem_prompt.reshape(bp * n_mem, d)
    bias_near = _near_bias(rel_bias)

    outs = {n: [] for n in ("kA_p", "vA_p", "fA_p", "kB_p", "vB_p", "iB_p", "cC_p", "ff_p", "mk_p",
                            "mv_p", "kA_s", "vA_s", "fA_s", "kB_s", "vB_s", "iB_s", "cC_s", "ff_s")}

    slot_q, slot_k, ones_q, ones_k = _forget_slot_maps(d)

    def fox_prompt(x, g, j):
        w = w_in_A[j]
        w_f = jnp.pad(w[:, 3 * d:], ((0, 0), (0, LANES - H_A)))
        b_f = jnp.pad(b_f_A[j], (0, LANES - H_A)).reshape(1, LANES)
        k, v, lf, qa, ka, vt = _a_in_seq(
            x, g, bf(w[:, :d]), bf(w[:, d:2 * d]), bf(w[:, 2 * d:3 * d]),
            bf(w[:, 2 * d:3 * d].T), bf(w_f), b_f, slot_q, slot_k, ones_q, ones_k,
            tm=tk_a, t_len=tp)
        o = _fox_attention(qa, ka, vt, b=bp, tq=tq_a, tk=tk_a, q_off=0)
        return o, k, v, lf

    def fox_sample(x, g, j, ck, cv, clf):
        b, t, nl = bs, ts, H_A * ts
        assert nl % LANES == 0 and 3 * H_A + 3 <= LANES
        q, k, v, kb, vb, lf = _a_in(x, g, bf(w_in_A[j][:, :3 * d]), bf(w_in_A[j][:, 3 * d:]),
                                    row(b_f_A[j]), min(256, ms))
        t_f = past + LANES
        lf_t = jnp.swapaxes(lf.reshape(b, t, H_A), 1, 2)
        lf_all = jnp.concatenate([jnp.swapaxes(clf, 1, 2).astype(F32), lf_t,
                                  jnp.zeros((b, H_A, LANES - t), F32)], axis=2)
        f_all = _cumsum_lanes(lf_all.reshape(b * H_A, t_f), min(64, b * H_A)).reshape(b, H_A, t_f)
        f3 = _split3(f_all * LOG2E)
        spare = LANES - 3 * H_A - 3
        kslot = jnp.concatenate([jnp.swapaxes(p, 1, 2) for p in f3]
                                + [jnp.ones((b, t_f, 3), BF16), jnp.zeros((b, t_f, spare), BF16)],
                                axis=-1)
        eye = jnp.eye(H_A, dtype=BF16)
        own = jnp.broadcast_to(-jnp.repeat(eye, t, axis=1), (b, H_A, nl))
        fq3 = [p[:, :, past:past + t].reshape(b, 1, nl) for p in f3]
        q_slot = jnp.concatenate([own] * 3 + fq3 + [jnp.zeros((b, spare, nl), BF16)], axis=1)
        q_feat = jnp.einsum("bthd,hg->bhdgt", q.reshape(b, t, H_A, HEAD_DIM), eye)
        qbd = jnp.concatenate([q_feat.reshape(b, d, nl), q_slot], axis=1)
        rows128 = lambda a: jnp.pad(a.reshape(b, t, d), ((0, 0), (0, LANES - t), (0, 0)))
        o = _fox_cache_attention(ck.reshape(b, past, d), cv.reshape(b, past, d), kslot,
                                 rows128(kb), rows128(vb), qbd, ts=t)
        return o.reshape(ms, d), k, v, lf

    def dsa(x, g, j, b, t, cache):
        m = b * t
        w = w_in_B[j]
        dkv = HKV_B * HEAD_DIM
        c0, c2, c3 = d, d + 2 * dkv, d + 2 * dkv + H_IDX * D_IDX
        q, k, v, kb, vb, qi, ki, kib, wi, vt = _b_in(
            x, g, bf(w[:, :c0]), bf(w[:, c0:c2]), bf(w[:, c2:c3]),
            bf(jnp.concatenate([w[:, c3:c3 + D_IDX]] * 2, axis=1)), bf(w[:, c3 + D_IDX:]),
            bf(w[:, c0 + dkv:c2].T), min(256, m))
        if cache is None:
            q_off, k_top, tk_valid, tq_pad = 0, k_top_p, t, t
            k_all, ki_all = kb.reshape(b, t, dkv), kib.reshape(b, t, -1)
        else:
            ck, cv, cki = cache
            q_off, k_top, tk_valid, tq_pad = past, k_top_s, t_all, KEY_TILE
            k_all = _pad_keys(ck.reshape(b, past, dkv), kb.reshape(b, t, dkv), t_pad)
            v_all = _pad_keys(cv.reshape(b, past, dkv), vb.reshape(b, t, dkv), t_pad)
            vt = _key_tiles_feature_major(v_all, KEY_TILE).reshape(-1, dkv, KEY_TILE)
            cki2 = jnp.concatenate([cki, cki], axis=-1)
            ki_all = _pad_keys(cki2, kib.reshape(b, t, -1), t_pad)
        rows = lambda a: jnp.pad(a.reshape(b, t, -1), ((0, 0), (0, tq_pad - t), (0, 0)))
        o = _dsa_attention(rows(q), rows(qi), _feature_major(wi.reshape(b, t, H_IDX), tq_pad),
                           k_all, vt, ki_all, bias_near, q_off=q_off, k_top=k_top,
                           tk_valid=tk_valid)
        return o[:, :t].reshape(m, d), k, v, ki

    for l in range(depth):
        mix, j = mixers[l], slot[l]
        gm, gx, gf = row(g_mix[l]), row(g_xa[l]), row(g_ffn[l])
        mk, mv = _proj(mem, row(g_mem[l]), [bf(w_k_xa[l]), bf(w_v_xa[l])], min(256, bp * n_mem))
        outs["mk_p"].append(mk.reshape(bp, n_mem, XA_HEADS, d // XA_HEADS))
        outs["mv_p"].append(mv.reshape(bp, n_mem, XA_HEADS, d // XA_HEADS))
        mks, mvs = cache_mem_k[l].reshape(bs, n_mem, d), cache_mem_v[l].reshape(bs, n_mem, d)
        post = dict(g=gx, w_q=bf(w_q_xa[l]), w_o=bf(w_o_xa[l]))

        if mix == "A":
            ap, k, v, f = fox_prompt(xp, gm, j)
            outs["kA_p"].append(k.reshape(bp, tp, H_A, HEAD_DIM))
            outs["vA_p"].append(v.reshape(bp, tp, H_A, HEAD_DIM))
            outs["fA_p"].append(f.reshape(bp, tp, H_A))
            a_s, k, v, f = fox_sample(xs, gm, j, cache_k_A[j], cache_v_A[j], cache_logf_A[j])
            outs["kA_s"].append(k.reshape(bs, ts, H_A, HEAD_DIM))
            outs["vA_s"].append(v.reshape(bs, ts, H_A, HEAD_DIM))
            outs["fA_s"].append(f.reshape(bs, ts, H_A))
            w_mix = bf(w_out_A[j])
        elif mix == "B":
            ap, k, v, ki = dsa(xp, gm, j, bp, tp, None)
            outs["kB_p"].append(k.reshape(bp, tp, HKV_B, HEAD_DIM))
            outs["vB_p"].append(v.reshape(bp, tp, HKV_B, HEAD_DIM))
            outs["iB_p"].append(ki.reshape(bp, tp, D_IDX))
            a_s, k, v, ki = dsa(xs, gm, j, bs, ts, (cache_k_B[j], cache_v_B[j], cache_kidx_B[j]))
            outs["kB_s"].append(k.reshape(bs, ts, HKV_B, HEAD_DIM))
            outs["vB_s"].append(v.reshape(bs, ts, HKV_B, HEAD_DIM))
            outs["iB_s"].append(ki.reshape(bs, ts, D_IDX))
            w_mix = bf(w_out_B[j])
        else:
            xp, tail = _sconv_stream(xp, gm, bf(w_in_C[j]), w_conv_C[j], bf(w_out_C[j]),
                                     tm=tm_p, t_len=tp)
            outs["cC_p"].append(tail[:, SUBLANES - 2:, :])
            pm1, pm2 = _seq_context(state_conv_C[j], ts)
            xs, p_all = _sconv_seq(xs, gm, bf(w_in_C[j]), w_conv_C[j], bf(w_out_C[j]), pm1, pm2,
                                   tm=tm_s, seq=ts)
            outs["cC_s"].append(p_all.reshape(bs, ts, d)[:, ts - 2:, :])
            ap = a_s = w_mix = None

        xp = _post(xp, ap, w_mix, mk=mk.reshape(bp, n_mem, d), mv=mv.reshape(bp, n_mem, d),
                   tm=tm_p, nb=1, tiles_per_mem=tp // tm_p, **post)
        xs = _post(xs, a_s, w_mix, mk=mks, mv=mvs, tm=nb_s * ts, nb=nb_s, tiles_per_mem=1, **post)

        wu, wc, bc, wd = bf(w_up[l]), w_conv_ff[l], row(b_conv_ff[l]), bf(w_down[l])
        xp, tail = _ffn_stream(xp, gf, wu, wc, bc, wd, tm=tm_p, t_len=tp)
        outs["ff_p"].append(tail[:, SUBLANES - 2:, :])
        pm1, pm2 = _seq_context(state_ffconv[l], ts)
        xs, up_all = _ffn_seq(xs, gf, wu, wc, bc, wd, pm1, pm2, tm=tm_s, seq=ts)
        outs["ff_s"].append(up_all.reshape(bs, ts, -1)[:, ts - 2:, :])

    y_p = _final_norm(xp, row(g_final), tm_p).reshape(bp, tp, d)
    y_s = _final_norm(xs, row(g_final), min(256, ms)).reshape(bs, ts, d)
    st = lambda n: jnp.stack(outs[n])
    return (y_p, y_s, st("kA_p"), st("vA_p"), st("fA_p"), st("kB_p"), st("vB_p"), st("iB_p"),
            st("cC_p"), st("ff_p"), st("mk_p"), st("mv_p"), st("kA_s"), st("vA_s"), st("fA_s"),
            st("kB_s"), st("vB_s"), st("iB_s"), st("cC_s"), st("ff_s"))
```
